```python
import jax
import jax.numpy as jnp
from jax import lax
import numpy as np

D_MODEL = 1024
BATCH = 16
SEQ = 2048
DEPTH = 2

CTX_LEN = 256
GRID_W = 64
D_MIX = D_MODEL
ML_HEADS = 4
ML_DV = D_MIX // (2 * ML_HEADS)
ML_DK = ML_DV // 2
ML_CHUNK = 64
CONV_W = 5
NA_HEADS = 8
NA_DH = (D_MIX - ML_HEADS * ML_DV) // NA_HEADS
NA_WIN_H = 8
NA_WIN_W = 16
NA_QB = 16
NA_BAND_W = 2 * NA_WIN_W
ROPE_BASE = 10000.0
PEER_HEADS = 8
PEER_NKEYS = 128
PEER_EXPERTS = PEER_NKEYS * PEER_NKEYS
PEER_TOPK = 16
PEER_DKEY = 256
PEER_TB = 128
N_MOD = 6
ML_QK_W = 2 * ML_HEADS * ML_DK
ML_V_W = ML_HEADS * ML_DV
ML_G_W = 4 * ML_HEADS
NA_W = NA_HEADS * NA_DH
D_IN = ML_QK_W + 2 * ML_V_W + ML_G_W + 3 * NA_W
DEEPNORM_ALPHA = (2.0 * DEPTH) ** 0.25
DEEPNORM_BETA = (8.0 * DEPTH) ** -0.25
LN_EPS = 1e-5

kernel_name = 'hybrid_mlstm_natten_peer_dit'


def _layer_norm(x, w, b):
    xf = x.astype(jnp.float32)
    mu = xf.mean(-1, keepdims=True)
    var = jnp.square(xf - mu).mean(-1, keepdims=True)
    return ((xf - mu) * lax.rsqrt(var + LN_EPS) * w + b).astype(x.dtype)


def _modulation(cond, w, b):
    return jnp.split(jax.nn.silu(cond) @ w + b, N_MOD, axis=-1)


def _heads(t, h):
    return t.reshape(t.shape[0], t.shape[1], h, -1).transpose(0, 2, 1, 3)


def _merge_heads(t):
    return t.transpose(0, 2, 1, 3).reshape(t.shape[0], t.shape[2], -1)


def _split_in(p):
    sizes = (ML_QK_W, ML_V_W, ML_V_W, ML_G_W, NA_W, NA_W, NA_W)
    offs = [0]
    for s in sizes:
        offs.append(offs[-1] + s)
    return [p[..., a:b] for a, b in zip(offs[:-1], offs[1:])]


def _centred_dwconv(x, w):
    return lax.conv_general_dilated(
        x, w[:, None, :].astype(x.dtype), window_strides=(1,),
        padding=[(CONV_W // 2, CONV_W // 2)],
        dimension_numbers=('NWC', 'WIO', 'NWC'), feature_group_count=x.shape[-1])


def _axial_rope(t):
    T, d = t.shape[2], t.shape[-1]
    pos = jnp.arange(T)
    n_freq = d // 4
    inv = ROPE_BASE ** (-jnp.arange(n_freq, dtype=jnp.float32) / n_freq)
    ang = jnp.concatenate([(pos // GRID_W)[:, None] * inv, (pos % GRID_W)[:, None] * inv], -1)
    cos, sin = jnp.cos(ang), jnp.sin(ang)
    t1, t2 = t[..., : d // 2], t[..., d // 2:]
    return jnp.concatenate([t1 * cos - t2 * sin, t1 * sin + t2 * cos], -1).astype(t.dtype)


def _chunked(t):
    return t.reshape(t.shape[:2] + (t.shape[2] // ML_CHUNK, ML_CHUNK) + t.shape[3:])


def _mlstm_states(k, v, log_i, log_f, state):
    kc, vc = _chunked(k), _chunked(v)
    li, lf = _chunked(log_i), _chunked(log_f)
    b = jnp.cumsum(lf, axis=-1)
    b_last = b[..., -1]
    a = b_last[..., None] - b + li
    m_loc = a.max(-1)
    w = jnp.exp(a - m_loc[..., None])
    c_loc = jnp.einsum('bhcs,bhcsv,bhcsk->bhcvk', w, vc, kc)
    n_loc = jnp.einsum('bhcs,bhcsk->bhck', w, kc)

    def step(carry, xs):
        C, n, m = carry
        C_l, n_l, m_l, bl = xs
        m_new = jnp.maximum(bl + m, m_l)
        s_old = jnp.exp(bl + m - m_new)
        s_loc = jnp.exp(m_l - m_new)
        C_new = s_old[..., None, None] * C + s_loc[..., None, None] * C_l
        n_new = s_old[..., None] * n + s_loc[..., None] * n_l
        return (C_new, n_new, m_new), (C, n, m)

    front = lambda t: jnp.moveaxis(t, 2, 0)
    final, (c_in, n_in, m_in) = lax.scan(step, state, (front(c_loc), front(n_loc), front(m_loc), front(b_last)))
    back = lambda t: jnp.moveaxis(t, 0, 2)
    return (back(c_in), back(n_in), back(m_in)), final


def _mlstm_outputs(q, k, v, log_i, log_f, incoming):
    c_in, n_in, m_in = incoming
    qc, kc, vc = _chunked(q), _chunked(k), _chunked(v)
    li, lf = _chunked(log_i), _chunked(log_f)
    b = jnp.cumsum(lf, axis=-1)
    dmat = b[..., :, None] - b[..., None, :] + li[..., None, :]
    lower = jnp.tril(jnp.ones((ML_CHUNK, ML_CHUNK), bool))
    dmat = jnp.where(lower, dmat, -jnp.inf)
    m_inter = b + m_in[..., None]
    m_j = jnp.maximum(m_inter, dmat.max(-1))
    s = jnp.einsum('bhcjk,bhcsk->bhcjs', qc, kc) * jnp.exp(dmat - m_j[..., None])
    inter = jnp.exp(m_inter - m_j)
    num = jnp.einsum('bhcjs,bhcsv->bhcjv', s, vc) + inter[..., None] * jnp.einsum('bhcvk,bhcjk->bhcjv', c_in, qc)
    den = s.sum(-1) + inter * jnp.einsum('bhck,bhcjk->bhcj', n_in, qc)
    h = num / jnp.maximum(jnp.abs(den), jnp.exp(-m_j))[..., None]
    return h.reshape(h.shape[:2] + (-1, h.shape[-1]))


def _mlstm_bidir(qx, kx, vx, gx, qc, kc, vc, gc, need_ctx):
    B_ = qx.shape[0]
    f32 = jnp.float32
    zero = (jnp.zeros((B_, ML_HEADS, ML_DV, ML_DK), f32), jnp.zeros((B_, ML_HEADS, ML_DK), f32),
            jnp.zeros((B_, ML_HEADS), f32))
    hx, hc = [], []
    for d in range(2):
        o = (lambda t: t) if d == 0 else (lambda t: jnp.flip(t, axis=2))
        li_c, lf_c = o(gc[2 * d]), o(jax.nn.log_sigmoid(gc[2 * d + 1]))
        li_x, lf_x = o(gx[2 * d]), o(jax.nn.log_sigmoid(gx[2 * d + 1]))
        inc_c, final_c = _mlstm_states(o(kc), o(vc), li_c, lf_c, zero)
        inc_x, _ = _mlstm_states(o(kx), o(vx), li_x, lf_x, final_c)
        hx.append(o(_mlstm_outputs(o(qx), o(kx), o(vx), li_x, lf_x, inc_x)))
        if need_ctx:
            hc.append(o(_mlstm_outputs(o(qc), o(kc), o(vc), li_c, lf_c, inc_c)))
    return hx[0] + hx[1], (hc[0] + hc[1] if need_ctx else None)


def _head_norm(h, w):
    mu = h.mean(-1, keepdims=True)
    var = jnp.square(h - mu).mean(-1, keepdims=True)
    return _merge_heads((h - mu) * lax.rsqrt(var + LN_EPS)) * w


def _neighbourhood_attention(qx, kx, vx, qc, kc, vc, rpb, need_ctx):
    B_, T, _ = qx.shape
    rows = T // GRID_W
    kh = min(NA_WIN_H, rows)
    scale = NA_DH ** -0.5
    grid = lambda t: t.reshape(B_, rows, GRID_W, NA_HEADS, NA_DH).transpose(0, 3, 1, 2, 4)
    qg, kg, vg = grid(qx * scale), grid(kx), grid(vx)
    qh_c, kh_c, vh_c = _heads(qc * scale, NA_HEADS), _heads(kc, NA_HEADS), _heads(vc, NA_HEADS)
    n_cb = GRID_W // NA_QB
    q_cols = np.arange(GRID_W).reshape(n_cb, NA_QB)
    band_cols = np.clip(q_cols[:, :1] - NA_WIN_W // 2, 0, GRID_W - NA_BAND_W) + np.arange(NA_BAND_W)
    win_lo = np.clip(q_cols - NA_WIN_W // 2, 0, GRID_W - NA_WIN_W)[..., None]
    col_ok = (band_cols[:, None, :] >= win_lo) & (band_cols[:, None, :] < win_lo + NA_WIN_W)
    col_idx = np.clip(band_cols[:, None, :] - q_cols[..., None], 1 - NA_WIN_W, NA_WIN_W - 1) + NA_WIN_W - 1
    n_loc = kh * NA_BAND_W

    def row_block(r):
        r0 = jnp.clip(r - kh // 2, 0, rows - kh)
        k_band = lax.dynamic_slice_in_dim(kg, r0, kh, axis=2)[:, :, :, band_cols]
        v_band = lax.dynamic_slice_in_dim(vg, r0, kh, axis=2)[:, :, :, band_cols]
        q_blk = lax.dynamic_index_in_dim(qg, r, axis=2, keepdims=False).reshape(B_, NA_HEADS, n_cb, NA_QB, NA_DH)
        row_idx = r0 + jnp.arange(kh) - r + NA_WIN_H - 1
        bias = rpb[:, row_idx[None, None, :, None], col_idx[:, :, None, :]]
        s_loc = jnp.einsum('bhcqd,bhicjd->bhcqij', q_blk, k_band).astype(jnp.float32) + bias
        s_loc = jnp.where(col_ok[:, :, None, :], s_loc, -jnp.inf)
        s_ctx = jnp.einsum('bhcqd,bhnd->bhcqn', q_blk, kh_c).astype(jnp.float32)
        p = jax.nn.softmax(jnp.concatenate([s_loc.reshape(s_loc.shape[:4] + (n_loc,)), s_ctx], -1), axis=-1)
        p = p.astype(vx.dtype)
        out = (jnp.einsum('bhcqij,bhicjd->bhcqd', p[..., :n_loc].reshape(s_loc.shape), v_band)
               + jnp.einsum('bhcqn,bhnd->bhcqd', p[..., n_loc:], vh_c))
        return out.reshape(B_, NA_HEADS, GRID_W, NA_DH)

    out = lax.map(row_block, jnp.arange(rows))
    yx = out.transpose(1, 0, 3, 2, 4).reshape(B_, T, NA_W)
    yc = None
    if need_ctx:
        pc = jax.nn.softmax(jnp.einsum('bhnd,bhmd->bhnm', qh_c, kh_c).astype(jnp.float32), axis=-1)
        yc = _merge_heads(jnp.einsum('bhnm,bhmd->bhnd', pc.astype(vc.dtype), vh_c))
    return yx, yc


def _hybrid_mixer(ux, uc, w_in, b_gate, conv_w, ml_norm_w, rpb, w_out, need_ctx):
    f32 = jnp.float32
    px, pc = ux @ w_in, uc @ w_in

    def prep(p, rope):
        qk, v, o, g, nq, nk, nv = _split_in(p)
        qk = jax.nn.silu(_centred_dwconv(qk, conv_w)).astype(f32)
        q = _heads(qk[..., : ML_QK_W // 2], ML_HEADS)
        k = _heads(qk[..., ML_QK_W // 2:], ML_HEADS) * ML_DK ** -0.5
        if rope:
            q, k = _axial_rope(q), _axial_rope(k)
        gates = (g.astype(f32) + b_gate).reshape(g.shape[:2] + (4, ML_HEADS)).transpose(2, 0, 3, 1)
        return q, k, _heads(v.astype(f32), ML_HEADS), o, gates, nq, nk, nv

    qx, kx, vx, ox, gx, nqx, nkx, nvx = prep(px, True)
    qc, kc, vc, oc, gc, nqc, nkc, nvc = prep(pc, False)
    hx, hc = _mlstm_bidir(qx, kx, vx, gx, qc, kc, vc, gc, need_ctx)
    ml_x = (_head_norm(hx, ml_norm_w) * jax.nn.sigmoid(ox.astype(f32))).astype(ux.dtype)
    na_x, na_c = _neighbourhood_attention(nqx, nkx, nvx, nqc, nkc, nvc, rpb, need_ctx)
    yx = jnp.concatenate([ml_x, na_x], -1) @ w_out
    yc = None
    if need_ctx:
        ml_c = (_head_norm(hc, ml_norm_w) * jax.nn.sigmoid(oc.astype(f32))).astype(uc.dtype)
        yc = jnp.concatenate([ml_c, na_c], -1) @ w_out
    return yx, yc


def _peer(u, wq, sub_keys, up, down):
    B_, T, D = u.shape

    def block(ub):
        tb = ub.shape[0]
        q = (ub @ wq).reshape(tb, PEER_HEADS, 2, PEER_DKEY // 2)
        s1 = jnp.einsum('thd,hnd->thn', q[:, :, 0], sub_keys[:, 0])
        s2 = jnp.einsum('thd,hnd->thn', q[:, :, 1], sub_keys[:, 1])
        t1, i1 = lax.top_k(s1, PEER_TOPK)
        t2, i2 = lax.top_k(s2, PEER_TOPK)
        cand = (t1[..., :, None] + t2[..., None, :]).reshape(tb, PEER_HEADS, PEER_TOPK * PEER_TOPK)
        cidx = (i1[..., :, None] * PEER_NKEYS + i2[..., None, :]).reshape(tb, PEER_HEADS, PEER_TOPK * PEER_TOPK)
        top, pos = lax.top_k(cand, PEER_TOPK)
        eidx = jnp.take_along_axis(cidx, pos, -1).reshape(tb, PEER_HEADS * PEER_TOPK)
        g = jax.nn.softmax(top.astype(jnp.float32), axis=-1).reshape(tb, -1).astype(ub.dtype)
        act = jax.nn.gelu(jnp.einsum('td,tkd->tk', ub, up[eidx]))
        return jnp.einsum('tk,tkd->td', g * act, down[eidx])

    y = lax.map(block, u.reshape(B_ * T // PEER_TB, PEER_TB, D))
    return y.reshape(B_, T, D)


def setup_inputs(seed: int = 0) -> dict:
    key = jax.random.key(seed)
    ks = jax.random.split(key, 20)
    f32 = jnp.float32

    def nrm(k, shape, scale):
        return scale * jax.random.normal(k, shape, f32)

    L = DEPTH
    gate_base = jnp.repeat(jnp.array([0.0, 3.0, 0.0, 3.0], f32), ML_HEADS)
    return {
        'x': nrm(ks[0], (BATCH, SEQ, D_MODEL), 1.0),
        'c': nrm(ks[1], (BATCH, D_MODEL), 1.0),
        'ctx': nrm(ks[2], (BATCH, CTX_LEN, D_MODEL), 1.0),
        'c_ctx': nrm(ks[3], (D_MODEL,), 1.0),
        'ada_w': nrm(ks[4], (L, D_MODEL, N_MOD * D_MODEL), D_MODEL ** -0.5),
        'ada_b': nrm(ks[5], (L, N_MOD * D_MODEL), 0.02),
        'w_in': nrm(ks[6], (L, D_MODEL, D_IN), D_MODEL ** -0.5),
        'b_gate': gate_base + nrm(ks[7], (L, ML_G_W), 0.1),
        'conv_w': nrm(ks[8], (L, CONV_W, ML_QK_W), CONV_W ** -0.5),
        'ml_norm_w': 1.0 + nrm(ks[9], (L, ML_V_W), 0.1),
        'na_rpb': nrm(ks[10], (L, NA_HEADS, 2 * NA_WIN_H - 1, 2 * NA_WIN_W - 1), 0.1),
        'w_out': nrm(ks[11], (L, D_MIX, D_MODEL), DEEPNORM_BETA * D_MIX ** -0.5),
        'ln1_w': 1.0 + nrm(ks[12], (L, D_MODEL), 0.1),
        'ln1_b': nrm(ks[13], (L, D_MODEL), 0.02),
        'peer_wq': nrm(ks[14], (L, D_MODEL, PEER_HEADS * PEER_DKEY), D_MODEL ** -0.5),
        'peer_keys': nrm(ks[15], (L, PEER_HEADS, 2, PEER_NKEYS, PEER_DKEY // 2), (PEER_DKEY // 2) ** -0.5),
        'peer_up': nrm(ks[16], (L, PEER_EXPERTS, D_MODEL), D_MODEL ** -0.5),
        'peer_down': nrm(ks[17], (L, PEER_EXPERTS, D_MODEL), DEEPNORM_BETA),
        'ln2_w': 1.0 + nrm(ks[18], (L, D_MODEL), 0.1),
        'ln2_b': nrm(ks[19], (L, D_MODEL), 0.02),
    }


def reference(x, c, ctx, c_ctx, ada_w, ada_b, w_in, b_gate, conv_w, ml_norm_w, na_rpb, w_out,
              ln1_w, ln1_b, peer_wq, peer_keys, peer_up, peer_down, ln2_w, ln2_b):
    for l in range(DEPTH):
        need_ctx = l < DEPTH - 1
        sh1, sc1, g1, sh2, sc2, g2 = _modulation(c[:, None, :], ada_w[l], ada_b[l])
        csh1, csc1, cg1, csh2, csc2, cg2 = _modulation(c_ctx[None, None, :], ada_w[l], ada_b[l])
        ux = x * (1.0 + sc1) + sh1
        uc = ctx * (1.0 + csc1) + csh1
        yx, yc = _hybrid_mixer(ux, uc, w_in[l], b_gate[l], conv_w[l], ml_norm_w[l], na_rpb[l], w_out[l], need_ctx)
        x = _layer_norm(DEEPNORM_ALPHA * x + g1 * yx, ln1_w[l], ln1_b[l])
        fx = _peer(x * (1.0 + sc2) + sh2, peer_wq[l], peer_keys[l], peer_up[l], peer_down[l])
        x = _layer_norm(DEEPNORM_ALPHA * x + g2 * fx, ln2_w[l], ln2_b[l])
        if need_ctx:
            ctx = _layer_norm(DEEPNORM_ALPHA * ctx + cg1 * yc, ln1_w[l], ln1_b[l])
            fc = _peer(ctx * (1.0 + csc2) + csh2, peer_wq[l], peer_keys[l], peer_up[l], peer_down[l])
            ctx = _layer_norm(DEEPNORM_ALPHA * ctx + cg2 * fc, ln2_w[l], ln2_b[l])
    return x
```

```python
import functools

import jax
import jax.numpy as jnp
import numpy as np
from jax import lax
from jax.experimental import pallas as pl
from jax.experimental.pallas import tpu as pltpu

F32 = jnp.float32
BF16 = jnp.bfloat16

D_MODEL = 1024
GRID_W = 64
ML_HEADS = 4
ML_DV = 128
ML_DK = 64
ML_CHUNK = 64
CONV_W = 5
NA_HEADS = 8
NA_DH = 64
NA_WIN_H = 8
NA_WIN_W = 16
ROPE_BASE = 10000.0
PEER_HEADS = 8
PEER_NKEYS = 128
PEER_TOPK = 16
N_MOD = 6
ML_QK_W = 2 * ML_HEADS * ML_DK
ML_V_W = ML_HEADS * ML_DV
ML_G_W = 4 * ML_HEADS
NA_W = NA_HEADS * NA_DH
LN_EPS = 1e-5
NEG_BIG = -1e30

LANES = 128
VMEM_LIMIT = 56 * 1024 * 1024

NN = (((1,), (0,)), ((), ()))
NT = (((1,), (1,)), ((), ()))


def _dg(a, b, dims=NN):
    return lax.dot_general(a, b, dims, preferred_element_type=F32)


def _split_bf16(a):
    hi = a.astype(BF16)
    lo = (a - hi.astype(F32)).astype(BF16)
    return hi, lo


def _dot3(a, b, dims=NN):
    ah, al = _split_bf16(a)
    bh, bl = _split_bf16(b)
    return _dg(ah, bh, dims) + (_dg(ah, bl, dims) + _dg(al, bh, dims))


def _sigmoid(x):
    return 1.0 / (1.0 + jnp.exp(-x))


def _log_sigmoid(x):
    return jnp.minimum(x, 0.0) - jnp.log(1.0 + jnp.exp(-jnp.abs(x)))


def _layer_norm(z, w, b):
    mu = jnp.mean(z, axis=-1, keepdims=True)
    zc = z - mu
    var = jnp.mean(zc * zc, axis=-1, keepdims=True)
    return zc * lax.rsqrt(var + LN_EPS) * w + b


def _params(sem):
    return pltpu.CompilerParams(dimension_semantics=sem, vmem_limit_bytes=VMEM_LIMIT)


def _mod_kernel(c_ref, w_ref, b_ref, o_ref):
    c = c_ref[...]
    s = c * _sigmoid(c)
    o_ref[0] = _dot3(s, w_ref[0]) + b_ref[0]


def _modulation(cond, ada_w, ada_b):
    depth, d, n = ada_w.shape
    rows = cond.shape[0]
    tn = 1024
    return pl.pallas_call(
        _mod_kernel,
        grid=(depth, n // tn),
        in_specs=[
            pl.BlockSpec((rows, d), lambda l, j: (0, 0)),
            pl.BlockSpec((1, d, tn), lambda l, j: (l, 0, j)),
            pl.BlockSpec((1, 1, tn), lambda l, j: (l, 0, j)),
        ],
        out_specs=pl.BlockSpec((1, rows, tn), lambda l, j: (l, 0, j)),
        out_shape=jax.ShapeDtypeStruct((depth, rows, n), F32),
        compiler_params=_params(("parallel", "parallel")),
        name="modulation",
    )(cond, ada_w, ada_b.reshape(depth, 1, n))


def _inproj_kernel(x_ref, sc_ref, sh_ref, wm_ref, wg_ref, wgt_ref,
                   qk_ref, v_ref, o_ref, nq_ref, nk_ref, nv_ref, g_ref, gt_ref):
    u = x_ref[0] * (1.0 + sc_ref[0]) + sh_ref[0]
    p = _dg(u.astype(BF16), wm_ref[...])
    qk_ref[0] = p[:, 0:512]
    v_ref[0] = p[:, 512:1024].astype(BF16)
    o_ref[0] = p[:, 1024:1536].astype(BF16)
    nq_ref[0] = (p[:, 1536:2048] * (NA_DH ** -0.5)).astype(BF16)
    nk_ref[0] = p[:, 2048:2560].astype(BF16)
    nv_ref[0] = p[:, 2560:3072].astype(BF16)
    g_ref[0] = _dot3(u, wg_ref[...])
    gt_ref[0] = _dot3(wgt_ref[...], u, NT)


def _inproj(x, sc, sh, wm, wg, wgt):
    b, t, d = x.shape
    tm = min(t, 512)
    tok = lambda w, dt: jax.ShapeDtypeStruct((b, t, w), dt)
    blk = lambda w: pl.BlockSpec((1, tm, w), lambda i, j: (i, j, 0))
    return pl.pallas_call(
        _inproj_kernel,
        grid=(b, t // tm),
        in_specs=[
            blk(d),
            pl.BlockSpec((1, 1, d), lambda i, j: (i, 0, 0)),
            pl.BlockSpec((1, 1, d), lambda i, j: (i, 0, 0)),
            pl.BlockSpec(wm.shape, lambda i, j: (0, 0)),
            pl.BlockSpec(wg.shape, lambda i, j: (0, 0)),
            pl.BlockSpec(wgt.shape, lambda i, j: (0, 0)),
        ],
        out_specs=[blk(512), blk(512), blk(512), blk(512), blk(512), blk(512), blk(LANES),
                   pl.BlockSpec((1, ML_G_W, tm), lambda i, j: (i, 0, j))],
        out_shape=[tok(512, F32), tok(512, BF16), tok(512, BF16), tok(512, BF16), tok(512, BF16),
                   tok(512, BF16), tok(LANES, F32), jax.ShapeDtypeStruct((b, ML_G_W, t), F32)],
        compiler_params=_params(("parallel", "parallel")),
        name="inproj",
    )(x, sc, sh, wm, wg, wgt)


CONV_TILE = 128


def _conv_silu(qk_ref, cw_ref, pad_ref, t):
    width = qk_ref.shape[-1]
    zero = jnp.zeros((8, width), F32)
    pad_ref[pl.ds(0, 8), :] = zero
    pad_ref[pl.ds(8 + t, 8), :] = zero

    def copy(i, carry):
        r0 = pl.multiple_of(i * CONV_TILE, CONV_TILE)
        pad_ref[pl.ds(r0 + 8, CONV_TILE), :] = qk_ref[0, pl.ds(r0, CONV_TILE), :]
        return carry

    lax.fori_loop(0, t // CONV_TILE, copy, 0)


def _conv_tile(pad_ref, cw_ref, r0):
    n = CONV_TILE + 16
    win = pad_ref[pl.ds(r0, n), :]
    acc = None
    for j in range(CONV_W):
        k = 6 + j
        sh = pltpu.roll(win, n - k, axis=0)[:CONV_TILE]
        term = sh * cw_ref[pl.ds(j, 1), :]
        acc = term if acc is None else acc + term
    return acc * _sigmoid(acc)


def _rope_tile(y, cos, sin):
    lane = lax.broadcasted_iota(jnp.int32, (1, y.shape[1]), 1)
    first = (lane % ML_DK) < (ML_DK // 2)
    n = y.shape[1]
    partner = jnp.where(first, pltpu.roll(y, n - ML_DK // 2, axis=1), pltpu.roll(y, ML_DK // 2, axis=1))
    cos4 = jnp.concatenate([cos] * (n // LANES), axis=1)
    sin4 = jnp.concatenate([sin] * (n // LANES), axis=1)
    return y * cos4 + partner * sin4


def _mlstm_kernel(need_ctx, t_x, t_c,
                  qkx_ref, vx_ref, ox_ref, gx_ref, gtx_ref,
                  qkc_ref, vc_ref, oc_ref, gc_ref, gtc_ref,
                  cos_ref, sin_ref, cw_ref, ks_ref, bg_ref, bgt_ref, nw_ref,
                  *rest):
    if need_ctx:
        mlx_ref, mlc_ref = rest[:2]
        scratch = rest[2:]
    else:
        mlx_ref, mlc_ref = rest[0], None
        scratch = rest[1:]
    padx_ref, padc_ref, qsx_ref, qsc_ref, hfx_ref, hbx_ref, hfc_ref, hbc_ref, st_ref, m_ref = scratch

    _conv_silu(qkx_ref, cw_ref, padx_ref, t_x)
    _conv_silu(qkc_ref, cw_ref, padc_ref, t_c)
    kscale = ks_ref[...]

    def prep_x(i, carry):
        r0 = pl.multiple_of(i * CONV_TILE, CONV_TILE)
        y = _conv_tile(padx_ref, cw_ref, r0)
        y = _rope_tile(y, cos_ref[pl.ds(r0, CONV_TILE), :], sin_ref[pl.ds(r0, CONV_TILE), :])
        qsx_ref[pl.ds(r0, CONV_TILE), :] = y * kscale
        return carry

    def prep_c(i, carry):
        r0 = pl.multiple_of(i * CONV_TILE, CONV_TILE)
        qsc_ref[pl.ds(r0, CONV_TILE), :] = _conv_tile(padc_ref, cw_ref, r0) * kscale
        return carry

    lax.fori_loop(0, t_x // CONV_TILE, prep_x, 0)
    lax.fori_loop(0, t_c // CONV_TILE, prep_c, 0)

    st_ref[...] = jnp.zeros(st_ref.shape, F32)
    m_ref[...] = jnp.zeros(m_ref.shape, F32)

    L = ML_CHUNK
    row = lax.broadcasted_iota(jnp.int32, (L, L), 0)
    col = lax.broadcasted_iota(jnp.int32, (L, L), 1)
    lower = col <= row
    upper = col >= row
    ones_pad = (lax.broadcasted_iota(jnp.int32, (L, ML_DV), 1) == 0).astype(BF16)
    bg = bg_ref[...]
    bgt = bgt_ref[...]

    def chunk_pair(qs_ref, v_ref, g_ref, gt_ref, hf_ref, hb_ref, n_chunks, write_h):
        def body(i, carry):
            for d in range(2):
                c = i if d == 0 else n_chunks - 1 - i
                r0 = pl.multiple_of(c * L, L)
                mask = lower if d == 0 else upper
                mask_t = upper if d == 0 else lower
                qk = qs_ref[pl.ds(r0, L), :]
                vv = v_ref[0, pl.ds(r0, L), :]
                g = g_ref[0, pl.ds(r0, L), :][:, 0:ML_G_W] + bg
                gt = gt_ref[0, c] + bgt
                lsg = _log_sigmoid(g)
                lsgt = _log_sigmoid(gt)
                h_ref = hf_ref if d == 0 else hb_ref
                for h in range(ML_HEADS):
                    ci = 2 * d * ML_HEADS + h
                    fi = (2 * d + 1) * ML_HEADS + h
                    li_col, lf_col = g[:, ci:ci + 1], lsg[:, fi:fi + 1]
                    li_row, lf_row = gt[ci:ci + 1, :], lsgt[fi:fi + 1, :]
                    b_col = jnp.sum(jnp.where(mask, lf_row, 0.0), axis=1, keepdims=True)
                    b_row = jnp.sum(jnp.where(mask_t, lf_col, 0.0), axis=0, keepdims=True)
                    b_last = jnp.sum(lf_row, axis=1, keepdims=True)
                    sidx = d * ML_HEADS + h
                    m_in = m_ref[sidx][:, 0:1]
                    ct = st_ref[sidx]
                    q_h = qk[:, h * ML_DK:(h + 1) * ML_DK]
                    k_h = qk[:, ML_QK_W // 2 + h * ML_DK: ML_QK_W // 2 + (h + 1) * ML_DK]
                    v_ext = jnp.concatenate([vv[:, h * ML_DV:(h + 1) * ML_DV], ones_pad], axis=1)
                    if write_h:
                        dm = jnp.where(mask, b_col - b_row + li_row, NEG_BIG)
                        m_inter = b_col + m_in
                        m_j = jnp.maximum(m_inter, jnp.max(dm, axis=1, keepdims=True))
                        s = _dg(q_h.astype(BF16), k_h.astype(BF16), NT) * jnp.exp(dm - m_j)
                        inter = jnp.exp(m_inter - m_j)
                        ne = _dg(s.astype(BF16), v_ext) + inter * _dg(q_h.astype(BF16), ct.astype(BF16))
                        den = jnp.maximum(jnp.abs(ne[:, ML_DV:ML_DV + 1]), jnp.exp(-m_j))
                        h_ref[pl.ds(r0, L), h * ML_DV:(h + 1) * ML_DV] = ne[:, 0:ML_DV] / den
                    a_col = b_last - b_col + li_col
                    m_loc = jnp.max(a_col, axis=0, keepdims=True)
                    kw = (k_h * jnp.exp(a_col - m_loc)).T.astype(BF16)
                    c_loc = _dg(kw, v_ext)
                    m_new = jnp.maximum(b_last + m_in, m_loc)
                    st_ref[sidx] = jnp.exp(b_last + m_in - m_new) * ct + jnp.exp(m_loc - m_new) * c_loc
                    m_ref[sidx] = jnp.broadcast_to(m_new, (1, LANES))
            return carry

        lax.fori_loop(0, n_chunks, body, 0)

    chunk_pair(qsc_ref, vc_ref, gc_ref, gtc_ref, hfc_ref, hbc_ref, t_c // L, need_ctx)
    chunk_pair(qsx_ref, vx_ref, gx_ref, gtx_ref, hfx_ref, hbx_ref, t_x // L, True)

    nw = nw_ref[...]

    def finish(hf_ref, hb_ref, o_ref, out_ref, t):
        def body(i, carry):
            r0 = pl.multiple_of(i * CONV_TILE, CONV_TILE)
            hsum = hf_ref[pl.ds(r0, CONV_TILE), :] + hb_ref[pl.ds(r0, CONV_TILE), :]
            parts = []
            for h in range(ML_HEADS):
                hh = hsum[:, h * ML_DV:(h + 1) * ML_DV]
                mu = jnp.mean(hh, axis=1, keepdims=True)
                hc = hh - mu
                var = jnp.mean(hc * hc, axis=1, keepdims=True)
                parts.append(hc * lax.rsqrt(var + LN_EPS))
            y = jnp.concatenate(parts, axis=1) * nw
            y = y * _sigmoid(o_ref[0, pl.ds(r0, CONV_TILE), :].astype(F32))
            out_ref[0, pl.ds(r0, CONV_TILE), :] = y.astype(BF16)
            return carry

        lax.fori_loop(0, t // CONV_TILE, body, 0)

    finish(hfx_ref, hbx_ref, ox_ref, mlx_ref, t_x)
    if need_ctx:
        finish(hfc_ref, hbc_ref, oc_ref, mlc_ref, t_c)


def _mlstm(need_ctx, px, pc, cos, sin, conv_w, kscale, b_gate, ml_norm_w):
    qkx, vx, ox, gx, gtx = px
    qkc, vc, oc, gc, gtc = pc
    b, t_x, _ = qkx.shape
    t_c = qkc.shape[1]
    L = ML_CHUNK
    gtx = gtx.reshape(b, ML_G_W, t_x // L, L).transpose(0, 2, 1, 3)
    gtc = gtc.reshape(b, ML_G_W, t_c // L, L).transpose(0, 2, 1, 3)
    cw = jnp.zeros((8, ML_QK_W), F32).at[:CONV_W].set(conv_w)
    tokx = lambda w: pl.BlockSpec((1, t_x, w), lambda i: (i, 0, 0))
    tokc = lambda w: pl.BlockSpec((1, t_c, w), lambda i: (i, 0, 0))
    const = lambda a: pl.BlockSpec(a.shape, lambda i: (0,) * a.ndim)
    bg = b_gate.reshape(1, ML_G_W)
    bgt = b_gate.reshape(ML_G_W, 1)
    nw = ml_norm_w.reshape(1, ML_V_W)
    out_specs = [tokx(ML_V_W)]
    out_shape = [jax.ShapeDtypeStruct((b, t_x, ML_V_W), BF16)]
    if need_ctx:
        out_specs.append(tokc(ML_V_W))
        out_shape.append(jax.ShapeDtypeStruct((b, t_c, ML_V_W), BF16))
    outs = pl.pallas_call(
        functools.partial(_mlstm_kernel, need_ctx, t_x, t_c),
        grid=(b,),
        in_specs=[
            tokx(ML_QK_W), tokx(ML_V_W), tokx(ML_V_W), tokx(LANES),
            pl.BlockSpec((1, t_x // L, ML_G_W, L), lambda i: (i, 0, 0, 0)),
            tokc(ML_QK_W), tokc(ML_V_W), tokc(ML_V_W), tokc(LANES),
            pl.BlockSpec((1, t_c // L, ML_G_W, L), lambda i: (i, 0, 0, 0)),
            const(cos), const(sin), const(cw), const(kscale), const(bg), const(bgt), const(nw),
        ],
        out_specs=out_specs,
        out_shape=out_shape,
        scratch_shapes=[
            pltpu.VMEM((t_x + 16, ML_QK_W), F32), pltpu.VMEM((t_c + 16, ML_QK_W), F32),
            pltpu.VMEM((t_x, ML_QK_W), F32), pltpu.VMEM((t_c, ML_QK_W), F32),
            pltpu.VMEM((t_x, ML_V_W), F32), pltpu.VMEM((t_x, ML_V_W), F32),
            pltpu.VMEM((t_c, ML_V_W), F32), pltpu.VMEM((t_c, ML_V_W), F32),
            pltpu.VMEM((2 * ML_HEADS, ML_DK, 2 * ML_DV), F32),
            pltpu.VMEM((2 * ML_HEADS, 1, LANES), F32),
        ],
        compiler_params=_params(("parallel",)),
        name="mlstm",
    )(qkx, vx, ox, gx, gtx, qkc, vc, oc, gc, gtc, cos, sin, cw, kscale, bg, bgt, nw)
    return (outs[0], outs[1]) if need_ctx else (outs[0], None)


def _rope_tables(t):
    pos = np.arange(t)
    n_freq = ML_DK // 4
    inv = ROPE_BASE ** (-np.arange(n_freq, dtype=np.float32) / n_freq)
    ang = np.concatenate([(pos // GRID_W)[:, None] * inv, (pos % GRID_W)[:, None] * inv], -1).astype(np.float32)
    cos, sin = np.cos(ang), np.sin(ang)
    cos_h = np.concatenate([cos, cos], -1)
    sin_h = np.concatenate([-sin, sin], -1)
    return (jnp.asarray(np.concatenate([cos_h, cos_h], -1), F32),
            jnp.asarray(np.concatenate([sin_h, sin_h], -1), F32))


def _head_masks():
    lane = lax.broadcasted_iota(jnp.int32, (1, 2 * NA_DH), 1)
    return lane < NA_DH, lane >= NA_DH


def _na_kernel(rows, q_ref, k_ref, v_ref, kc_ref, vc_ref, bm_ref, o_ref):
    masks = _head_masks()
    kc = kc_ref[0]
    vc = vc_ref[0]
    kh = NA_WIN_H
    band = kh * GRID_W

    def body(r, carry):
        r0 = jnp.clip(r - kh // 2, 0, rows - kh)
        off = r - r0
        q = q_ref[0, pl.ds(pl.multiple_of(r * GRID_W, GRID_W), GRID_W), :]
        kb = k_ref[0, pl.ds(pl.multiple_of(r0 * GRID_W, GRID_W), band), :]
        vb = v_ref[0, pl.ds(pl.multiple_of(r0 * GRID_W, GRID_W), band), :]
        outs = []
        for hh in range(2):
            qm = jnp.where(masks[hh], q, jnp.zeros_like(q))
            s_loc = _dg(qm, kb, NT) + bm_ref[hh, off]
            s_ctx = _dg(qm, kc, NT)
            m = jnp.maximum(jnp.max(s_loc, axis=1, keepdims=True), jnp.max(s_ctx, axis=1, keepdims=True))
            p_loc = jnp.exp(s_loc - m)
            p_ctx = jnp.exp(s_ctx - m)
            denom = jnp.sum(p_loc, axis=1, keepdims=True) + jnp.sum(p_ctx, axis=1, keepdims=True)
            o = _dg(p_loc.astype(BF16), vb) + _dg(p_ctx.astype(BF16), vc)
            outs.append(o / denom)
        out = jnp.where(masks[0], outs[0], outs[1])
        o_ref[0, pl.ds(pl.multiple_of(r * GRID_W, GRID_W), GRID_W), :] = out.astype(BF16)
        return carry

    lax.fori_loop(0, rows, body, 0)


def _na_latent(nq, nk, nv, nkc, nvc, bias):
    b, t, _ = nq.shape
    t_c = nkc.shape[1]
    rows = t // GRID_W
    tok = lambda tt: pl.BlockSpec((1, tt, 2 * NA_DH), lambda i, j: (i, 0, j))
    return pl.pallas_call(
        functools.partial(_na_kernel, rows),
        grid=(b, NA_HEADS // 2),
        in_specs=[tok(t), tok(t), tok(t), tok(t_c), tok(t_c),
                  pl.BlockSpec((2,) + bias.shape[1:], lambda i, j: (j, 0, 0, 0))],
        out_specs=tok(t),
        out_shape=jax.ShapeDtypeStruct((b, t, NA_W), BF16),
        compiler_params=_params(("parallel", "parallel")),
        name="na_latent",
    )(nq, nk, nv, nkc, nvc, bias)


def _nactx_kernel(q_ref, k_ref, v_ref, o_ref):
    masks = _head_masks()
    q, k, v = q_ref[0], k_ref[0], v_ref[0]
    outs = []
    for hh in range(2):
        qm = jnp.where(masks[hh], q, jnp.zeros_like(q))
        s = _dg(qm, k, NT)
        p = jnp.exp(s - jnp.max(s, axis=1, keepdims=True))
        outs.append(_dg(p.astype(BF16), v) / jnp.sum(p, axis=1, keepdims=True))
    o_ref[0] = jnp.where(masks[0], outs[0], outs[1]).astype(BF16)


def _na_ctx(nqc, nkc, nvc):
    b, t_c, _ = nqc.shape
    tok = pl.BlockSpec((1, t_c, 2 * NA_DH), lambda i, j: (i, 0, j))
    return pl.pallas_call(
        _nactx_kernel,
        grid=(b, NA_HEADS // 2),
        in_specs=[tok, tok, tok],
        out_specs=tok,
        out_shape=jax.ShapeDtypeStruct((b, t_c, NA_W), BF16),
        compiler_params=_params(("parallel", "parallel")),
        name="na_ctx",
    )(nqc, nkc, nvc)


def _na_bias(rpb, rows):
    kh = min(NA_WIN_H, rows)
    off = np.arange(kh)
    row_idx = np.arange(kh)[None, :] - off[:, None] + NA_WIN_H - 1
    c = np.arange(GRID_W)
    win_lo = np.clip(c - NA_WIN_W // 2, 0, GRID_W - NA_WIN_W)
    ok = (c[None, :] >= win_lo[:, None]) & (c[None, :] < win_lo[:, None] + NA_WIN_W)
    col_idx = np.clip(c[None, :] - c[:, None], 1 - NA_WIN_W, NA_WIN_W - 1) + NA_WIN_W - 1
    bias = rpb[:, row_idx[:, None, :, None], col_idx[None, :, None, :]]
    bias = jnp.where(ok[None, None, :, None, :], bias, NEG_BIG)
    return bias.reshape(NA_HEADS, kh, GRID_W, kh * GRID_W)


def _outproj_kernel(alpha, ml_ref, na_ref, x_ref, g1_ref, w_ref, lw_ref, lb_ref, sc_ref, sh_ref,
                    x1_ref, u2_ref):
    a = jnp.concatenate([ml_ref[0], na_ref[0]], axis=1)
    y = _dg(a, w_ref[...])
    x1 = _layer_norm(alpha * x_ref[0] + g1_ref[0] * y, lw_ref[...], lb_ref[...])
    x1_ref[0] = x1
    u2_ref[0] = (x1 * (1.0 + sc_ref[0]) + sh_ref[0]).astype(BF16)


def _outproj(alpha, ml, na, x, g1, w_out, lw, lb, sc2, sh2):
    b, t, d = x.shape
    tm = min(t, 512)
    blk = lambda w: pl.BlockSpec((1, tm, w), lambda i, j: (i, j, 0))
    per_b = pl.BlockSpec((1, 1, d), lambda i, j: (i, 0, 0))
    const = lambda a: pl.BlockSpec(a.shape, lambda i, j: (0,) * a.ndim)
    return pl.pallas_call(
        functools.partial(_outproj_kernel, alpha),
        grid=(b, t // tm),
        in_specs=[blk(ML_V_W), blk(NA_W), blk(d), per_b, const(w_out), const(lw), const(lb), per_b, per_b],
        out_specs=[blk(d), blk(d)],
        out_shape=[jax.ShapeDtypeStruct((b, t, d), F32), jax.ShapeDtypeStruct((b, t, d), BF16)],
        compiler_params=_params(("parallel", "parallel")),
        name="outproj",
    )(ml, na, x, g1, w_out, lw, lb, sc2, sh2)


ROUTE_TB = 128
CAND_ROWS = 16 + 7 * 8 + 8


def _cand_tables():
    a_idx = np.zeros(CAND_ROWS, np.int64)
    b_idx = np.zeros(CAND_ROWS, np.int64)
    a_idx[0:16], b_idx[0:16] = 0, np.arange(16)
    for a in range(1, 8):
        a_idx[16 + 8 * (a - 1): 24 + 8 * (a - 1)] = a
        b_idx[16 + 8 * (a - 1): 24 + 8 * (a - 1)] = np.arange(8)
    a_idx[72:80], b_idx[72:80] = np.arange(8, 16), 0
    valid = (a_idx + 1) * (b_idx + 1) <= PEER_TOPK
    flat = (a_idx * PEER_TOPK + b_idx).astype(np.float32)
    neg = np.where(valid, 0.0, -np.inf).astype(np.float32)
    tile = lambda v: jnp.asarray(np.tile(v[:, None], (1, ROUTE_TB)), F32)
    return tile(flat), tile(neg)


def _route_kernel(x1_ref, sc_ref, sh_ref, wqh_ref, wql_ref, kh_ref, kl_ref, flat_ref, neg_ref,
                  lim_ref, ea_ref, r2_ref, eb_ref, qt_ref, rank_ref, t_ref, e_ref):
    tb = x1_ref.shape[0]
    nk = PEER_NKEYS
    u = x1_ref[...] * (1.0 + sc_ref[0]) + sh_ref[0]
    uh, ul = _split_bf16(u)
    wqh = wqh_ref[...]
    qt_ref[...] = _dg(wqh, uh, NT) + (_dg(wqh, ul, NT) + _dg(wql_ref[...], uh, NT))
    key_iota = lax.broadcasted_iota(jnp.int32, (nk, tb), 0).astype(F32)

    def sub_scores(hh, carry):
        q = qt_ref[pl.ds(pl.multiple_of(hh * nk, nk), nk), :]
        qh, ql = _split_bf16(q)
        kh = kh_ref[hh]
        s0 = _dg(kh, qh) + (_dg(kh, ql) + _dg(kl_ref[hh], qh))
        s = s0
        rank = jnp.full((nk, tb), float(PEER_TOPK), F32)
        tops = []
        for k in range(PEER_TOPK):
            m = jnp.max(s, axis=0, keepdims=True)
            first = jnp.min(jnp.where(s == m, key_iota, float(nk)), axis=0, keepdims=True)
            sel = key_iota == first
            rank = jnp.where(sel, float(k), rank)
            s = jnp.where(sel, -jnp.inf, s)
            tops.append(m)
        rank_ref[hh] = rank
        t_ref[hh] = jnp.concatenate(tops, axis=0)
        e_ref[hh] = jnp.exp(s0 - tops[0])
        return carry

    lax.fori_loop(0, 2 * PEER_HEADS, sub_scores, 0)

    flat = flat_ref[...]
    neg = neg_ref[...]

    def joint(h, carry):
        t1 = t_ref[2 * h]
        t2 = t_ref[2 * h + 1]
        blocks = [t1[0:1] + t2]
        for a in range(1, 8):
            blocks.append(t1[a:a + 1] + t2[0:8])
        blocks.append(t1[8:16] + t2[0:1])
        cand0 = jnp.concatenate(blocks, axis=0) + neg
        cand = cand0
        chosen = jnp.zeros(cand.shape, F32)
        for k in range(PEER_TOPK):
            m = jnp.max(cand, axis=0, keepdims=True)
            first = jnp.min(jnp.where(cand == m, flat, 1e9), axis=0, keepdims=True)
            sel = flat == first
            chosen = jnp.where(sel, 1.0, chosen)
            cand = jnp.where(sel, -jnp.inf, cand)
        top = t1[0:1] + t2[0:1]
        z = jnp.sum(chosen * jnp.exp(cand0 - top), axis=0, keepdims=True)
        counts = [jnp.sum(chosen[0:16], axis=0, keepdims=True)]
        for a in range(1, 8):
            counts.append(jnp.sum(chosen[16 + 8 * (a - 1): 24 + 8 * (a - 1)], axis=0, keepdims=True))
        tail = chosen[72:80]
        for a in range(8):
            counts.append(tail[a:a + 1])
        rank1 = rank_ref[2 * h]
        lim = jnp.zeros((nk, tb), F32)
        for a in range(PEER_TOPK):
            lim = jnp.where(rank1 == float(a), counts[a], lim)
        lim_ref[h] = lim
        ea_ref[h] = e_ref[2 * h] * (1.0 / z)
        r2_ref[h] = rank_ref[2 * h + 1].astype(BF16)
        eb_ref[h] = e_ref[2 * h + 1].astype(BF16)
        return carry

    lax.fori_loop(0, PEER_HEADS, joint, 0)


def _route(x1, sc2, sh2, wqh, wql, kh, kl, tokens_per_batch):
    n, d = x1.shape
    tb = ROUTE_TB
    flat, neg = _cand_tables()
    nk = PEER_NKEYS
    per_b = pl.BlockSpec((1, 1, d), lambda i: (i * tb // tokens_per_batch, 0, 0))
    const = lambda a: pl.BlockSpec(a.shape, lambda i: (0,) * a.ndim)
    out_blk = pl.BlockSpec((PEER_HEADS, nk, tb), lambda i: (0, 0, i))
    meta = lambda dt: jax.ShapeDtypeStruct((PEER_HEADS, nk, n), dt)
    return pl.pallas_call(
        _route_kernel,
        grid=(n // tb,),
        in_specs=[pl.BlockSpec((tb, d), lambda i: (i, 0)), per_b, per_b,
                  const(wqh), const(wql), const(kh), const(kl), const(flat), const(neg)],
        out_specs=[out_blk, out_blk, out_blk, out_blk],
        out_shape=[meta(F32), meta(F32), meta(BF16), meta(BF16)],
        scratch_shapes=[
            pltpu.VMEM((2 * PEER_HEADS * nk, tb), F32),
            pltpu.VMEM((2 * PEER_HEADS, nk, tb), F32),
            pltpu.VMEM((2 * PEER_HEADS, PEER_TOPK, tb), F32),
            pltpu.VMEM((2 * PEER_HEADS, nk, tb), F32),
        ],
        compiler_params=_params(("parallel",)),
        name="peer_route",
    )(x1, sc2, sh2, wqh, wql, kh, kl, flat, neg)


PEER_TB = 512
PEER_EB = 1024


def _gelu_tanh(x):
    return 0.5 * x * (1.0 + jnp.tanh(0.7978845608028654 * (x + 0.044715 * (x * x * x))))


def _peer_kernel(alpha, n_eb, u2_ref, up_ref, dnt_ref, lim_ref, ea_ref, r2_ref, eb_ref,
                 x1_ref, g2_ref, lw_ref, lb_ref, out_ref, acc_ref, ht_ref, z_ref):
    e = pl.program_id(1)
    nk = PEER_NKEYS
    tb = u2_ref.shape[0]

    @pl.when(e == 0)
    def _():
        acc_ref[...] = jnp.zeros(acc_ref.shape, F32)

    ht_ref[...] = _dg(up_ref[...], u2_ref[...], NT)

    def per_key(j, carry):
        r0 = pl.multiple_of(j * nk, nk)
        act = _gelu_tanh(ht_ref[pl.ds(r0, nk), :])
        gate = jnp.zeros((nk, tb), BF16)
        for h in range(PEER_HEADS):
            lim = jnp.broadcast_to(lim_ref[h, pl.ds(j, 1), :], (nk, tb)).astype(BF16)
            ea = jnp.broadcast_to(ea_ref[h, pl.ds(j, 1), :], (nk, tb)).astype(BF16)
            gate = gate + jnp.where(r2_ref[h] < lim, eb_ref[h], jnp.zeros((nk, tb), BF16)) * ea
        z_ref[pl.ds(r0, nk), :] = (act * gate.astype(F32)).astype(BF16)
        return carry

    lax.fori_loop(0, PEER_EB // nk, per_key, 0)
    acc_ref[...] += _dg(dnt_ref[...], z_ref[...])

    @pl.when(e == n_eb - 1)
    def _():
        y = acc_ref[...].T
        out_ref[...] = _layer_norm(alpha * x1_ref[...] + g2_ref[0] * y, lw_ref[...], lb_ref[...])


def _peer_experts(alpha, u2, up, dnt, lim, ea, r2, eb, x1, g2, lw, lb, tokens_per_batch):
    n, d = x1.shape
    tb = min(PEER_TB, tokens_per_batch)
    n_exp = up.shape[0]
    n_eb = n_exp // PEER_EB
    nk = PEER_NKEYS
    keys_per_step = PEER_EB // nk
    tok = pl.BlockSpec((tb, d), lambda i, e: (i, 0))
    row_meta = pl.BlockSpec((PEER_HEADS, keys_per_step, tb), lambda i, e: (0, e, i))
    col_meta = pl.BlockSpec((PEER_HEADS, nk, tb), lambda i, e: (0, 0, i))
    const = lambda a: pl.BlockSpec(a.shape, lambda i, e: (0,) * a.ndim)
    return pl.pallas_call(
        functools.partial(_peer_kernel, alpha, n_eb),
        grid=(n // tb, n_eb),
        in_specs=[tok,
                  pl.BlockSpec((PEER_EB, d), lambda i, e: (e, 0)),
                  pl.BlockSpec((d, PEER_EB), lambda i, e: (0, e)),
                  row_meta, row_meta, col_meta, col_meta,
                  tok,
                  pl.BlockSpec((1, 1, d), lambda i, e: (i * tb // tokens_per_batch, 0, 0)),
                  const(lw), const(lb)],
        out_specs=tok,
        out_shape=jax.ShapeDtypeStruct((n, d), F32),
        scratch_shapes=[pltpu.VMEM((d, tb), F32), pltpu.VMEM((PEER_EB, tb), F32), pltpu.VMEM((PEER_EB, tb), BF16)],
        compiler_params=_params(("parallel", "arbitrary")),
        name="peer_experts",
    )(u2, up, dnt, lim, ea, r2, eb, x1, g2, lw, lb)


def kernel(x, c, ctx, c_ctx, ada_w, ada_b, w_in, b_gate, conv_w, ml_norm_w, na_rpb, w_out, ln1_w, ln1_b,
           peer_wq, peer_keys, peer_up, peer_down, ln2_w, ln2_b):
    depth = ada_w.shape[0]
    b, t, d = x.shape
    t_c = ctx.shape[1]
    alpha = (2.0 * depth) ** 0.25
    rows = t // GRID_W

    n_cond = b + 1
    cond = jnp.zeros((-(-n_cond // 8) * 8, d), F32).at[:b].set(c).at[b].set(c_ctx)
    mods = _modulation(cond, ada_w, ada_b)
    cos, sin = _rope_tables(t)
    kscale = jnp.concatenate([jnp.ones((1, ML_QK_W // 2), F32), jnp.full((1, ML_QK_W // 2), ML_DK ** -0.5, F32)], 1)

    g_lo = ML_QK_W + 2 * ML_V_W
    g_hi = g_lo + ML_G_W
    for l in range(depth):
        need_ctx = l < depth - 1
        mod = mods[l].reshape(-1, N_MOD, d)
        lat = [mod[:b, i].reshape(b, 1, d) for i in range(N_MOD)]
        cxm = [jnp.broadcast_to(mod[b, i].reshape(1, 1, d), (b, 1, d)) for i in range(N_MOD)]

        w = w_in[l]
        wm = jnp.concatenate([w[:, :g_lo], w[:, g_hi:]], axis=1).astype(BF16)
        wg = jnp.zeros((d, LANES), F32).at[:, :ML_G_W].set(w[:, g_lo:g_hi])
        wgt = w[:, g_lo:g_hi].T

        qkx, vx, ox, nqx, nkx, nvx, gx, gtx = _inproj(x, lat[1], lat[0], wm, wg, wgt)
        qkc, vc, oc, nqc, nkc, nvc, gc, gtc = _inproj(ctx, cxm[1], cxm[0], wm, wg, wgt)

        ml_x, ml_c = _mlstm(need_ctx, (qkx, vx, ox, gx, gtx), (qkc, vc, oc, gc, gtc),
                            cos, sin, conv_w[l], kscale, b_gate[l], ml_norm_w[l])
        na_x = _na_latent(nqx, nkx, nvx, nkc, nvc, _na_bias(na_rpb[l], rows))

        wo = w_out[l].astype(BF16)
        lw1, lb1 = ln1_w[l].reshape(1, d), ln1_b[l].reshape(1, d)
        lw2, lb2 = ln2_w[l].reshape(1, d), ln2_b[l].reshape(1, d)
        wq_t = peer_wq[l].T
        wqh = wq_t.astype(BF16)
        wql = (wq_t - wqh.astype(F32)).astype(BF16)
        keys = peer_keys[l].reshape(2 * PEER_HEADS, PEER_NKEYS, -1)
        kh = keys.astype(BF16)
        kl = (keys - kh.astype(F32)).astype(BF16)
        up = peer_up[l].astype(BF16)
        dnt = peer_down[l].T.astype(BF16)

        def channel(xin, ml, na, m, tokens):
            x1, u2 = _outproj(alpha, ml, na, xin, m[2], wo, lw1, lb1, m[4], m[3])
            x1f = x1.reshape(-1, d)
            lim, ea, r2, eb = _route(x1f, m[4], m[3], wqh, wql, kh, kl, tokens)
            out = _peer_experts(alpha, u2.reshape(-1, d), up, dnt, lim, ea, r2, eb, x1f, m[5], lw2, lb2, tokens)
            return out.reshape(xin.shape)

        x_new = channel(x, ml_x, na_x, lat, t)
        if need_ctx:
            na_c = _na_ctx(nqc, nkc, nvc)
            ctx = channel(ctx, ml_c, na_c, cxm, t_c)
        x = x_new
    return x
```

```python
import functools

import jax
import jax.numpy as jnp
import numpy as np
from jax import lax
from jax.experimental import pallas as pl
from jax.experimental.pallas import tpu as pltpu

F32 = jnp.float32
BF16 = jnp.bfloat16

D_MODEL = 1024
GRID_W = 64
ML_HEADS = 4
ML_DV = 128
ML_DK = 64
ML_CHUNK = 64
CONV_W = 5
NA_HEADS = 8
NA_DH = 64
NA_WIN_H = 8
NA_WIN_W = 16
ROPE_BASE = 10000.0
PEER_HEADS = 8
PEER_NKEYS = 128
PEER_TOPK = 16
N_MOD = 6
ML_QK_W = 2 * ML_HEADS * ML_DK
ML_V_W = ML_HEADS * ML_DV
ML_G_W = 4 * ML_HEADS
NA_W = NA_HEADS * NA_DH
LN_EPS = 1e-5
NEG_BIG = -1e30

LANES = 128
SUBLANES = 8
VMEM_LIMIT = 56 * 1024 * 1024

NN = (((1,), (0,)), ((), ()))
NT = (((1,), (1,)), ((), ()))


def _dg(a, b, dims=NN):
    return lax.dot_general(a, b, dims, preferred_element_type=F32)


def _split_bf16(a):
    hi = a.astype(BF16)
    lo = (a - hi.astype(F32)).astype(BF16)
    return hi, lo


def _dot3(a, b, dims=NN):
    ah, al = _split_bf16(a)
    bh, bl = _split_bf16(b)
    return _dg(ah, bh, dims) + (_dg(ah, bl, dims) + _dg(al, bh, dims))


def _sigmoid(x):
    return 1.0 / (1.0 + jnp.exp(-x))


def _log_sigmoid(x):
    return jnp.minimum(x, 0.0) - jnp.log(1.0 + jnp.exp(-jnp.abs(x)))


def _layer_norm(z, w, b):
    mu = jnp.mean(z, axis=-1, keepdims=True)
    zc = z - mu
    var = jnp.mean(zc * zc, axis=-1, keepdims=True)
    return zc * lax.rsqrt(var + LN_EPS) * w + b


def _params(sem):
    return pltpu.CompilerParams(dimension_semantics=sem, vmem_limit_bytes=VMEM_LIMIT)


def _mod_kernel(c_ref, w_ref, b_ref, o_ref):
    c = c_ref[...]
    s = c * _sigmoid(c)
    o_ref[0] = _dot3(s, w_ref[0]) + b_ref[0]


def _modulation(cond, ada_w, ada_b):
    depth, d, n = ada_w.shape
    rows = cond.shape[0]
    tn = 1024
    return pl.pallas_call(
        _mod_kernel,
        grid=(depth, n // tn),
        in_specs=[
            pl.BlockSpec((rows, d), lambda l, j: (0, 0)),
            pl.BlockSpec((1, d, tn), lambda l, j: (l, 0, j)),
            pl.BlockSpec((1, 1, tn), lambda l, j: (l, 0, j)),
        ],
        out_specs=pl.BlockSpec((1, rows, tn), lambda l, j: (l, 0, j)),
        out_shape=jax.ShapeDtypeStruct((depth, rows, n), F32),
        compiler_params=_params(("parallel", "parallel")),
        name="modulation",
    )(cond, ada_w, ada_b.reshape(depth, 1, n))


def _inproj_kernel(x_ref, sc_ref, sh_ref, wm_ref, wg_ref, wgt_ref,
                   qk_ref, v_ref, o_ref, nq_ref, nk_ref, nv_ref, g_ref, gt_ref):
    u = x_ref[0] * (1.0 + sc_ref[0]) + sh_ref[0]
    p = _dg(u.astype(BF16), wm_ref[...])
    qk_ref[0] = p[:, 0:512]
    v_ref[0] = p[:, 512:1024].astype(BF16)
    o_ref[0] = p[:, 1024:1536].astype(BF16)
    nq_ref[0] = (p[:, 1536:2048] * (NA_DH ** -0.5)).astype(BF16)
    nk_ref[0] = p[:, 2048:2560].astype(BF16)
    nv_ref[0] = p[:, 2560:3072].astype(BF16)
    g_ref[0] = _dot3(u, wg_ref[...])
    gt_ref[0] = _dot3(wgt_ref[...], u, NT)


def _inproj(x, sc, sh, wm, wg, wgt):
    b, t, d = x.shape
    tm = min(t, 512)
    tok = lambda w, dt: jax.ShapeDtypeStruct((b, t, w), dt)
    blk = lambda w: pl.BlockSpec((1, tm, w), lambda i, j: (i, j, 0))
    return pl.pallas_call(
        _inproj_kernel,
        grid=(b, t // tm),
        in_specs=[
            blk(d),
            pl.BlockSpec((1, 1, d), lambda i, j: (i, 0, 0)),
            pl.BlockSpec((1, 1, d), lambda i, j: (i, 0, 0)),
            pl.BlockSpec(wm.shape, lambda i, j: (0, 0)),
            pl.BlockSpec(wg.shape, lambda i, j: (0, 0)),
            pl.BlockSpec(wgt.shape, lambda i, j: (0, 0)),
        ],
        out_specs=[blk(512), blk(512), blk(512), blk(512), blk(512), blk(512), blk(LANES),
                   pl.BlockSpec((1, ML_G_W, tm), lambda i, j: (i, 0, j))],
        out_shape=[tok(512, F32), tok(512, BF16), tok(512, BF16), tok(512, BF16), tok(512, BF16),
                   tok(512, BF16), tok(LANES, F32), jax.ShapeDtypeStruct((b, ML_G_W, t), F32)],
        compiler_params=_params(("parallel", "parallel")),
        name="inproj",
    )(x, sc, sh, wm, wg, wgt)


CONV_TILE = 128


def _conv_silu(qk_ref, cw_ref, pad_ref, t):
    width = qk_ref.shape[-1]
    zero = jnp.zeros((8, width), F32)
    pad_ref[pl.ds(0, 8), :] = zero
    pad_ref[pl.ds(8 + t, 8), :] = zero

    def copy(i, carry):
        r0 = pl.multiple_of(i * CONV_TILE, CONV_TILE)
        pad_ref[pl.ds(r0 + 8, CONV_TILE), :] = qk_ref[0, pl.ds(r0, CONV_TILE), :]
        return carry

    lax.fori_loop(0, t // CONV_TILE, copy, 0)


def _conv_tile(pad_ref, cw_ref, r0):
    n = CONV_TILE + 16
    win = pad_ref[pl.ds(r0, n), :]
    acc = None
    for j in range(CONV_W):
        k = 6 + j
        sh = pltpu.roll(win, n - k, axis=0)[:CONV_TILE]
        term = sh * cw_ref[pl.ds(j, 1), :]
        acc = term if acc is None else acc + term
    return acc * _sigmoid(acc)


def _rope_tile(y, cos, sin):
    lane = lax.broadcasted_iota(jnp.int32, (1, y.shape[1]), 1)
    first = (lane % ML_DK) < (ML_DK // 2)
    n = y.shape[1]
    partner = jnp.where(first, pltpu.roll(y, n - ML_DK // 2, axis=1), pltpu.roll(y, ML_DK // 2, axis=1))
    cos4 = jnp.concatenate([cos] * (n // LANES), axis=1)
    sin4 = jnp.concatenate([sin] * (n // LANES), axis=1)
    return y * cos4 + partner * sin4


def _mlstm_kernel(need_ctx, t_x, t_c,
                  qkx_ref, vx_ref, ox_ref, gx_ref, gtx_ref,
                  qkc_ref, vc_ref, oc_ref, gc_ref, gtc_ref,
                  cos_ref, sin_ref, cw_ref, ks_ref, bg_ref, bgt_ref, nw_ref,
                  *rest):
    if need_ctx:
        mlx_ref, mlc_ref = rest[:2]
        scratch = rest[2:]
    else:
        mlx_ref, mlc_ref = rest[0], None
        scratch = rest[1:]
    padx_ref, padc_ref, qsx_ref, qsc_ref, hfx_ref, hbx_ref, hfc_ref, hbc_ref, st_ref, m_ref = scratch

    _conv_silu(qkx_ref, cw_ref, padx_ref, t_x)
    _conv_silu(qkc_ref, cw_ref, padc_ref, t_c)
    kscale = ks_ref[...]

    def prep_x(i, carry):
        r0 = pl.multiple_of(i * CONV_TILE, CONV_TILE)
        y = _conv_tile(padx_ref, cw_ref, r0)
        y = _rope_tile(y, cos_ref[pl.ds(r0, CONV_TILE), :], sin_ref[pl.ds(r0, CONV_TILE), :])
        qsx_ref[pl.ds(r0, CONV_TILE), :] = y * kscale
        return carry

    def prep_c(i, carry):
        r0 = pl.multiple_of(i * CONV_TILE, CONV_TILE)
        qsc_ref[pl.ds(r0, CONV_TILE), :] = _conv_tile(padc_ref, cw_ref, r0) * kscale
        return carry

    lax.fori_loop(0, t_x // CONV_TILE, prep_x, 0)
    lax.fori_loop(0, t_c // CONV_TILE, prep_c, 0)

    st_ref[...] = jnp.zeros(st_ref.shape, F32)
    m_ref[...] = jnp.zeros(m_ref.shape, F32)

    L = ML_CHUNK
    row = lax.broadcasted_iota(jnp.int32, (L, L), 0)
    col = lax.broadcasted_iota(jnp.int32, (L, L), 1)
    lower = col <= row
    upper = col >= row
    ones_pad = (lax.broadcasted_iota(jnp.int32, (L, ML_DV), 1) == 0).astype(BF16)
    bg = bg_ref[...]
    bgt = bgt_ref[...]

    def chunk_pair(qs_ref, v_ref, g_ref, gt_ref, hf_ref, hb_ref, n_chunks, write_h):
        def body(i, carry):
            for d in range(2):
                c = i if d == 0 else n_chunks - 1 - i
                r0 = pl.multiple_of(c * L, L)
                mask = lower if d == 0 else upper
                mask_t = upper if d == 0 else lower
                qk = qs_ref[pl.ds(r0, L), :]
                vv = v_ref[0, pl.ds(r0, L), :]
                g = g_ref[0, pl.ds(r0, L), :][:, 0:ML_G_W] + bg
                gt = gt_ref[0, c] + bgt
                lsg = _log_sigmoid(g)
                lsgt = _log_sigmoid(gt)
                h_ref = hf_ref if d == 0 else hb_ref
                for h in range(ML_HEADS):
                    ci = 2 * d * ML_HEADS + h
                    fi = (2 * d + 1) * ML_HEADS + h
                    li_col, lf_col = g[:, ci:ci + 1], lsg[:, fi:fi + 1]
                    li_row, lf_row = gt[ci:ci + 1, :], lsgt[fi:fi + 1, :]
                    b_col = jnp.sum(jnp.where(mask, lf_row, 0.0), axis=1, keepdims=True)
                    b_row = jnp.sum(jnp.where(mask_t, lf_col, 0.0), axis=0, keepdims=True)
                    b_last = jnp.sum(lf_row, axis=1, keepdims=True)
                    sidx = d * ML_HEADS + h
                    m_in = m_ref[sidx][:, 0:1]
                    ct = st_ref[sidx]
                    q_h = qk[:, h * ML_DK:(h + 1) * ML_DK]
                    k_h = qk[:, ML_QK_W // 2 + h * ML_DK: ML_QK_W // 2 + (h + 1) * ML_DK]
                    v_ext = jnp.concatenate([vv[:, h * ML_DV:(h + 1) * ML_DV], ones_pad], axis=1)
                    if write_h:
                        dm = jnp.where(mask, b_col - b_row + li_row, NEG_BIG)
                        m_inter = b_col + m_in
                        m_j = jnp.maximum(m_inter, jnp.max(dm, axis=1, keepdims=True))
                        s = _dg(q_h.astype(BF16), k_h.astype(BF16), NT) * jnp.exp(dm - m_j)
                        inter = jnp.exp(m_inter - m_j)
                        ne = _dg(s.astype(BF16), v_ext) + inter * _dg(q_h.astype(BF16), ct.astype(BF16))
                        den = jnp.maximum(jnp.abs(ne[:, ML_DV:ML_DV + 1]), jnp.exp(-m_j))
                        h_ref[pl.ds(r0, L), h * ML_DV:(h + 1) * ML_DV] = ne[:, 0:ML_DV] / den
                    a_col = b_last - b_col + li_col
                    m_loc = jnp.max(a_col, axis=0, keepdims=True)
                    kw = (k_h * jnp.exp(a_col - m_loc)).T.astype(BF16)
                    c_loc = _dg(kw, v_ext)
                    m_new = jnp.maximum(b_last + m_in, m_loc)
                    st_ref[sidx] = jnp.exp(b_last + m_in - m_new) * ct + jnp.exp(m_loc - m_new) * c_loc
                    m_ref[sidx] = jnp.broadcast_to(m_new, (1, LANES))
            return carry

        lax.fori_loop(0, n_chunks, body, 0)

    chunk_pair(qsc_ref, vc_ref, gc_ref, gtc_ref, hfc_ref, hbc_ref, t_c // L, need_ctx)
    chunk_pair(qsx_ref, vx_ref, gx_ref, gtx_ref, hfx_ref, hbx_ref, t_x // L, True)

    nw = nw_ref[...]

    def finish(hf_ref, hb_ref, o_ref, out_ref, t):
        def body(i, carry):
            r0 = pl.multiple_of(i * CONV_TILE, CONV_TILE)
            hsum = hf_ref[pl.ds(r0, CONV_TILE), :] + hb_ref[pl.ds(r0, CONV_TILE), :]
            parts = []
            for h in range(ML_HEADS):
                hh = hsum[:, h * ML_DV:(h + 1) * ML_DV]
                mu = jnp.mean(hh, axis=1, keepdims=True)
                hc = hh - mu
                var = jnp.mean(hc * hc, axis=1, keepdims=True)
                parts.append(hc * lax.rsqrt(var + LN_EPS))
            y = jnp.concatenate(parts, axis=1) * nw
            y = y * _sigmoid(o_ref[0, pl.ds(r0, CONV_TILE), :].astype(F32))
            out_ref[0, pl.ds(r0, CONV_TILE), :] = y.astype(BF16)
            return carry

        lax.fori_loop(0, t // CONV_TILE, body, 0)

    finish(hfx_ref, hbx_ref, ox_ref, mlx_ref, t_x)
    if need_ctx:
        finish(hfc_ref, hbc_ref, oc_ref, mlc_ref, t_c)


def _mlstm(need_ctx, px, pc, cos, sin, conv_w, kscale, b_gate, ml_norm_w):
    qkx, vx, ox, gx, gtx = px
    qkc, vc, oc, gc, gtc = pc
    b, t_x, _ = qkx.shape
    t_c = qkc.shape[1]
    L = ML_CHUNK
    gtx = gtx.reshape(b, ML_G_W, t_x // L, L).transpose(0, 2, 1, 3)
    gtc = gtc.reshape(b, ML_G_W, t_c // L, L).transpose(0, 2, 1, 3)
    cw = jnp.zeros((8, ML_QK_W), F32).at[:CONV_W].set(conv_w)
    tokx = lambda w: pl.BlockSpec((1, t_x, w), lambda i: (i, 0, 0))
    tokc = lambda w: pl.BlockSpec((1, t_c, w), lambda i: (i, 0, 0))
    const = lambda a: pl.BlockSpec(a.shape, lambda i: (0,) * a.ndim)
    bg = b_gate.reshape(1, ML_G_W)
    bgt = b_gate.reshape(ML_G_W, 1)
    nw = ml_norm_w.reshape(1, ML_V_W)
    out_specs = [tokx(ML_V_W)]
    out_shape = [jax.ShapeDtypeStruct((b, t_x, ML_V_W), BF16)]
    if need_ctx:
        out_specs.append(tokc(ML_V_W))
        out_shape.append(jax.ShapeDtypeStruct((b, t_c, ML_V_W), BF16))
    outs = pl.pallas_call(
        functools.partial(_mlstm_kernel, need_ctx, t_x, t_c),
        grid=(b,),
        in_specs=[
            tokx(ML_QK_W), tokx(ML_V_W), tokx(ML_V_W), tokx(LANES),
            pl.BlockSpec((1, t_x // L, ML_G_W, L), lambda i: (i, 0, 0, 0)),
            tokc(ML_QK_W), tokc(ML_V_W), tokc(ML_V_W), tokc(LANES),
            pl.BlockSpec((1, t_c // L, ML_G_W, L), lambda i: (i, 0, 0, 0)),
            const(cos), const(sin), const(cw), const(kscale), const(bg), const(bgt), const(nw),
        ],
        out_specs=out_specs,
        out_shape=out_shape,
        scratch_shapes=[
            pltpu.VMEM((t_x + 16, ML_QK_W), F32), pltpu.VMEM((t_c + 16, ML_QK_W), F32),
            pltpu.VMEM((t_x, ML_QK_W), F32), pltpu.VMEM((t_c, ML_QK_W), F32),
            pltpu.VMEM((t_x, ML_V_W), F32), pltpu.VMEM((t_x, ML_V_W), F32),
            pltpu.VMEM((t_c, ML_V_W), F32), pltpu.VMEM((t_c, ML_V_W), F32),
            pltpu.VMEM((2 * ML_HEADS, ML_DK, 2 * ML_DV), F32),
            pltpu.VMEM((2 * ML_HEADS, 1, LANES), F32),
        ],
        compiler_params=_params(("parallel",)),
        name="mlstm",
    )(qkx, vx, ox, gx, gtx, qkc, vc, oc, gc, gtc, cos, sin, cw, kscale, bg, bgt, nw)
    return (outs[0], outs[1]) if need_ctx else (outs[0], None)


def _rope_tables(t):
    pos = np.arange(t)
    n_freq = ML_DK // 4
    inv = ROPE_BASE ** (-np.arange(n_freq, dtype=np.float32) / n_freq)
    ang = np.concatenate([(pos // GRID_W)[:, None] * inv, (pos % GRID_W)[:, None] * inv], -1).astype(np.float32)
    cos, sin = np.cos(ang), np.sin(ang)
    cos_h = np.concatenate([cos, cos], -1)
    sin_h = np.concatenate([-sin, sin], -1)
    return (jnp.asarray(np.concatenate([cos_h, cos_h], -1), F32),
            jnp.asarray(np.concatenate([sin_h, sin_h], -1), F32))


def _head_masks():
    lane = lax.broadcasted_iota(jnp.int32, (1, 2 * NA_DH), 1)
    return lane < NA_DH, lane >= NA_DH


def _na_kernel(rows, q_ref, k_ref, v_ref, kc_ref, vc_ref, bm_ref, o_ref):
    masks = _head_masks()
    kc = kc_ref[0]
    vc = vc_ref[0]
    kh = NA_WIN_H
    band = kh * GRID_W

    def body(r, carry):
        r0 = jnp.clip(r - kh // 2, 0, rows - kh)
        off = r - r0
        q = q_ref[0, pl.ds(pl.multiple_of(r * GRID_W, GRID_W), GRID_W), :]
        kb = k_ref[0, pl.ds(pl.multiple_of(r0 * GRID_W, GRID_W), band), :]
        vb = v_ref[0, pl.ds(pl.multiple_of(r0 * GRID_W, GRID_W), band), :]
        outs = []
        for hh in range(2):
            qm = jnp.where(masks[hh], q, jnp.zeros_like(q))
            s_loc = _dg(qm, kb, NT) + bm_ref[hh, off]
            s_ctx = _dg(qm, kc, NT)
            m = jnp.maximum(jnp.max(s_loc, axis=1, keepdims=True), jnp.max(s_ctx, axis=1, keepdims=True))
            p_loc = jnp.exp(s_loc - m)
            p_ctx = jnp.exp(s_ctx - m)
            denom = jnp.sum(p_loc, axis=1, keepdims=True) + jnp.sum(p_ctx, axis=1, keepdims=True)
            o = _dg(p_loc.astype(BF16), vb) + _dg(p_ctx.astype(BF16), vc)
            outs.append(o / denom)
        out = jnp.where(masks[0], outs[0], outs[1])
        o_ref[0, pl.ds(pl.multiple_of(r * GRID_W, GRID_W), GRID_W), :] = out.astype(BF16)
        return carry

    lax.fori_loop(0, rows, body, 0)


def _na_latent(nq, nk, nv, nkc, nvc, bias):
    b, t, _ = nq.shape
    t_c = nkc.shape[1]
    rows = t // GRID_W
    tok = lambda tt: pl.BlockSpec((1, tt, 2 * NA_DH), lambda i, j: (i, 0, j))
    return pl.pallas_call(
        functools.partial(_na_kernel, rows),
        grid=(b, NA_HEADS // 2),
        in_specs=[tok(t), tok(t), tok(t), tok(t_c), tok(t_c),
                  pl.BlockSpec((2,) + bias.shape[1:], lambda i, j: (j, 0, 0, 0))],
        out_specs=tok(t),
        out_shape=jax.ShapeDtypeStruct((b, t, NA_W), BF16),
        compiler_params=_params(("parallel", "parallel")),
        name="na_latent",
    )(nq, nk, nv, nkc, nvc, bias)


def _nactx_kernel(q_ref, k_ref, v_ref, o_ref):
    masks = _head_masks()
    q, k, v = q_ref[0], k_ref[0], v_ref[0]
    outs = []
    for hh in range(2):
        qm = jnp.where(masks[hh], q, jnp.zeros_like(q))
        s = _dg(qm, k, NT)
        p = jnp.exp(s - jnp.max(s, axis=1, keepdims=True))
        outs.append(_dg(p.astype(BF16), v) / jnp.sum(p, axis=1, keepdims=True))
    o_ref[0] = jnp.where(masks[0], outs[0], outs[1]).astype(BF16)


def _na_ctx(nqc, nkc, nvc):
    b, t_c, _ = nqc.shape
    tok = pl.BlockSpec((1, t_c, 2 * NA_DH), lambda i, j: (i, 0, j))
    return pl.pallas_call(
        _nactx_kernel,
        grid=(b, NA_HEADS // 2),
        in_specs=[tok, tok, tok],
        out_specs=tok,
        out_shape=jax.ShapeDtypeStruct((b, t_c, NA_W), BF16),
        compiler_params=_params(("parallel", "parallel")),
        name="na_ctx",
    )(nqc, nkc, nvc)


def _na_bias(rpb, rows):
    kh = min(NA_WIN_H, rows)
    c = np.arange(GRID_W)
    win_lo = np.clip(c - NA_WIN_W // 2, 0, GRID_W - NA_WIN_W)
    ok = (c[None, :] >= win_lo[:, None]) & (c[None, :] < win_lo[:, None] + NA_WIN_W)
    span = np.clip(np.arange(2 * GRID_W - 1) - (GRID_W - 1), 1 - NA_WIN_W, NA_WIN_W - 1) + NA_WIN_W - 1
    ext = rpb[:, :, span]
    cols = jnp.stack([ext[:, :, GRID_W - 1 - q: 2 * GRID_W - 1 - q] for q in range(GRID_W)], axis=2)
    cols = jnp.where(ok[None, None], cols, NEG_BIG)
    per_off = [cols[:, NA_WIN_H - 1 - off: NA_WIN_H - 1 - off + kh] for off in range(kh)]
    bias = jnp.stack(per_off, axis=1).transpose(0, 1, 3, 2, 4)
    return bias.reshape(NA_HEADS, kh, GRID_W, kh * GRID_W)


def _outproj_kernel(alpha, ml_ref, na_ref, x_ref, g1_ref, w_ref, lw_ref, lb_ref, sc_ref, sh_ref,
                    x1_ref, uth_ref, utl_ref):
    a = jnp.concatenate([ml_ref[0], na_ref[0]], axis=1)
    y = _dg(a, w_ref[...])
    x1 = _layer_norm(alpha * x_ref[0] + g1_ref[0] * y, lw_ref[...], lb_ref[...])
    x1_ref[0] = x1
    ut = (x1 * (1.0 + sc_ref[0]) + sh_ref[0]).T
    hi, lo = _split_bf16(ut)
    uth_ref[...] = hi
    utl_ref[...] = lo


def _outproj(alpha, ml, na, x, g1, w_out, lw, lb, sc2, sh2):
    b, t, d = x.shape
    tm = min(t, 512)
    blk = lambda w: pl.BlockSpec((1, tm, w), lambda i, j: (i, j, 0))
    per_b = pl.BlockSpec((1, 1, d), lambda i, j: (i, 0, 0))
    const = lambda a: pl.BlockSpec(a.shape, lambda i, j: (0,) * a.ndim)
    tposed = pl.BlockSpec((d, tm), lambda i, j: (0, i * (t // tm) + j))
    return pl.pallas_call(
        functools.partial(_outproj_kernel, alpha),
        grid=(b, t // tm),
        in_specs=[blk(ML_V_W), blk(NA_W), blk(d), per_b, const(w_out), const(lw), const(lb), per_b, per_b],
        out_specs=[blk(d), tposed, tposed],
        out_shape=[jax.ShapeDtypeStruct((b, t, d), F32), jax.ShapeDtypeStruct((d, b * t), BF16),
                   jax.ShapeDtypeStruct((d, b * t), BF16)],
        compiler_params=_params(("parallel", "parallel")),
        name="outproj",
    )(ml, na, x, g1, w_out, lw, lb, sc2, sh2)


ROUTE_TB = 256
ROUTE_GROUP = 2
CAND_ROWS = 16 + 7 * 8 + 8


def _cand_tables():
    a_idx = np.zeros(CAND_ROWS, np.int64)
    b_idx = np.zeros(CAND_ROWS, np.int64)
    a_idx[0:16], b_idx[0:16] = 0, np.arange(16)
    for a in range(1, 8):
        a_idx[16 + 8 * (a - 1): 24 + 8 * (a - 1)] = a
        b_idx[16 + 8 * (a - 1): 24 + 8 * (a - 1)] = np.arange(8)
    a_idx[72:80], b_idx[72:80] = np.arange(8, 16), 0
    valid = (a_idx + 1) * (b_idx + 1) <= PEER_TOPK
    flat = (a_idx * PEER_TOPK + b_idx).astype(np.float32)
    neg = np.where(valid, 0.0, -np.inf).astype(np.float32)
    tile = lambda v: jnp.asarray(np.tile(v[:, None], (1, ROUTE_TB)), F32)
    return tile(flat), tile(neg)


def _tree(op, xs):
    while len(xs) > 1:
        xs = [op(xs[i], xs[i + 1]) for i in range(0, len(xs) - 1, 2)] + ([xs[-1]] if len(xs) % 2 else [])
    return xs[0]


def _top_rounds(s, orders=None, order_end=None):
    n = s.shape[0] // SUBLANES
    vals = [s[SUBLANES * i:SUBLANES * (i + 1)] for i in range(n)]
    ranks = [jnp.full(vals[0].shape, float(PEER_TOPK), F32)] * n
    tops = []
    for k in range(PEER_TOPK):
        m = jnp.max(_tree(jnp.maximum, vals), axis=0, keepdims=True)
        hits = [v == m for v in vals]
        if orders is not None:
            first = _tree(jnp.minimum, [jnp.where(h, o, order_end) for h, o in zip(hits, orders)])
            first = jnp.min(first, axis=0, keepdims=True)
            hits = [o == first for o in orders]
        ranks = [jnp.where(h, float(k), r) for h, r in zip(hits, ranks)]
        vals = [jnp.where(h, -jnp.inf, v) for h, v in zip(hits, vals)]
        tops.append(m)
    return jnp.concatenate(ranks, axis=0), tops


def _one_per_round(rank):
    taken = jnp.sum(jnp.where(rank < float(PEER_TOPK), 1.0, 0.0), axis=0, keepdims=True)
    return jnp.max(jnp.abs(taken - float(PEER_TOPK))) == 0.0


def _route_kernel(uth_ref, utl_ref, wqh_ref, wql_ref, kh_ref, kl_ref, flat_ref, neg_ref,
                  lim_ref, ea_ref, r2_ref, eb_ref, qt_ref, rank_ref, t_ref, e_ref, pick_ref):
    tb = uth_ref.shape[1]
    nk = PEER_NKEYS
    uh = uth_ref[...]
    wqh = wqh_ref[...]
    qt_ref[...] = _dg(wqh, uh) + (_dg(wqh, utl_ref[...]) + _dg(wql_ref[...], uh))
    sub_iota = lax.broadcasted_iota(jnp.int32, (SUBLANES, tb), 0).astype(F32)
    key_order = [sub_iota + float(SUBLANES * i) for i in range(nk // SUBLANES)]

    def sub_scores(i, carry):
        todo = []
        for g in range(ROUTE_GROUP):
            hh = ROUTE_GROUP * i + g
            q = qt_ref[pl.ds(pl.multiple_of(hh * nk, nk), nk), :]
            qh, ql = _split_bf16(q)
            kh = kh_ref[hh]
            s0 = _dg(kh, qh) + (_dg(kh, ql) + _dg(kl_ref[hh], qh))
            rank, tops = _top_rounds(s0)
            rank_ref[hh] = rank
            t_ref[hh] = jnp.concatenate(tops, axis=0)
            e_ref[hh] = jnp.exp(s0 - tops[0])
            todo.append((hh, s0, _one_per_round(rank)))

        @pl.when(jnp.logical_not(functools.reduce(jnp.logical_and, [ok for _, _, ok in todo])))
        def _():
            for hh, s0, _ in todo:
                rank_t, tops_t = _top_rounds(s0, key_order, float(nk))
                rank_ref[hh] = rank_t
                t_ref[hh] = jnp.concatenate(tops_t, axis=0)

        return carry

    lax.fori_loop(0, 2 * PEER_HEADS // ROUTE_GROUP, sub_scores, 0)

    def joint(i, carry):
        todo = []
        for g in range(ROUTE_GROUP):
            h = ROUTE_GROUP * i + g
            t1 = t_ref[2 * h]
            t2 = t_ref[2 * h + 1]
            blocks = [t1[0:1] + t2]
            for a in range(1, 8):
                blocks.append(t1[a:a + 1] + t2[0:8])
            blocks.append(t1[8:16] + t2[0:1])
            cand = jnp.concatenate(blocks, axis=0) + neg_ref[...]
            pick, _ = _top_rounds(cand)
            pick_ref[g] = pick
            todo.append((h, cand, t1[0:1] + t2[0:1], _one_per_round(pick)))

        @pl.when(jnp.logical_not(functools.reduce(jnp.logical_and, [ok for _, _, _, ok in todo])))
        def _():
            flat = [flat_ref[SUBLANES * r:SUBLANES * (r + 1), :] for r in range(CAND_ROWS // SUBLANES)]
            for g, (_, cand, _, _) in enumerate(todo):
                pick_ref[g] = _top_rounds(cand, flat, 1e9)[0]

        for g, (h, cand, top, _) in enumerate(todo):
            chosen = jnp.where(pick_ref[g] < float(PEER_TOPK), 1.0, 0.0)
            z = jnp.sum(chosen * jnp.exp(cand - top), axis=0, keepdims=True)
            counts = [jnp.sum(chosen[0:16], axis=0, keepdims=True)]
            for a in range(1, 8):
                counts.append(jnp.sum(chosen[16 + 8 * (a - 1): 24 + 8 * (a - 1)], axis=0, keepdims=True))
            tail = chosen[72:80]
            for a in range(8):
                counts.append(tail[a:a + 1])
            rank1 = rank_ref[2 * h].astype(BF16)
            lim = jnp.zeros((nk, tb), BF16)
            for a in range(PEER_TOPK):
                lim = jnp.where(rank1 == float(a), jnp.broadcast_to(counts[a], (nk, tb)).astype(BF16), lim)
            lim_ref[h] = lim.astype(F32)
            ea_ref[h] = e_ref[2 * h] * (1.0 / z)
            r2_ref[h] = rank_ref[2 * h + 1].astype(BF16)
            eb_ref[h] = e_ref[2 * h + 1].astype(BF16)
        return carry

    lax.fori_loop(0, PEER_HEADS // ROUTE_GROUP, joint, 0)


def _route(uth, utl, wqh, wql, kh, kl):
    d, n = uth.shape
    tb = ROUTE_TB
    flat, neg = _cand_tables()
    nk = PEER_NKEYS
    const = lambda a: pl.BlockSpec(a.shape, lambda i: (0,) * a.ndim)
    tok = pl.BlockSpec((d, tb), lambda i: (0, i))
    out_blk = pl.BlockSpec((PEER_HEADS, nk, tb), lambda i: (0, 0, i))
    meta = lambda dt: jax.ShapeDtypeStruct((PEER_HEADS, nk, n), dt)
    return pl.pallas_call(
        _route_kernel,
        grid=(n // tb,),
        in_specs=[tok, tok, const(wqh), const(wql), const(kh), const(kl), const(flat), const(neg)],
        out_specs=[out_blk, out_blk, out_blk, out_blk],
        out_shape=[meta(F32), meta(F32), meta(BF16), meta(BF16)],
        scratch_shapes=[
            pltpu.VMEM((2 * PEER_HEADS * nk, tb), F32),
            pltpu.VMEM((2 * PEER_HEADS, nk, tb), F32),
            pltpu.VMEM((2 * PEER_HEADS, PEER_TOPK, tb), F32),
            pltpu.VMEM((2 * PEER_HEADS, nk, tb), F32),
            pltpu.VMEM((ROUTE_GROUP, CAND_ROWS, tb), F32),
        ],
        compiler_params=_params(("parallel",)),
        name="peer_route",
    )(uth, utl, wqh, wql, kh, kl, flat, neg)


PEER_TB = 512
PEER_EB = 1024


def _gelu_tanh(x):
    return 0.5 * x * (1.0 + jnp.tanh(0.7978845608028654 * (x + 0.044715 * (x * x * x))))


def _peer_kernel(alpha, n_eb, ut_ref, up_ref, dnt_ref, lim_ref, ea_ref, r2_ref, eb_ref,
                 x1_ref, g2_ref, lw_ref, lb_ref, out_ref, acc_ref, z_ref):
    e = pl.program_id(1)
    nk = PEER_NKEYS
    tb = ut_ref.shape[1]

    @pl.when(e == 0)
    def _():
        acc_ref[...] = jnp.zeros(acc_ref.shape, F32)

    ht = _dg(up_ref[...], ut_ref[...])
    zero = jnp.zeros((nk, tb), BF16)
    for j in range(PEER_EB // nk):
        act = _gelu_tanh(ht[j * nk:(j + 1) * nk].astype(BF16))
        gate = None
        for h in range(PEER_HEADS):
            lim = jnp.broadcast_to(lim_ref[h, j:j + 1, :], (nk, tb)).astype(BF16)
            ea = jnp.broadcast_to(ea_ref[h, j:j + 1, :], (nk, tb)).astype(BF16)
            term = jnp.where(r2_ref[h] < lim, eb_ref[h], zero) * ea
            gate = term if gate is None else gate + term
        z_ref[j * nk:(j + 1) * nk, :] = act * gate
    acc_ref[...] += _dg(dnt_ref[...], z_ref[...])

    @pl.when(e == n_eb - 1)
    def _():
        y = acc_ref[...].T
        out_ref[...] = _layer_norm(alpha * x1_ref[...] + g2_ref[0] * y, lw_ref[...], lb_ref[...])


def _peer_experts(alpha, ut, up, dnt, lim, ea, r2, eb, x1, g2, lw, lb, tokens_per_batch):
    n, d = x1.shape
    tb = min(PEER_TB, tokens_per_batch)
    n_exp = up.shape[0]
    n_eb = n_exp // PEER_EB
    nk = PEER_NKEYS
    keys_per_step = PEER_EB // nk
    tok = pl.BlockSpec((tb, d), lambda i, e: (i, 0))
    row_meta = pl.BlockSpec((PEER_HEADS, keys_per_step, tb), lambda i, e: (0, e, i))
    col_meta = pl.BlockSpec((PEER_HEADS, nk, tb), lambda i, e: (0, 0, i))
    const = lambda a: pl.BlockSpec(a.shape, lambda i, e: (0,) * a.ndim)
    return pl.pallas_call(
        functools.partial(_peer_kernel, alpha, n_eb),
        grid=(n // tb, n_eb),
        in_specs=[pl.BlockSpec((d, tb), lambda i, e: (0, i)),
                  pl.BlockSpec((PEER_EB, d), lambda i, e: (e, 0)),
                  pl.BlockSpec((d, PEER_EB), lambda i, e: (0, e)),
                  row_meta, row_meta, col_meta, col_meta,
                  tok,
                  pl.BlockSpec((1, 1, d), lambda i, e: (i * tb // tokens_per_batch, 0, 0)),
                  const(lw), const(lb)],
        out_specs=tok,
        out_shape=jax.ShapeDtypeStruct((n, d), F32),
        scratch_shapes=[pltpu.VMEM((d, tb), F32), pltpu.VMEM((PEER_EB, tb), BF16)],
        compiler_params=_params(("parallel", "arbitrary")),
        name="peer_experts",
    )(ut, up, dnt, lim, ea, r2, eb, x1, g2, lw, lb)


def kernel(x, c, ctx, c_ctx, ada_w, ada_b, w_in, b_gate, conv_w, ml_norm_w, na_rpb, w_out, ln1_w, ln1_b,
           peer_wq, peer_keys, peer_up, peer_down, ln2_w, ln2_b):
    depth = ada_w.shape[0]
    b, t, d = x.shape
    t_c = ctx.shape[1]
    alpha = (2.0 * depth) ** 0.25
    rows = t // GRID_W

    n_cond = b + 1
    cond = jnp.zeros((-(-n_cond // 8) * 8, d), F32).at[:b].set(c).at[b].set(c_ctx)
    mods = _modulation(cond, ada_w, ada_b)
    cos, sin = _rope_tables(t)
    kscale = jnp.concatenate([jnp.ones((1, ML_QK_W // 2), F32), jnp.full((1, ML_QK_W // 2), ML_DK ** -0.5, F32)], 1)

    g_lo = ML_QK_W + 2 * ML_V_W
    g_hi = g_lo + ML_G_W
    for l in range(depth):
        need_ctx = l < depth - 1
        mod = mods[l].reshape(-1, N_MOD, d)
        lat = [mod[:b, i].reshape(b, 1, d) for i in range(N_MOD)]
        cxm = [jnp.broadcast_to(mod[b, i].reshape(1, 1, d), (b, 1, d)) for i in range(N_MOD)]

        w = w_in[l]
        wm = jnp.concatenate([w[:, :g_lo], w[:, g_hi:]], axis=1).astype(BF16)
        wg = jnp.zeros((d, LANES), F32).at[:, :ML_G_W].set(w[:, g_lo:g_hi])
        wgt = w[:, g_lo:g_hi].T

        qkx, vx, ox, nqx, nkx, nvx, gx, gtx = _inproj(x, lat[1], lat[0], wm, wg, wgt)
        qkc, vc, oc, nqc, nkc, nvc, gc, gtc = _inproj(ctx, cxm[1], cxm[0], wm, wg, wgt)

        ml_x, ml_c = _mlstm(need_ctx, (qkx, vx, ox, gx, gtx), (qkc, vc, oc, gc, gtc),
                            cos, sin, conv_w[l], kscale, b_gate[l], ml_norm_w[l])
        na_x = _na_latent(nqx, nkx, nvx, nkc, nvc, _na_bias(na_rpb[l], rows))

        wo = w_out[l].astype(BF16)
        lw1, lb1 = ln1_w[l].reshape(1, d), ln1_b[l].reshape(1, d)
        lw2, lb2 = ln2_w[l].reshape(1, d), ln2_b[l].reshape(1, d)
        wq_t = peer_wq[l].T
        wqh = wq_t.astype(BF16)
        wql = (wq_t - wqh.astype(F32)).astype(BF16)
        keys = peer_keys[l].reshape(2 * PEER_HEADS, PEER_NKEYS, -1)
        kh = keys.astype(BF16)
        kl = (keys - kh.astype(F32)).astype(BF16)
        up = peer_up[l].astype(BF16)
        dnt = peer_down[l].T.astype(BF16)

        def channel(xin, ml, na, m, tokens):
            x1, uth, utl = _outproj(alpha, ml, na, xin, m[2], wo, lw1, lb1, m[4], m[3])
            lim, ea, r2, eb = _route(uth, utl, wqh, wql, kh, kl)
            out = _peer_experts(alpha, uth, up, dnt, lim, ea, r2, eb, x1.reshape(-1, d), m[5], lw2, lb2, tokens)
            return out.reshape(xin.shape)

        x_new = channel(x, ml_x, na_x, lat, t)
        if need_ctx:
            na_c = _na_ctx(nqc, nkc, nvc)
            ctx = channel(ctx, ml_c, na_c, cxm, t_c)
        x = x_new
    return x
```

```python
import functools

import jax
import jax.numpy as jnp
import numpy as np
from jax import lax
from jax.experimental import pallas as pl
from jax.experimental.pallas import tpu as pltpu

F32 = jnp.float32
BF16 = jnp.bfloat16

D_MODEL = 1024
GRID_W = 64
ML_HEADS = 4
ML_DV = 128
ML_DK = 64
ML_CHUNK = 64
CONV_W = 5
NA_HEADS = 8
NA_DH = 64
NA_WIN_H = 8
NA_WIN_W = 16
ROPE_BASE = 10000.0
PEER_HEADS = 8
PEER_NKEYS = 128
PEER_TOPK = 16
N_MOD = 6
ML_QK_W = 2 * ML_HEADS * ML_DK
ML_V_W = ML_HEADS * ML_DV
ML_G_W = 4 * ML_HEADS
NA_W = NA_HEADS * NA_DH
LN_EPS = 1e-5
NEG_BIG = -1e30

LANES = 128
SUBLANES = 8
VMEM_LIMIT = 56 * 1024 * 1024

NN = (((1,), (0,)), ((), ()))
NT = (((1,), (1,)), ((), ()))


def _dg(a, b, dims=NN):
    return lax.dot_general(a, b, dims, preferred_element_type=F32)


def _split_bf16(a):
    hi = a.astype(BF16)
    lo = (a - hi.astype(F32)).astype(BF16)
    return hi, lo


def _dot3(a, b, dims=NN):
    ah, al = _split_bf16(a)
    bh, bl = _split_bf16(b)
    return _dg(ah, bh, dims) + (_dg(ah, bl, dims) + _dg(al, bh, dims))


def _sigmoid(x):
    return 1.0 / (1.0 + jnp.exp(-x))


def _log_sigmoid(x):
    return jnp.minimum(x, 0.0) - jnp.log(1.0 + jnp.exp(-jnp.abs(x)))


def _layer_norm(z, w, b):
    mu = jnp.mean(z, axis=-1, keepdims=True)
    zc = z - mu
    var = jnp.mean(zc * zc, axis=-1, keepdims=True)
    return zc * lax.rsqrt(var + LN_EPS) * w + b


def _params(sem):
    return pltpu.CompilerParams(dimension_semantics=sem, vmem_limit_bytes=VMEM_LIMIT)


def _mod_kernel(c_ref, w_ref, b_ref, o_ref):
    c = c_ref[...]
    s = c * _sigmoid(c)
    o_ref[0] = _dot3(s, w_ref[0]) + b_ref[0]


def _modulation(cond, ada_w, ada_b):
    depth, d, n = ada_w.shape
    rows = cond.shape[0]
    tn = 1024
    return pl.pallas_call(
        _mod_kernel,
        grid=(depth, n // tn),
        in_specs=[
            pl.BlockSpec((rows, d), lambda l, j: (0, 0)),
            pl.BlockSpec((1, d, tn), lambda l, j: (l, 0, j)),
            pl.BlockSpec((1, 1, tn), lambda l, j: (l, 0, j)),
        ],
        out_specs=pl.BlockSpec((1, rows, tn), lambda l, j: (l, 0, j)),
        out_shape=jax.ShapeDtypeStruct((depth, rows, n), F32),
        compiler_params=_params(("parallel", "parallel")),
        name="modulation",
    )(cond, ada_w, ada_b.reshape(depth, 1, n))


def _inproj_kernel(x_ref, sc_ref, sh_ref, wm_ref, wg_ref, wgt_ref,
                   qk_ref, v_ref, o_ref, nq_ref, nk_ref, nv_ref, g_ref, gt_ref):
    u = x_ref[0] * (1.0 + sc_ref[0]) + sh_ref[0]
    p = _dg(u.astype(BF16), wm_ref[...])
    qk_ref[0] = p[:, 0:512]
    v_ref[0] = p[:, 512:1024].astype(BF16)
    o_ref[0] = p[:, 1024:1536].astype(BF16)
    nq_ref[0] = (p[:, 1536:2048] * (NA_DH ** -0.5)).astype(BF16)
    nk_ref[0] = p[:, 2048:2560].astype(BF16)
    nv_ref[0] = p[:, 2560:3072].astype(BF16)
    g_ref[0] = _dot3(u, wg_ref[...])
    gt_ref[0] = _dot3(wgt_ref[...], u, NT)


def _inproj(x, sc, sh, wm, wg, wgt):
    b, t, d = x.shape
    tm = min(t, 512)
    tok = lambda w, dt: jax.ShapeDtypeStruct((b, t, w), dt)
    blk = lambda w: pl.BlockSpec((1, tm, w), lambda i, j: (i, j, 0))
    return pl.pallas_call(
        _inproj_kernel,
        grid=(b, t // tm),
        in_specs=[
            blk(d),
            pl.BlockSpec((1, 1, d), lambda i, j: (i, 0, 0)),
            pl.BlockSpec((1, 1, d), lambda i, j: (i, 0, 0)),
            pl.BlockSpec(wm.shape, lambda i, j: (0, 0)),
            pl.BlockSpec(wg.shape, lambda i, j: (0, 0)),
            pl.BlockSpec(wgt.shape, lambda i, j: (0, 0)),
        ],
        out_specs=[blk(512), blk(512), blk(512), blk(512), blk(512), blk(512), blk(LANES),
                   pl.BlockSpec((1, ML_G_W, tm), lambda i, j: (i, 0, j))],
        out_shape=[tok(512, F32), tok(512, BF16), tok(512, BF16), tok(512, BF16), tok(512, BF16),
                   tok(512, BF16), tok(LANES, F32), jax.ShapeDtypeStruct((b, ML_G_W, t), F32)],
        compiler_params=_params(("parallel", "parallel")),
        name="inproj",
    )(x, sc, sh, wm, wg, wgt)


CONV_TILE = 128


def _conv_silu(qk_ref, cw_ref, pad_ref, t):
    width = qk_ref.shape[-1]
    zero = jnp.zeros((8, width), F32)
    pad_ref[pl.ds(0, 8), :] = zero
    pad_ref[pl.ds(8 + t, 8), :] = zero

    def copy(i, carry):
        r0 = pl.multiple_of(i * CONV_TILE, CONV_TILE)
        pad_ref[pl.ds(r0 + 8, CONV_TILE), :] = qk_ref[0, pl.ds(r0, CONV_TILE), :]
        return carry

    lax.fori_loop(0, t // CONV_TILE, copy, 0)


def _conv_tile(pad_ref, cw_ref, r0):
    n = CONV_TILE + 16
    win = pad_ref[pl.ds(r0, n), :]
    acc = None
    for j in range(CONV_W):
        k = 6 + j
        sh = pltpu.roll(win, n - k, axis=0)[:CONV_TILE]
        term = sh * cw_ref[pl.ds(j, 1), :]
        acc = term if acc is None else acc + term
    return acc * _sigmoid(acc)


def _rope_tile(y, cos, sin):
    lane = lax.broadcasted_iota(jnp.int32, (1, y.shape[1]), 1)
    first = (lane % ML_DK) < (ML_DK // 2)
    n = y.shape[1]
    partner = jnp.where(first, pltpu.roll(y, n - ML_DK // 2, axis=1), pltpu.roll(y, ML_DK // 2, axis=1))
    cos4 = jnp.concatenate([cos] * (n // LANES), axis=1)
    sin4 = jnp.concatenate([sin] * (n // LANES), axis=1)
    return y * cos4 + partner * sin4


def _mlstm_kernel(need_ctx, t_x, t_c,
                  qkx_ref, vx_ref, ox_ref, gx_ref, gtx_ref,
                  qkc_ref, vc_ref, oc_ref, gc_ref, gtc_ref,
                  cos_ref, sin_ref, cw_ref, ks_ref, bg_ref, bgt_ref, nw_ref,
                  *rest):
    if need_ctx:
        mlx_ref, mlc_ref = rest[:2]
        scratch = rest[2:]
    else:
        mlx_ref, mlc_ref = rest[0], None
        scratch = rest[1:]
    padx_ref, padc_ref, qsx_ref, qsc_ref, hfx_ref, hbx_ref, hfc_ref, hbc_ref, st_ref, m_ref = scratch

    _conv_silu(qkx_ref, cw_ref, padx_ref, t_x)
    _conv_silu(qkc_ref, cw_ref, padc_ref, t_c)
    kscale = ks_ref[...]

    def prep_x(i, carry):
        r0 = pl.multiple_of(i * CONV_TILE, CONV_TILE)
        y = _conv_tile(padx_ref, cw_ref, r0)
        y = _rope_tile(y, cos_ref[pl.ds(r0, CONV_TILE), :], sin_ref[pl.ds(r0, CONV_TILE), :])
        qsx_ref[pl.ds(r0, CONV_TILE), :] = y * kscale
        return carry

    def prep_c(i, carry):
        r0 = pl.multiple_of(i * CONV_TILE, CONV_TILE)
        qsc_ref[pl.ds(r0, CONV_TILE), :] = _conv_tile(padc_ref, cw_ref, r0) * kscale
        return carry

    lax.fori_loop(0, t_x // CONV_TILE, prep_x, 0)
    lax.fori_loop(0, t_c // CONV_TILE, prep_c, 0)

    st_ref[...] = jnp.zeros(st_ref.shape, F32)
    m_ref[...] = jnp.zeros(m_ref.shape, F32)

    L = ML_CHUNK
    row = lax.broadcasted_iota(jnp.int32, (L, L), 0)
    col = lax.broadcasted_iota(jnp.int32, (L, L), 1)
    lower = col <= row
    upper = col >= row
    ones_pad = (lax.broadcasted_iota(jnp.int32, (L, ML_DV), 1) == 0).astype(BF16)
    bg = bg_ref[...]
    bgt = bgt_ref[...]

    def chunk_pair(qs_ref, v_ref, g_ref, gt_ref, hf_ref, hb_ref, n_chunks, write_h):
        def body(i, carry):
            for d in range(2):
                c = i if d == 0 else n_chunks - 1 - i
                r0 = pl.multiple_of(c * L, L)
                mask = lower if d == 0 else upper
                mask_t = upper if d == 0 else lower
                qk = qs_ref[pl.ds(r0, L), :]
                vv = v_ref[0, pl.ds(r0, L), :]
                g = g_ref[0, pl.ds(r0, L), :][:, 0:ML_G_W] + bg
                gt = gt_ref[0, c] + bgt
                lsg = _log_sigmoid(g)
                lsgt = _log_sigmoid(gt)
                h_ref = hf_ref if d == 0 else hb_ref
                for h in range(ML_HEADS):
                    ci = 2 * d * ML_HEADS + h
                    fi = (2 * d + 1) * ML_HEADS + h
                    li_col, lf_col = g[:, ci:ci + 1], lsg[:, fi:fi + 1]
                    li_row, lf_row = gt[ci:ci + 1, :], lsgt[fi:fi + 1, :]
                    b_col = jnp.sum(jnp.where(mask, lf_row, 0.0), axis=1, keepdims=True)
                    b_row = jnp.sum(jnp.where(mask_t, lf_col, 0.0), axis=0, keepdims=True)
                    b_last = jnp.sum(lf_row, axis=1, keepdims=True)
                    sidx = d * ML_HEADS + h
                    m_in = m_ref[sidx][:, 0:1]
                    ct = st_ref[sidx]
                    q_h = qk[:, h * ML_DK:(h + 1) * ML_DK]
                    k_h = qk[:, ML_QK_W // 2 + h * ML_DK: ML_QK_W // 2 + (h + 1) * ML_DK]
                    v_ext = jnp.concatenate([vv[:, h * ML_DV:(h + 1) * ML_DV], ones_pad], axis=1)
                    if write_h:
                        dm = jnp.where(mask, b_col - b_row + li_row, NEG_BIG)
                        m_inter = b_col + m_in
                        m_j = jnp.maximum(m_inter, jnp.max(dm, axis=1, keepdims=True))
                        s = _dg(q_h.astype(BF16), k_h.astype(BF16), NT) * jnp.exp(dm - m_j)
                        inter = jnp.exp(m_inter - m_j)
                        ne = _dg(s.astype(BF16), v_ext) + inter * _dg(q_h.astype(BF16), ct.astype(BF16))
                        den = jnp.maximum(jnp.abs(ne[:, ML_DV:ML_DV + 1]), jnp.exp(-m_j))
                        h_ref[pl.ds(r0, L), h * ML_DV:(h + 1) * ML_DV] = ne[:, 0:ML_DV] / den
                    a_col = b_last - b_col + li_col
                    m_loc = jnp.max(a_col, axis=0, keepdims=True)
                    kw = (k_h * jnp.exp(a_col - m_loc)).T.astype(BF16)
                    c_loc = _dg(kw, v_ext)
                    m_new = jnp.maximum(b_last + m_in, m_loc)
                    st_ref[sidx] = jnp.exp(b_last + m_in - m_new) * ct + jnp.exp(m_loc - m_new) * c_loc
                    m_ref[sidx] = jnp.broadcast_to(m_new, (1, LANES))
            return carry

        lax.fori_loop(0, n_chunks, body, 0)

    chunk_pair(qsc_ref, vc_ref, gc_ref, gtc_ref, hfc_ref, hbc_ref, t_c // L, need_ctx)
    chunk_pair(qsx_ref, vx_ref, gx_ref, gtx_ref, hfx_ref, hbx_ref, t_x // L, True)

    nw = nw_ref[...]

    def finish(hf_ref, hb_ref, o_ref, out_ref, t):
        def body(i, carry):
            r0 = pl.multiple_of(i * CONV_TILE, CONV_TILE)
            hsum = hf_ref[pl.ds(r0, CONV_TILE), :] + hb_ref[pl.ds(r0, CONV_TILE), :]
            parts = []
            for h in range(ML_HEADS):
                hh = hsum[:, h * ML_DV:(h + 1) * ML_DV]
                mu = jnp.mean(hh, axis=1, keepdims=True)
                hc = hh - mu
                var = jnp.mean(hc * hc, axis=1, keepdims=True)
                parts.append(hc * lax.rsqrt(var + LN_EPS))
            y = jnp.concatenate(parts, axis=1) * nw
            y = y * _sigmoid(o_ref[0, pl.ds(r0, CONV_TILE), :].astype(F32))
            out_ref[0, pl.ds(r0, CONV_TILE), :] = y.astype(BF16)
            return carry

        lax.fori_loop(0, t // CONV_TILE, body, 0)

    finish(hfx_ref, hbx_ref, ox_ref, mlx_ref, t_x)
    if need_ctx:
        finish(hfc_ref, hbc_ref, oc_ref, mlc_ref, t_c)


def _mlstm(need_ctx, px, pc, cos, sin, conv_w, kscale, b_gate, ml_norm_w):
    qkx, vx, ox, gx, gtx = px
    qkc, vc, oc, gc, gtc = pc
    b, t_x, _ = qkx.shape
    t_c = qkc.shape[1]
    L = ML_CHUNK
    gtx = gtx.reshape(b, ML_G_W, t_x // L, L).transpose(0, 2, 1, 3)
    gtc = gtc.reshape(b, ML_G_W, t_c // L, L).transpose(0, 2, 1, 3)
    cw = jnp.zeros((8, ML_QK_W), F32).at[:CONV_W].set(conv_w)
    tokx = lambda w: pl.BlockSpec((1, t_x, w), lambda i: (i, 0, 0))
    tokc = lambda w: pl.BlockSpec((1, t_c, w), lambda i: (i, 0, 0))
    const = lambda a: pl.BlockSpec(a.shape, lambda i: (0,) * a.ndim)
    bg = b_gate.reshape(1, ML_G_W)
    bgt = b_gate.reshape(ML_G_W, 1)
    nw = ml_norm_w.reshape(1, ML_V_W)
    out_specs = [tokx(ML_V_W)]
    out_shape = [jax.ShapeDtypeStruct((b, t_x, ML_V_W), BF16)]
    if need_ctx:
        out_specs.append(tokc(ML_V_W))
        out_shape.append(jax.ShapeDtypeStruct((b, t_c, ML_V_W), BF16))
    outs = pl.pallas_call(
        functools.partial(_mlstm_kernel, need_ctx, t_x, t_c),
        grid=(b,),
        in_specs=[
            tokx(ML_QK_W), tokx(ML_V_W), tokx(ML_V_W), tokx(LANES),
            pl.BlockSpec((1, t_x // L, ML_G_W, L), lambda i: (i, 0, 0, 0)),
            tokc(ML_QK_W), tokc(ML_V_W), tokc(ML_V_W), tokc(LANES),
            pl.BlockSpec((1, t_c // L, ML_G_W, L), lambda i: (i, 0, 0, 0)),
            const(cos), const(sin), const(cw), const(kscale), const(bg), const(bgt), const(nw),
        ],
        out_specs=out_specs,
        out_shape=out_shape,
        scratch_shapes=[
            pltpu.VMEM((t_x + 16, ML_QK_W), F32), pltpu.VMEM((t_c + 16, ML_QK_W), F32),
            pltpu.VMEM((t_x, ML_QK_W), F32), pltpu.VMEM((t_c, ML_QK_W), F32),
            pltpu.VMEM((t_x, ML_V_W), F32), pltpu.VMEM((t_x, ML_V_W), F32),
            pltpu.VMEM((t_c, ML_V_W), F32), pltpu.VMEM((t_c, ML_V_W), F32),
            pltpu.VMEM((2 * ML_HEADS, ML_DK, 2 * ML_DV), F32),
            pltpu.VMEM((2 * ML_HEADS, 1, LANES), F32),
        ],
        compiler_params=_params(("parallel",)),
        name="mlstm",
    )(qkx, vx, ox, gx, gtx, qkc, vc, oc, gc, gtc, cos, sin, cw, kscale, bg, bgt, nw)
    return (outs[0], outs[1]) if need_ctx else (outs[0], None)


def _rope_tables(t):
    pos = np.arange(t)
    n_freq = ML_DK // 4
    inv = ROPE_BASE ** (-np.arange(n_freq, dtype=np.float32) / n_freq)
    ang = np.concatenate([(pos // GRID_W)[:, None] * inv, (pos % GRID_W)[:, None] * inv], -1).astype(np.float32)
    cos, sin = np.cos(ang), np.sin(ang)
    cos_h = np.concatenate([cos, cos], -1)
    sin_h = np.concatenate([-sin, sin], -1)
    return (jnp.asarray(np.concatenate([cos_h, cos_h], -1), F32),
            jnp.asarray(np.concatenate([sin_h, sin_h], -1), F32))


NA_GROUP = 4
NA_BAND = NA_WIN_H + NA_GROUP - 1
NA_FINISH_TILE = 512


def _head_masks():
    lane = lax.broadcasted_iota(jnp.int32, (1, 2 * NA_DH), 1)
    return lane < NA_DH, lane >= NA_DH


def _na_kernel(rows, q_ref, k_ref, v_ref, kc_ref, vc_ref, bm_ref, o_ref, sctx_ref, pctx_ref, oloc_ref, den_ref):
    masks = _head_masks()
    t = rows * GRID_W
    n_groups = rows // NA_GROUP
    gq = NA_GROUP * GRID_W
    band = NA_BAND * GRID_W
    kc = kc_ref[0]
    vc = vc_ref[0]

    for hh in range(2):
        q_all = q_ref[0]
        sctx_ref[hh] = _dg(jnp.where(masks[hh], q_all, jnp.zeros_like(q_all)), kc, NT)

    def group(g, carry):
        b0 = jnp.clip(NA_GROUP * g - NA_WIN_H // 2, 0, rows - NA_BAND)
        kind = jnp.where(g == 0, 0, jnp.where(g == n_groups - 1, 2, 1))
        qrows = pl.ds(pl.multiple_of(g * gq, gq), gq)
        krows = pl.ds(pl.multiple_of(b0 * GRID_W, GRID_W), band)
        q = q_ref[0, qrows, :]
        kb = k_ref[0, krows, :]
        vb = v_ref[0, krows, :]
        for hh in range(2):
            qm = jnp.where(masks[hh], q, jnp.zeros_like(q))
            s_loc = _dg(qm, kb, NT) + bm_ref[hh, kind]
            s_ctx = sctx_ref[hh, qrows, :]
            m = jnp.maximum(jnp.max(s_loc, axis=1, keepdims=True), jnp.max(s_ctx, axis=1, keepdims=True))
            p_loc = jnp.exp(s_loc - m)
            p_ctx = jnp.exp(s_ctx - m)
            den = jnp.sum(p_loc, axis=1, keepdims=True) + jnp.sum(p_ctx, axis=1, keepdims=True)
            pctx_ref[hh, qrows, :] = p_ctx.astype(BF16)
            den_ref[hh, qrows, :] = jnp.broadcast_to(den, (gq, 2 * NA_DH))
            oloc_ref[hh, qrows, :] = _dg(p_loc.astype(BF16), vb)
        return carry

    lax.fori_loop(0, n_groups, group, 0)

    def finish(i, carry):
        trows = pl.ds(pl.multiple_of(i * NA_FINISH_TILE, NA_FINISH_TILE), NA_FINISH_TILE)
        outs = []
        for hh in range(2):
            o = oloc_ref[hh, trows, :] + _dg(pctx_ref[hh, trows, :], vc)
            outs.append(o / den_ref[hh, trows, :])
        o_ref[0, trows, :] = jnp.where(masks[0], outs[0], outs[1]).astype(BF16)
        return carry

    lax.fori_loop(0, t // NA_FINISH_TILE, finish, 0)


def _na_latent(nq, nk, nv, nkc, nvc, bias):
    b, t, _ = nq.shape
    t_c = nkc.shape[1]
    rows = t // GRID_W
    assert rows % NA_GROUP == 0 and rows >= NA_BAND and t % NA_FINISH_TILE == 0
    tok = lambda tt: pl.BlockSpec((1, tt, 2 * NA_DH), lambda i, j: (i, 0, j))
    return pl.pallas_call(
        functools.partial(_na_kernel, rows),
        grid=(b, NA_HEADS // 2),
        in_specs=[tok(t), tok(t), tok(t), tok(t_c), tok(t_c),
                  pl.BlockSpec((2,) + bias.shape[1:], lambda i, j: (j, 0, 0, 0))],
        out_specs=tok(t),
        out_shape=jax.ShapeDtypeStruct((b, t, NA_W), BF16),
        scratch_shapes=[pltpu.VMEM((2, t, t_c), F32), pltpu.VMEM((2, t, t_c), BF16),
                        pltpu.VMEM((2, t, 2 * NA_DH), F32), pltpu.VMEM((2, t, 2 * NA_DH), F32)],
        compiler_params=_params(("parallel", "parallel")),
        name="na_latent",
    )(nq, nk, nv, nkc, nvc, bias)


def _nactx_kernel(q_ref, k_ref, v_ref, o_ref):
    masks = _head_masks()
    q, k, v = q_ref[0], k_ref[0], v_ref[0]
    outs = []
    for hh in range(2):
        qm = jnp.where(masks[hh], q, jnp.zeros_like(q))
        s = _dg(qm, k, NT)
        p = jnp.exp(s - jnp.max(s, axis=1, keepdims=True))
        outs.append(_dg(p.astype(BF16), v) / jnp.sum(p, axis=1, keepdims=True))
    o_ref[0] = jnp.where(masks[0], outs[0], outs[1]).astype(BF16)


def _na_ctx(nqc, nkc, nvc):
    b, t_c, _ = nqc.shape
    tok = pl.BlockSpec((1, t_c, 2 * NA_DH), lambda i, j: (i, 0, j))
    return pl.pallas_call(
        _nactx_kernel,
        grid=(b, NA_HEADS // 2),
        in_specs=[tok, tok, tok],
        out_specs=tok,
        out_shape=jax.ShapeDtypeStruct((b, t_c, NA_W), BF16),
        compiler_params=_params(("parallel", "parallel")),
        name="na_ctx",
    )(nqc, nkc, nvc)


def _na_bias(rpb):
    c = np.arange(GRID_W)
    win_lo = np.clip(c - NA_WIN_W // 2, 0, GRID_W - NA_WIN_W)
    ok = (c[None, :] >= win_lo[:, None]) & (c[None, :] < win_lo[:, None] + NA_WIN_W)
    span = np.clip(np.arange(2 * GRID_W - 1) - (GRID_W - 1), 1 - NA_WIN_W, NA_WIN_W - 1) + NA_WIN_W - 1
    ext = rpb[:, :, span]
    cols = jnp.stack([ext[:, :, GRID_W - 1 - q: 2 * GRID_W - 1 - q] for q in range(GRID_W)], axis=2)
    cols = jnp.where(ok[None, None], cols, NEG_BIG)
    masked = jnp.full(cols[:, 0].shape, NEG_BIG, F32)
    half = NA_WIN_H // 2
    kinds = [lambda rq: (rq, 0),
             lambda rq: (half + rq, rq),
             lambda rq: (NA_BAND - NA_GROUP + rq, NA_BAND - NA_WIN_H)]
    per_kind = []
    for kind in kinds:
        per_row = []
        for rq in range(NA_GROUP):
            q_rel, lo = kind(rq)
            per_row.append(jnp.stack([cols[:, i - q_rel + NA_WIN_H - 1] if lo <= i < lo + NA_WIN_H else masked
                                      for i in range(NA_BAND)], axis=2))
        per_kind.append(jnp.stack(per_row, axis=1))
    bias = jnp.stack(per_kind, axis=1)
    return bias.reshape(NA_HEADS, len(kinds), NA_GROUP * GRID_W, NA_BAND * GRID_W)


def _outproj_kernel(alpha, ml_ref, na_ref, x_ref, g1_ref, w_ref, lw_ref, lb_ref, sc_ref, sh_ref,
                    x1_ref, uth_ref, utl_ref):
    a = jnp.concatenate([ml_ref[0], na_ref[0]], axis=1)
    y = _dg(a, w_ref[...])
    x1 = _layer_norm(alpha * x_ref[0] + g1_ref[0] * y, lw_ref[...], lb_ref[...])
    x1_ref[0] = x1
    ut = (x1 * (1.0 + sc_ref[0]) + sh_ref[0]).T
    hi, lo = _split_bf16(ut)
    uth_ref[...] = hi
    utl_ref[...] = lo


def _outproj(alpha, ml, na, x, g1, w_out, lw, lb, sc2, sh2):
    b, t, d = x.shape
    tm = min(t, 512)
    blk = lambda w: pl.BlockSpec((1, tm, w), lambda i, j: (i, j, 0))
    per_b = pl.BlockSpec((1, 1, d), lambda i, j: (i, 0, 0))
    const = lambda a: pl.BlockSpec(a.shape, lambda i, j: (0,) * a.ndim)
    tposed = pl.BlockSpec((d, tm), lambda i, j: (0, i * (t // tm) + j))
    return pl.pallas_call(
        functools.partial(_outproj_kernel, alpha),
        grid=(b, t // tm),
        in_specs=[blk(ML_V_W), blk(NA_W), blk(d), per_b, const(w_out), const(lw), const(lb), per_b, per_b],
        out_specs=[blk(d), tposed, tposed],
        out_shape=[jax.ShapeDtypeStruct((b, t, d), F32), jax.ShapeDtypeStruct((d, b * t), BF16),
                   jax.ShapeDtypeStruct((d, b * t), BF16)],
        compiler_params=_params(("parallel", "parallel")),
        name="outproj",
    )(ml, na, x, g1, w_out, lw, lb, sc2, sh2)


ROUTE_TB = 256
ROUTE_GROUP = 2
CAND_ROWS = 16 + 7 * 8 + 8


def _cand_tables():
    a_idx = np.zeros(CAND_ROWS, np.int64)
    b_idx = np.zeros(CAND_ROWS, np.int64)
    a_idx[0:16], b_idx[0:16] = 0, np.arange(16)
    for a in range(1, 8):
        a_idx[16 + 8 * (a - 1): 24 + 8 * (a - 1)] = a
        b_idx[16 + 8 * (a - 1): 24 + 8 * (a - 1)] = np.arange(8)
    a_idx[72:80], b_idx[72:80] = np.arange(8, 16), 0
    valid = (a_idx + 1) * (b_idx + 1) <= PEER_TOPK
    flat = (a_idx * PEER_TOPK + b_idx).astype(np.float32)
    neg = np.where(valid, 0.0, -np.inf).astype(np.float32)
    tile = lambda v: jnp.asarray(np.tile(v[:, None], (1, ROUTE_TB)), F32)
    return tile(flat), tile(neg)


def _tree(op, xs):
    while len(xs) > 1:
        xs = [op(xs[i], xs[i + 1]) for i in range(0, len(xs) - 1, 2)] + ([xs[-1]] if len(xs) % 2 else [])
    return xs[0]


def _top_rounds(s, orders=None, order_end=None):
    n = s.shape[0] // SUBLANES
    vals = [s[SUBLANES * i:SUBLANES * (i + 1)] for i in range(n)]
    ranks = [jnp.full(vals[0].shape, float(PEER_TOPK), F32)] * n
    tops = []
    for k in range(PEER_TOPK):
        m = jnp.max(_tree(jnp.maximum, vals), axis=0, keepdims=True)
        hits = [v == m for v in vals]
        if orders is not None:
            first = _tree(jnp.minimum, [jnp.where(h, o, order_end) for h, o in zip(hits, orders)])
            first = jnp.min(first, axis=0, keepdims=True)
            hits = [o == first for o in orders]
        ranks = [jnp.where(h, float(k), r) for h, r in zip(hits, ranks)]
        vals = [jnp.where(h, -jnp.inf, v) for h, v in zip(hits, vals)]
        tops.append(m)
    return jnp.concatenate(ranks, axis=0), tops


def _one_per_round(rank):
    taken = jnp.sum(jnp.where(rank < float(PEER_TOPK), 1.0, 0.0), axis=0, keepdims=True)
    return jnp.max(jnp.abs(taken - float(PEER_TOPK))) == 0.0


def _route_kernel(uth_ref, utl_ref, wqh_ref, wql_ref, kh_ref, kl_ref, flat_ref, neg_ref,
                  lim_ref, ea_ref, r2_ref, eb_ref, qt_ref, rank_ref, t_ref, e_ref, pick_ref):
    tb = uth_ref.shape[1]
    nk = PEER_NKEYS
    uh = uth_ref[...]
    wqh = wqh_ref[...]
    qt_ref[...] = _dg(wqh, uh) + (_dg(wqh, utl_ref[...]) + _dg(wql_ref[...], uh))
    sub_iota = lax.broadcasted_iota(jnp.int32, (SUBLANES, tb), 0).astype(F32)
    key_order = [sub_iota + float(SUBLANES * i) for i in range(nk // SUBLANES)]

    def sub_scores(i, carry):
        todo = []
        for g in range(ROUTE_GROUP):
            hh = ROUTE_GROUP * i + g
            q = qt_ref[pl.ds(pl.multiple_of(hh * nk, nk), nk), :]
            qh, ql = _split_bf16(q)
            kh = kh_ref[hh]
            s0 = _dg(kh, qh) + (_dg(kh, ql) + _dg(kl_ref[hh], qh))
            rank, tops = _top_rounds(s0)
            rank_ref[hh] = rank
            t_ref[hh] = jnp.concatenate(tops, axis=0)
            e_ref[hh] = jnp.exp(s0 - tops[0])
            todo.append((hh, s0, _one_per_round(rank)))

        @pl.when(jnp.logical_not(functools.reduce(jnp.logical_and, [ok for _, _, ok in todo])))
        def _():
            for hh, s0, _ in todo:
                rank_t, tops_t = _top_rounds(s0, key_order, float(nk))
                rank_ref[hh] = rank_t
                t_ref[hh] = jnp.concatenate(tops_t, axis=0)

        return carry

    lax.fori_loop(0, 2 * PEER_HEADS // ROUTE_GROUP, sub_scores, 0)

    def joint(i, carry):
        todo = []
        for g in range(ROUTE_GROUP):
            h = ROUTE_GROUP * i + g
            t1 = t_ref[2 * h]
            t2 = t_ref[2 * h + 1]
            blocks = [t1[0:1] + t2]
            for a in range(1, 8):
                blocks.append(t1[a:a + 1] + t2[0:8])
            blocks.append(t1[8:16] + t2[0:1])
            cand = jnp.concatenate(blocks, axis=0) + neg_ref[...]
            pick, _ = _top_rounds(cand)
            pick_ref[g] = pick
            todo.append((h, cand, t1[0:1] + t2[0:1], _one_per_round(pick)))

        @pl.when(jnp.logical_not(functools.reduce(jnp.logical_and, [ok for _, _, _, ok in todo])))
        def _():
            flat = [flat_ref[SUBLANES * r:SUBLANES * (r + 1), :] for r in range(CAND_ROWS // SUBLANES)]
            for g, (_, cand, _, _) in enumerate(todo):
                pick_ref[g] = _top_rounds(cand, flat, 1e9)[0]

        for g, (h, cand, top, _) in enumerate(todo):
            chosen = jnp.where(pick_ref[g] < float(PEER_TOPK), 1.0, 0.0)
            z = jnp.sum(chosen * jnp.exp(cand - top), axis=0, keepdims=True)
            counts = [jnp.sum(chosen[0:16], axis=0, keepdims=True)]
            for a in range(1, 8):
                counts.append(jnp.sum(chosen[16 + 8 * (a - 1): 24 + 8 * (a - 1)], axis=0, keepdims=True))
            tail = chosen[72:80]
            for a in range(8):
                counts.append(tail[a:a + 1])
            rank1 = rank_ref[2 * h].astype(BF16)
            lim = jnp.zeros((nk, tb), BF16)
            for a in range(PEER_TOPK):
                lim = jnp.where(rank1 == float(a), jnp.broadcast_to(counts[a], (nk, tb)).astype(BF16), lim)
            lim_ref[h] = lim.astype(F32)
            ea_ref[h] = e_ref[2 * h] * (1.0 / z)
            r2_ref[h] = rank_ref[2 * h + 1].astype(BF16)
            eb_ref[h] = e_ref[2 * h + 1].astype(BF16)
        return carry

    lax.fori_loop(0, PEER_HEADS // ROUTE_GROUP, joint, 0)


def _route(uth, utl, wqh, wql, kh, kl):
    d, n = uth.shape
    tb = ROUTE_TB
    flat, neg = _cand_tables()
    nk = PEER_NKEYS
    const = lambda a: pl.BlockSpec(a.shape, lambda i: (0,) * a.ndim)
    tok = pl.BlockSpec((d, tb), lambda i: (0, i))
    out_blk = pl.BlockSpec((PEER_HEADS, nk, tb), lambda i: (0, 0, i))
    meta = lambda dt: jax.ShapeDtypeStruct((PEER_HEADS, nk, n), dt)
    return pl.pallas_call(
        _route_kernel,
        grid=(n // tb,),
        in_specs=[tok, tok, const(wqh), const(wql), const(kh), const(kl), const(flat), const(neg)],
        out_specs=[out_blk, out_blk, out_blk, out_blk],
        out_shape=[meta(F32), meta(F32), meta(BF16), meta(BF16)],
        scratch_shapes=[
            pltpu.VMEM((2 * PEER_HEADS * nk, tb), F32),
            pltpu.VMEM((2 * PEER_HEADS, nk, tb), F32),
            pltpu.VMEM((2 * PEER_HEADS, PEER_TOPK, tb), F32),
            pltpu.VMEM((2 * PEER_HEADS, nk, tb), F32),
            pltpu.VMEM((ROUTE_GROUP, CAND_ROWS, tb), F32),
        ],
        compiler_params=_params(("parallel",)),
        name="peer_route",
    )(uth, utl, wqh, wql, kh, kl, flat, neg)


PEER_TB = 512
PEER_EB = 1024


def _gelu_tanh(x):
    return 0.5 * x * (1.0 + jnp.tanh(0.7978845608028654 * (x + 0.044715 * (x * x * x))))


def _peer_kernel(alpha, n_eb, ut_ref, up_ref, dnt_ref, lim_ref, ea_ref, r2_ref, eb_ref,
                 x1_ref, g2_ref, lw_ref, lb_ref, out_ref, acc_ref, z_ref):
    e = pl.program_id(1)
    nk = PEER_NKEYS
    tb = ut_ref.shape[1]

    @pl.when(e == 0)
    def _():
        acc_ref[...] = jnp.zeros(acc_ref.shape, F32)

    ht = _dg(up_ref[...], ut_ref[...])
    zero = jnp.zeros((nk, tb), BF16)
    for j in range(PEER_EB // nk):
        act = _gelu_tanh(ht[j * nk:(j + 1) * nk].astype(BF16))
        gate = None
        for h in range(PEER_HEADS):
            lim = jnp.broadcast_to(lim_ref[h, j:j + 1, :], (nk, tb)).astype(BF16)
            ea = jnp.broadcast_to(ea_ref[h, j:j + 1, :], (nk, tb)).astype(BF16)
            term = jnp.where(r2_ref[h] < lim, eb_ref[h], zero) * ea
            gate = term if gate is None else gate + term
        z_ref[j * nk:(j + 1) * nk, :] = act * gate
    acc_ref[...] += _dg(dnt_ref[...], z_ref[...])

    @pl.when(e == n_eb - 1)
    def _():
        y = acc_ref[...].T
        out_ref[...] = _layer_norm(alpha * x1_ref[...] + g2_ref[0] * y, lw_ref[...], lb_ref[...])


def _peer_experts(alpha, ut, up, dnt, lim, ea, r2, eb, x1, g2, lw, lb, tokens_per_batch):
    n, d = x1.shape
    tb = min(PEER_TB, tokens_per_batch)
    n_exp = up.shape[0]
    n_eb = n_exp // PEER_EB
    nk = PEER_NKEYS
    keys_per_step = PEER_EB // nk
    tok = pl.BlockSpec((tb, d), lambda i, e: (i, 0))
    row_meta = pl.BlockSpec((PEER_HEADS, keys_per_step, tb), lambda i, e: (0, e, i))
    col_meta = pl.BlockSpec((PEER_HEADS, nk, tb), lambda i, e: (0, 0, i))
    const = lambda a: pl.BlockSpec(a.shape, lambda i, e: (0,) * a.ndim)
    return pl.pallas_call(
        functools.partial(_peer_kernel, alpha, n_eb),
        grid=(n // tb, n_eb),
        in_specs=[pl.BlockSpec((d, tb), lambda i, e: (0, i)),
                  pl.BlockSpec((PEER_EB, d), lambda i, e: (e, 0)),
                  pl.BlockSpec((d, PEER_EB), lambda i, e: (0, e)),
                  row_meta, row_meta, col_meta, col_meta,
                  tok,
                  pl.BlockSpec((1, 1, d), lambda i, e: (i * tb // tokens_per_batch, 0, 0)),
                  const(lw), const(lb)],
        out_specs=tok,
        out_shape=jax.ShapeDtypeStruct((n, d), F32),
        scratch_shapes=[pltpu.VMEM((d, tb), F32), pltpu.VMEM((PEER_EB, tb), BF16)],
        compiler_params=_params(("parallel", "arbitrary")),
        name="peer_experts",
    )(ut, up, dnt, lim, ea, r2, eb, x1, g2, lw, lb)


def kernel(x, c, ctx, c_ctx, ada_w, ada_b, w_in, b_gate, conv_w, ml_norm_w, na_rpb, w_out, ln1_w, ln1_b,
           peer_wq, peer_keys, peer_up, peer_down, ln2_w, ln2_b):
    depth = ada_w.shape[0]
    b, t, d = x.shape
    t_c = ctx.shape[1]
    alpha = (2.0 * depth) ** 0.25
    rows = t // GRID_W

    n_cond = b + 1
    cond = jnp.zeros((-(-n_cond // 8) * 8, d), F32).at[:b].set(c).at[b].set(c_ctx)
    mods = _modulation(cond, ada_w, ada_b)
    cos, sin = _rope_tables(t)
    kscale = jnp.concatenate([jnp.ones((1, ML_QK_W // 2), F32), jnp.full((1, ML_QK_W // 2), ML_DK ** -0.5, F32)], 1)

    g_lo = ML_QK_W + 2 * ML_V_W
    g_hi = g_lo + ML_G_W
    for l in range(depth):
        need_ctx = l < depth - 1
        mod = mods[l].reshape(-1, N_MOD, d)
        lat = [mod[:b, i].reshape(b, 1, d) for i in range(N_MOD)]
        cxm = [jnp.broadcast_to(mod[b, i].reshape(1, 1, d), (b, 1, d)) for i in range(N_MOD)]

        w = w_in[l]
        wm = jnp.concatenate([w[:, :g_lo], w[:, g_hi:]], axis=1).astype(BF16)
        wg = jnp.zeros((d, LANES), F32).at[:, :ML_G_W].set(w[:, g_lo:g_hi])
        wgt = w[:, g_lo:g_hi].T

        qkx, vx, ox, nqx, nkx, nvx, gx, gtx = _inproj(x, lat[1], lat[0], wm, wg, wgt)
        qkc, vc, oc, nqc, nkc, nvc, gc, gtc = _inproj(ctx, cxm[1], cxm[0], wm, wg, wgt)

        ml_x, ml_c = _mlstm(need_ctx, (qkx, vx, ox, gx, gtx), (qkc, vc, oc, gc, gtc),
                            cos, sin, conv_w[l], kscale, b_gate[l], ml_norm_w[l])
        na_x = _na_latent(nqx, nkx, nvx, nkc, nvc, _na_bias(na_rpb[l]))

        wo = w_out[l].astype(BF16)
        lw1, lb1 = ln1_w[l].reshape(1, d), ln1_b[l].reshape(1, d)
        lw2, lb2 = ln2_w[l].reshape(1, d), ln2_b[l].reshape(1, d)
        wq_t = peer_wq[l].T
        wqh = wq_t.astype(BF16)
        wql = (wq_t - wqh.astype(F32)).astype(BF16)
        keys = peer_keys[l].reshape(2 * PEER_HEADS, PEER_NKEYS, -1)
        kh = keys.astype(BF16)
        kl = (keys - kh.astype(F32)).astype(BF16)
        up = peer_up[l].astype(BF16)
        dnt = peer_down[l].T.astype(BF16)

        def channel(xin, ml, na, m, tokens):
            x1, uth, utl = _outproj(alpha, ml, na, xin, m[2], wo, lw1, lb1, m[4], m[3])
            lim, ea, r2, eb = _route(uth, utl, wqh, wql, kh, kl)
            out = _peer_experts(alpha, uth, up, dnt, lim, ea, r2, eb, x1.reshape(-1, d), m[5], lw2, lb2, tokens)
            return out.reshape(xin.shape)

        x_new = channel(x, ml_x, na_x, lat, t)
        if need_ctx:
            na_c = _na_ctx(nqc, nkc, nvc)
            ctx = channel(ctx, ml_c, na_c, cxm, t_c)
        x = x_new
    return x
```

```python
import functools

import jax
import jax.numpy as jnp
import numpy as np
from jax import lax
from jax.experimental import pallas as pl
from jax.experimental.pallas import tpu as pltpu

F32 = jnp.float32
BF16 = jnp.bfloat16

D_MODEL = 1024
GRID_W = 64
ML_HEADS = 4
ML_DV = 128
ML_DK = 64
ML_CHUNK = 64
CONV_W = 5
NA_HEADS = 8
NA_DH = 64
NA_WIN_H = 8
NA_WIN_W = 16
ROPE_BASE = 10000.0
PEER_HEADS = 8
PEER_NKEYS = 128
PEER_TOPK = 16
N_MOD = 6
ML_QK_W = 2 * ML_HEADS * ML_DK
ML_V_W = ML_HEADS * ML_DV
ML_G_W = 4 * ML_HEADS
NA_W = NA_HEADS * NA_DH
LN_EPS = 1e-5
NEG_BIG = -1e30

LANES = 128
SUBLANES = 8
VMEM_LIMIT = 56 * 1024 * 1024

NN = (((1,), (0,)), ((), ()))
NT = (((1,), (1,)), ((), ()))


def _dg(a, b, dims=NN):
    return lax.dot_general(a, b, dims, preferred_element_type=F32)


def _split_bf16(a):
    hi = a.astype(BF16)
    lo = (a - hi.astype(F32)).astype(BF16)
    return hi, lo


def _dot3(a, b, dims=NN):
    ah, al = _split_bf16(a)
    bh, bl = _split_bf16(b)
    return _dg(ah, bh, dims) + (_dg(ah, bl, dims) + _dg(al, bh, dims))


def _sigmoid(x):
    return 1.0 / (1.0 + jnp.exp(-x))


def _log_sigmoid(x):
    return jnp.minimum(x, 0.0) - jnp.log(1.0 + jnp.exp(-jnp.abs(x)))


def _layer_norm(z, w, b):
    mu = jnp.mean(z, axis=-1, keepdims=True)
    zc = z - mu
    var = jnp.mean(zc * zc, axis=-1, keepdims=True)
    return zc * lax.rsqrt(var + LN_EPS) * w + b


def _params(sem):
    return pltpu.CompilerParams(dimension_semantics=sem, vmem_limit_bytes=VMEM_LIMIT)


def _mod_kernel(c_ref, w_ref, b_ref, o_ref):
    c = c_ref[...]
    s = c * _sigmoid(c)
    o_ref[0] = _dot3(s, w_ref[0]) + b_ref[0]


def _modulation(cond, ada_w, ada_b):
    depth, d, n = ada_w.shape
    rows = cond.shape[0]
    tn = 1024
    return pl.pallas_call(
        _mod_kernel,
        grid=(depth, n // tn),
        in_specs=[
            pl.BlockSpec((rows, d), lambda l, j: (0, 0)),
            pl.BlockSpec((1, d, tn), lambda l, j: (l, 0, j)),
            pl.BlockSpec((1, 1, tn), lambda l, j: (l, 0, j)),
        ],
        out_specs=pl.BlockSpec((1, rows, tn), lambda l, j: (l, 0, j)),
        out_shape=jax.ShapeDtypeStruct((depth, rows, n), F32),
        compiler_params=_params(("parallel", "parallel")),
        name="modulation",
    )(cond, ada_w, ada_b.reshape(depth, 1, n))


def _inproj_kernel(x_ref, sc_ref, sh_ref, wm_ref, wg_ref, wgt_ref,
                   qk_ref, v_ref, o_ref, nq_ref, nk_ref, nv_ref, g_ref, gt_ref):
    u = x_ref[0] * (1.0 + sc_ref[0]) + sh_ref[0]
    p = _dg(u.astype(BF16), wm_ref[...])
    qk_ref[0] = p[:, 0:512]
    v_ref[0] = p[:, 512:1024].astype(BF16)
    o_ref[0] = p[:, 1024:1536].astype(BF16)
    nq_ref[0] = (p[:, 1536:2048] * (NA_DH ** -0.5)).astype(BF16)
    nk_ref[0] = p[:, 2048:2560].astype(BF16)
    nv_ref[0] = p[:, 2560:3072].astype(BF16)
    g_ref[0] = _dot3(u, wg_ref[...])
    gt_ref[0] = _dot3(wgt_ref[...], u, NT)


def _inproj(x, sc, sh, wm, wg, wgt):
    b, t, d = x.shape
    tm = min(t, 512)
    tok = lambda w, dt: jax.ShapeDtypeStruct((b, t, w), dt)
    blk = lambda w: pl.BlockSpec((1, tm, w), lambda i, j: (i, j, 0))
    return pl.pallas_call(
        _inproj_kernel,
        grid=(b, t // tm),
        in_specs=[
            blk(d),
            pl.BlockSpec((1, 1, d), lambda i, j: (i, 0, 0)),
            pl.BlockSpec((1, 1, d), lambda i, j: (i, 0, 0)),
            pl.BlockSpec(wm.shape, lambda i, j: (0, 0)),
            pl.BlockSpec(wg.shape, lambda i, j: (0, 0)),
            pl.BlockSpec(wgt.shape, lambda i, j: (0, 0)),
        ],
        out_specs=[blk(512), blk(512), blk(512), blk(512), blk(512), blk(512), blk(LANES),
                   pl.BlockSpec((1, ML_G_W, tm), lambda i, j: (i, 0, j))],
        out_shape=[tok(512, F32), tok(512, BF16), tok(512, BF16), tok(512, BF16), tok(512, BF16),
                   tok(512, BF16), tok(LANES, F32), jax.ShapeDtypeStruct((b, ML_G_W, t), F32)],
        compiler_params=_params(("parallel", "parallel")),
        name="inproj",
    )(x, sc, sh, wm, wg, wgt)


CONV_TILE = 128


def _conv_silu(qk_ref, cw_ref, pad_ref, t):
    width = qk_ref.shape[-1]
    zero = jnp.zeros((8, width), F32)
    pad_ref[pl.ds(0, 8), :] = zero
    pad_ref[pl.ds(8 + t, 8), :] = zero

    def copy(i, carry):
        r0 = pl.multiple_of(i * CONV_TILE, CONV_TILE)
        pad_ref[pl.ds(r0 + 8, CONV_TILE), :] = qk_ref[0, pl.ds(r0, CONV_TILE), :]
        return carry

    lax.fori_loop(0, t // CONV_TILE, copy, 0)


def _conv_tile(pad_ref, cw_ref, r0):
    n = CONV_TILE + 16
    win = pad_ref[pl.ds(r0, n), :]
    acc = None
    for j in range(CONV_W):
        k = 6 + j
        sh = pltpu.roll(win, n - k, axis=0)[:CONV_TILE]
        term = sh * cw_ref[pl.ds(j, 1), :]
        acc = term if acc is None else acc + term
    return acc * _sigmoid(acc)


def _rope_tile(y, cos, sin):
    lane = lax.broadcasted_iota(jnp.int32, (1, y.shape[1]), 1)
    first = (lane % ML_DK) < (ML_DK // 2)
    n = y.shape[1]
    partner = jnp.where(first, pltpu.roll(y, n - ML_DK // 2, axis=1), pltpu.roll(y, ML_DK // 2, axis=1))
    cos4 = jnp.concatenate([cos] * (n // LANES), axis=1)
    sin4 = jnp.concatenate([sin] * (n // LANES), axis=1)
    return y * cos4 + partner * sin4


def _mlstm_kernel(need_ctx, t_x, t_c,
                  qkx_ref, vx_ref, ox_ref, gx_ref, gtx_ref,
                  qkc_ref, vc_ref, oc_ref, gc_ref, gtc_ref,
                  cos_ref, sin_ref, cw_ref, ks_ref, bg_ref, bgt_ref, nw_ref,
                  *rest):
    if need_ctx:
        mlx_ref, mlc_ref = rest[:2]
        scratch = rest[2:]
    else:
        mlx_ref, mlc_ref = rest[0], None
        scratch = rest[1:]
    padx_ref, padc_ref, qsx_ref, qsc_ref, hfx_ref, hbx_ref, hfc_ref, hbc_ref, st_ref, m_ref = scratch

    _conv_silu(qkx_ref, cw_ref, padx_ref, t_x)
    _conv_silu(qkc_ref, cw_ref, padc_ref, t_c)
    kscale = ks_ref[...]

    def prep_x(i, carry):
        r0 = pl.multiple_of(i * CONV_TILE, CONV_TILE)
        y = _conv_tile(padx_ref, cw_ref, r0)
        y = _rope_tile(y, cos_ref[pl.ds(r0, CONV_TILE), :], sin_ref[pl.ds(r0, CONV_TILE), :])
        qsx_ref[pl.ds(r0, CONV_TILE), :] = y * kscale
        return carry

    def prep_c(i, carry):
        r0 = pl.multiple_of(i * CONV_TILE, CONV_TILE)
        qsc_ref[pl.ds(r0, CONV_TILE), :] = _conv_tile(padc_ref, cw_ref, r0) * kscale
        return carry

    lax.fori_loop(0, t_x // CONV_TILE, prep_x, 0)
    lax.fori_loop(0, t_c // CONV_TILE, prep_c, 0)

    st_ref[...] = jnp.zeros(st_ref.shape, F32)
    m_ref[...] = jnp.zeros(m_ref.shape, F32)

    L = ML_CHUNK
    row = lax.broadcasted_iota(jnp.int32, (L, L), 0)
    col = lax.broadcasted_iota(jnp.int32, (L, L), 1)
    lower = col <= row
    upper = col >= row
    ones_pad = (lax.broadcasted_iota(jnp.int32, (L, ML_DV), 1) == 0).astype(BF16)
    bg = bg_ref[...]
    bgt = bgt_ref[...]

    def chunk_pair(qs_ref, v_ref, g_ref, gt_ref, hf_ref, hb_ref, n_chunks, write_h):
        def body(i, carry):
            for d in range(2):
                c = i if d == 0 else n_chunks - 1 - i
                r0 = pl.multiple_of(c * L, L)
                mask = lower if d == 0 else upper
                mask_t = upper if d == 0 else lower
                qk = qs_ref[pl.ds(r0, L), :]
                vv = v_ref[0, pl.ds(r0, L), :]
                g = g_ref[0, pl.ds(r0, L), :][:, 0:ML_G_W] + bg
                gt = gt_ref[0, c] + bgt
                lsg = _log_sigmoid(g)
                lsgt = _log_sigmoid(gt)
                h_ref = hf_ref if d == 0 else hb_ref
                for h in range(ML_HEADS):
                    ci = 2 * d * ML_HEADS + h
                    fi = (2 * d + 1) * ML_HEADS + h
                    li_col, lf_col = g[:, ci:ci + 1], lsg[:, fi:fi + 1]
                    li_row, lf_row = gt[ci:ci + 1, :], lsgt[fi:fi + 1, :]
                    b_col = jnp.sum(jnp.where(mask, lf_row, 0.0), axis=1, keepdims=True)
                    b_row = jnp.sum(jnp.where(mask_t, lf_col, 0.0), axis=0, keepdims=True)
                    b_last = jnp.sum(lf_row, axis=1, keepdims=True)
                    sidx = d * ML_HEADS + h
                    m_in = m_ref[sidx][:, 0:1]
                    ct = st_ref[sidx]
                    q_h = qk[:, h * ML_DK:(h + 1) * ML_DK]
                    k_h = qk[:, ML_QK_W // 2 + h * ML_DK: ML_QK_W // 2 + (h + 1) * ML_DK]
                    v_ext = jnp.concatenate([vv[:, h * ML_DV:(h + 1) * ML_DV], ones_pad], axis=1)
                    if write_h:
                        dm = jnp.where(mask, b_col - b_row + li_row, NEG_BIG)
                        m_inter = b_col + m_in
                        m_j = jnp.maximum(m_inter, jnp.max(dm, axis=1, keepdims=True))
                        s = _dg(q_h.astype(BF16), k_h.astype(BF16), NT) * jnp.exp(dm - m_j)
                        inter = jnp.exp(m_inter - m_j)
                        ne = _dg(s.astype(BF16), v_ext) + inter * _dg(q_h.astype(BF16), ct.astype(BF16))
                        den = jnp.maximum(jnp.abs(ne[:, ML_DV:ML_DV + 1]), jnp.exp(-m_j))
                        h_ref[pl.ds(r0, L), h * ML_DV:(h + 1) * ML_DV] = ne[:, 0:ML_DV] / den
                    a_col = b_last - b_col + li_col
                    m_loc = jnp.max(a_col, axis=0, keepdims=True)
                    kw = (k_h * jnp.exp(a_col - m_loc)).T.astype(BF16)
                    c_loc = _dg(kw, v_ext)
                    m_new = jnp.maximum(b_last + m_in, m_loc)
                    st_ref[sidx] = jnp.exp(b_last + m_in - m_new) * ct + jnp.exp(m_loc - m_new) * c_loc
                    m_ref[sidx] = jnp.broadcast_to(m_new, (1, LANES))
            return carry

        lax.fori_loop(0, n_chunks, body, 0)

    chunk_pair(qsc_ref, vc_ref, gc_ref, gtc_ref, hfc_ref, hbc_ref, t_c // L, need_ctx)
    chunk_pair(qsx_ref, vx_ref, gx_ref, gtx_ref, hfx_ref, hbx_ref, t_x // L, True)

    nw = nw_ref[...]

    def finish(hf_ref, hb_ref, o_ref, out_ref, t):
        def body(i, carry):
            r0 = pl.multiple_of(i * CONV_TILE, CONV_TILE)
            hsum = hf_ref[pl.ds(r0, CONV_TILE), :] + hb_ref[pl.ds(r0, CONV_TILE), :]
            parts = []
            for h in range(ML_HEADS):
                hh = hsum[:, h * ML_DV:(h + 1) * ML_DV]
                mu = jnp.mean(hh, axis=1, keepdims=True)
                hc = hh - mu
                var = jnp.mean(hc * hc, axis=1, keepdims=True)
                parts.append(hc * lax.rsqrt(var + LN_EPS))
            y = jnp.concatenate(parts, axis=1) * nw
            y = y * _sigmoid(o_ref[0, pl.ds(r0, CONV_TILE), :].astype(F32))
            out_ref[0, pl.ds(r0, CONV_TILE), :] = y.astype(BF16)
            return carry

        lax.fori_loop(0, t // CONV_TILE, body, 0)

    finish(hfx_ref, hbx_ref, ox_ref, mlx_ref, t_x)
    if need_ctx:
        finish(hfc_ref, hbc_ref, oc_ref, mlc_ref, t_c)


def _mlstm(need_ctx, px, pc, cos, sin, conv_w, kscale, b_gate, ml_norm_w):
    qkx, vx, ox, gx, gtx = px
    qkc, vc, oc, gc, gtc = pc
    b, t_x, _ = qkx.shape
    t_c = qkc.shape[1]
    L = ML_CHUNK
    gtx = gtx.reshape(b, ML_G_W, t_x // L, L).transpose(0, 2, 1, 3)
    gtc = gtc.reshape(b, ML_G_W, t_c // L, L).transpose(0, 2, 1, 3)
    cw = jnp.zeros((8, ML_QK_W), F32).at[:CONV_W].set(conv_w)
    tokx = lambda w: pl.BlockSpec((1, t_x, w), lambda i: (i, 0, 0))
    tokc = lambda w: pl.BlockSpec((1, t_c, w), lambda i: (i, 0, 0))
    const = lambda a: pl.BlockSpec(a.shape, lambda i: (0,) * a.ndim)
    bg = b_gate.reshape(1, ML_G_W)
    bgt = b_gate.reshape(ML_G_W, 1)
    nw = ml_norm_w.reshape(1, ML_V_W)
    out_specs = [tokx(ML_V_W)]
    out_shape = [jax.ShapeDtypeStruct((b, t_x, ML_V_W), BF16)]
    if need_ctx:
        out_specs.append(tokc(ML_V_W))
        out_shape.append(jax.ShapeDtypeStruct((b, t_c, ML_V_W), BF16))
    outs = pl.pallas_call(
        functools.partial(_mlstm_kernel, need_ctx, t_x, t_c),
        grid=(b,),
        in_specs=[
            tokx(ML_QK_W), tokx(ML_V_W), tokx(ML_V_W), tokx(LANES),
            pl.BlockSpec((1, t_x // L, ML_G_W, L), lambda i: (i, 0, 0, 0)),
            tokc(ML_QK_W), tokc(ML_V_W), tokc(ML_V_W), tokc(LANES),
            pl.BlockSpec((1, t_c // L, ML_G_W, L), lambda i: (i, 0, 0, 0)),
            const(cos), const(sin), const(cw), const(kscale), const(bg), const(bgt), const(nw),
        ],
        out_specs=out_specs,
        out_shape=out_shape,
        scratch_shapes=[
            pltpu.VMEM((t_x + 16, ML_QK_W), F32), pltpu.VMEM((t_c + 16, ML_QK_W), F32),
            pltpu.VMEM((t_x, ML_QK_W), F32), pltpu.VMEM((t_c, ML_QK_W), F32),
            pltpu.VMEM((t_x, ML_V_W), F32), pltpu.VMEM((t_x, ML_V_W), F32),
            pltpu.VMEM((t_c, ML_V_W), F32), pltpu.VMEM((t_c, ML_V_W), F32),
            pltpu.VMEM((2 * ML_HEADS, ML_DK, 2 * ML_DV), F32),
            pltpu.VMEM((2 * ML_HEADS, 1, LANES), F32),
        ],
        compiler_params=_params(("parallel",)),
        name="mlstm",
    )(qkx, vx, ox, gx, gtx, qkc, vc, oc, gc, gtc, cos, sin, cw, kscale, bg, bgt, nw)
    return (outs[0], outs[1]) if need_ctx else (outs[0], None)


def _rope_tables(t):
    pos = np.arange(t)
    n_freq = ML_DK // 4
    inv = ROPE_BASE ** (-np.arange(n_freq, dtype=np.float32) / n_freq)
    ang = np.concatenate([(pos // GRID_W)[:, None] * inv, (pos % GRID_W)[:, None] * inv], -1).astype(np.float32)
    cos, sin = np.cos(ang), np.sin(ang)
    cos_h = np.concatenate([cos, cos], -1)
    sin_h = np.concatenate([-sin, sin], -1)
    return (jnp.asarray(np.concatenate([cos_h, cos_h], -1), F32),
            jnp.asarray(np.concatenate([sin_h, sin_h], -1), F32))


NA_GROUP = 4
NA_BAND = NA_WIN_H + NA_GROUP - 1
NA_FINISH_TILE = 512


def _head_masks():
    lane = lax.broadcasted_iota(jnp.int32, (1, 2 * NA_DH), 1)
    return lane < NA_DH, lane >= NA_DH


def _na_kernel(rows, q_ref, k_ref, v_ref, kc_ref, vc_ref, bm_ref, o_ref, sctx_ref, pctx_ref, oloc_ref, den_ref):
    masks = _head_masks()
    t = rows * GRID_W
    n_groups = rows // NA_GROUP
    gq = NA_GROUP * GRID_W
    band = NA_BAND * GRID_W
    kc = kc_ref[0]
    vc = vc_ref[0]

    for hh in range(2):
        q_all = q_ref[0]
        sctx_ref[hh] = _dg(jnp.where(masks[hh], q_all, jnp.zeros_like(q_all)), kc, NT)

    def group(g, carry):
        b0 = jnp.clip(NA_GROUP * g - NA_WIN_H // 2, 0, rows - NA_BAND)
        kind = jnp.where(g == 0, 0, jnp.where(g == n_groups - 1, 2, 1))
        qrows = pl.ds(pl.multiple_of(g * gq, gq), gq)
        krows = pl.ds(pl.multiple_of(b0 * GRID_W, GRID_W), band)
        q = q_ref[0, qrows, :]
        kb = k_ref[0, krows, :]
        vb = v_ref[0, krows, :]
        for hh in range(2):
            qm = jnp.where(masks[hh], q, jnp.zeros_like(q))
            s_loc = _dg(qm, kb, NT) + bm_ref[hh, kind]
            s_ctx = sctx_ref[hh, qrows, :]
            m = jnp.maximum(jnp.max(s_loc, axis=1, keepdims=True), jnp.max(s_ctx, axis=1, keepdims=True))
            p_loc = jnp.exp(s_loc - m)
            p_ctx = jnp.exp(s_ctx - m)
            den = jnp.sum(p_loc, axis=1, keepdims=True) + jnp.sum(p_ctx, axis=1, keepdims=True)
            pctx_ref[hh, qrows, :] = p_ctx.astype(BF16)
            den_ref[hh, qrows, :] = jnp.broadcast_to(den, (gq, 2 * NA_DH))
            oloc_ref[hh, qrows, :] = _dg(p_loc.astype(BF16), vb)
        return carry

    lax.fori_loop(0, n_groups, group, 0)

    def finish(i, carry):
        trows = pl.ds(pl.multiple_of(i * NA_FINISH_TILE, NA_FINISH_TILE), NA_FINISH_TILE)
        outs = []
        for hh in range(2):
            o = oloc_ref[hh, trows, :] + _dg(pctx_ref[hh, trows, :], vc)
            outs.append(o / den_ref[hh, trows, :])
        o_ref[0, trows, :] = jnp.where(masks[0], outs[0], outs[1]).astype(BF16)
        return carry

    lax.fori_loop(0, t // NA_FINISH_TILE, finish, 0)


def _na_latent(nq, nk, nv, nkc, nvc, bias):
    b, t, _ = nq.shape
    t_c = nkc.shape[1]
    rows = t // GRID_W
    assert rows % NA_GROUP == 0 and rows >= NA_BAND and t % NA_FINISH_TILE == 0
    tok = lambda tt: pl.BlockSpec((1, tt, 2 * NA_DH), lambda i, j: (i, 0, j))
    return pl.pallas_call(
        functools.partial(_na_kernel, rows),
        grid=(b, NA_HEADS // 2),
        in_specs=[tok(t), tok(t), tok(t), tok(t_c), tok(t_c),
                  pl.BlockSpec((2,) + bias.shape[1:], lambda i, j: (j, 0, 0, 0))],
        out_specs=tok(t),
        out_shape=jax.ShapeDtypeStruct((b, t, NA_W), BF16),
        scratch_shapes=[pltpu.VMEM((2, t, t_c), F32), pltpu.VMEM((2, t, t_c), BF16),
                        pltpu.VMEM((2, t, 2 * NA_DH), F32), pltpu.VMEM((2, t, 2 * NA_DH), F32)],
        compiler_params=_params(("parallel", "parallel")),
        name="na_latent",
    )(nq, nk, nv, nkc, nvc, bias)


def _nactx_kernel(q_ref, k_ref, v_ref, o_ref):
    masks = _head_masks()
    q, k, v = q_ref[0], k_ref[0], v_ref[0]
    outs = []
    for hh in range(2):
        qm = jnp.where(masks[hh], q, jnp.zeros_like(q))
        s = _dg(qm, k, NT)
        p = jnp.exp(s - jnp.max(s, axis=1, keepdims=True))
        outs.append(_dg(p.astype(BF16), v) / jnp.sum(p, axis=1, keepdims=True))
    o_ref[0] = jnp.where(masks[0], outs[0], outs[1]).astype(BF16)


def _na_ctx(nqc, nkc, nvc):
    b, t_c, _ = nqc.shape
    tok = pl.BlockSpec((1, t_c, 2 * NA_DH), lambda i, j: (i, 0, j))
    return pl.pallas_call(
        _nactx_kernel,
        grid=(b, NA_HEADS // 2),
        in_specs=[tok, tok, tok],
        out_specs=tok,
        out_shape=jax.ShapeDtypeStruct((b, t_c, NA_W), BF16),
        compiler_params=_params(("parallel", "parallel")),
        name="na_ctx",
    )(nqc, nkc, nvc)


def _na_bias(rpb):
    c = np.arange(GRID_W)
    win_lo = np.clip(c - NA_WIN_W // 2, 0, GRID_W - NA_WIN_W)
    ok = (c[None, :] >= win_lo[:, None]) & (c[None, :] < win_lo[:, None] + NA_WIN_W)
    span = np.clip(np.arange(2 * GRID_W - 1) - (GRID_W - 1), 1 - NA_WIN_W, NA_WIN_W - 1) + NA_WIN_W - 1
    ext = rpb[:, :, span]
    cols = jnp.stack([ext[:, :, GRID_W - 1 - q: 2 * GRID_W - 1 - q] for q in range(GRID_W)], axis=2)
    cols = jnp.where(ok[None, None], cols, NEG_BIG).transpose(0, 2, 1, 3)
    half = NA_WIN_H // 2
    kinds = [lambda rq: (rq, 0),
             lambda rq: (half + rq, rq),
             lambda rq: (NA_BAND - NA_GROUP + rq, NA_BAND - NA_WIN_H)]
    pieces = []
    for kind in kinds:
        for rq in range(NA_GROUP):
            q_rel, lo = kind(rq)
            first = lo - q_rel + NA_WIN_H - 1
            window = cols[:, :, first:first + NA_WIN_H]
            pad = ((0, 0), (0, 0), (lo, NA_BAND - NA_WIN_H - lo), (0, 0))
            pieces.append(jnp.pad(window, pad, constant_values=NEG_BIG))
    bias = jnp.stack(pieces, axis=1)
    return bias.reshape(NA_HEADS, len(kinds), NA_GROUP * GRID_W, NA_BAND * GRID_W)


def _outproj_kernel(alpha, ml_ref, na_ref, x_ref, g1_ref, w_ref, lw_ref, lb_ref, sc_ref, sh_ref,
                    x1_ref, ut_ref):
    a = jnp.concatenate([ml_ref[0], na_ref[0]], axis=1)
    y = _dg(a, w_ref[...])
    x1 = _layer_norm(alpha * x_ref[0] + g1_ref[0] * y, lw_ref[...], lb_ref[...])
    x1_ref[0] = x1
    ut_ref[...] = (x1 * (1.0 + sc_ref[0]) + sh_ref[0]).T.astype(BF16)


def _outproj(alpha, ml, na, x, g1, w_out, lw, lb, sc2, sh2):
    b, t, d = x.shape
    tm = min(t, 512)
    blk = lambda w: pl.BlockSpec((1, tm, w), lambda i, j: (i, j, 0))
    per_b = pl.BlockSpec((1, 1, d), lambda i, j: (i, 0, 0))
    const = lambda a: pl.BlockSpec(a.shape, lambda i, j: (0,) * a.ndim)
    tposed = pl.BlockSpec((d, tm), lambda i, j: (0, i * (t // tm) + j))
    return pl.pallas_call(
        functools.partial(_outproj_kernel, alpha),
        grid=(b, t // tm),
        in_specs=[blk(ML_V_W), blk(NA_W), blk(d), per_b, const(w_out), const(lw), const(lb), per_b, per_b],
        out_specs=[blk(d), tposed],
        out_shape=[jax.ShapeDtypeStruct((b, t, d), F32), jax.ShapeDtypeStruct((d, b * t), BF16)],
        compiler_params=_params(("parallel", "parallel")),
        name="outproj",
    )(ml, na, x, g1, w_out, lw, lb, sc2, sh2)


ROUTE_TB = 256
ROUTE_GROUP = 4
CAND_ROWS = 16 + 7 * 8 + 8


def _cand_tables():
    a_idx = np.zeros(CAND_ROWS, np.int64)
    b_idx = np.zeros(CAND_ROWS, np.int64)
    a_idx[0:16], b_idx[0:16] = 0, np.arange(16)
    for a in range(1, 8):
        a_idx[16 + 8 * (a - 1): 24 + 8 * (a - 1)] = a
        b_idx[16 + 8 * (a - 1): 24 + 8 * (a - 1)] = np.arange(8)
    a_idx[72:80], b_idx[72:80] = np.arange(8, 16), 0
    valid = (a_idx + 1) * (b_idx + 1) <= PEER_TOPK
    flat = (a_idx * PEER_TOPK + b_idx).astype(np.float32)
    neg = np.where(valid, 0.0, -np.inf).astype(np.float32)
    tile = lambda v: jnp.asarray(np.tile(v[:, None], (1, ROUTE_TB)), F32)
    return tile(flat), tile(neg)


def _tree(op, xs):
    while len(xs) > 1:
        xs = [op(xs[i], xs[i + 1]) for i in range(0, len(xs) - 1, 2)] + ([xs[-1]] if len(xs) % 2 else [])
    return xs[0]


def _top_rounds(s, orders=None, order_end=None):
    n = s.shape[0] // SUBLANES
    vals = [s[SUBLANES * i:SUBLANES * (i + 1)] for i in range(n)]
    ranks = [jnp.full(vals[0].shape, float(PEER_TOPK), F32)] * n
    tops = []
    for k in range(PEER_TOPK):
        m = jnp.max(_tree(jnp.maximum, vals), axis=0, keepdims=True)
        hits = [v == m for v in vals]
        if orders is not None:
            first = _tree(jnp.minimum, [jnp.where(h, o, order_end) for h, o in zip(hits, orders)])
            first = jnp.min(first, axis=0, keepdims=True)
            hits = [o == first for o in orders]
        ranks = [jnp.where(h, float(k), r) for h, r in zip(hits, ranks)]
        vals = [jnp.where(h, -jnp.inf, v) for h, v in zip(hits, vals)]
        tops.append(m)
    return jnp.concatenate(ranks, axis=0), tops


def _one_per_round(rank):
    taken = jnp.sum(jnp.where(rank < float(PEER_TOPK), 1.0, 0.0), axis=0, keepdims=True)
    return jnp.max(jnp.abs(taken - float(PEER_TOPK))) == 0.0


def _sort16_network():
    pairs = []

    def merge(lo, n, r):
        step = r * 2
        if step < n:
            merge(lo, n, step)
            merge(lo + r, n, step)
            pairs.extend((i, i + r) for i in range(lo + r, lo + n - r, step))
        else:
            pairs.append((lo, lo + r))

    def sort(lo, n):
        if n > 1:
            sort(lo, n // 2)
            sort(lo + n // 2, n // 2)
            merge(lo, n, 1)

    sort(0, PEER_TOPK)
    return pairs


def _larger_smaller(a, b):
    if b is None:
        return a, None
    if a is None:
        return b, None
    return jnp.maximum(a, b), jnp.minimum(a, b)


def _top_sorted(slabs):
    v = list(slabs) + [None] * (PEER_TOPK - len(slabs))
    for i, j in _sort16_network():
        v[i], v[j] = _larger_smaller(v[i], v[j])
    for shift in (4, 2, 1):
        moved = [None if a is None else pltpu.roll(a, shift, axis=0) for a in v]
        v = [_larger_smaller(v[k], moved[PEER_TOPK - 1 - k])[0] for k in range(PEER_TOPK)]
        for d in (8, 4, 2, 1):
            for i in range(PEER_TOPK):
                if i & d == 0:
                    v[i], v[i + d] = _larger_smaller(v[i], v[i + d])
    return v


def _rank_by_count(s, tops):
    n = s.shape[0] // SUBLANES
    ranks = []
    for i in range(n):
        v = s[SUBLANES * i:SUBLANES * (i + 1)]
        r = jnp.zeros(v.shape, F32)
        for a, t in enumerate(tops):
            r = jnp.where(t > v, float(a + 1), r)
        ranks.append(r)
    return jnp.concatenate(ranks, axis=0)


def _strictly_descending(tops):
    steps = [jnp.where(a > b, 1.0, 0.0) for a, b in zip(tops[:-1], tops[1:])]
    return jnp.min(_tree(jnp.minimum, steps)) == 1.0


def _route_kernel(ut_ref, wq_ref, kh_ref, kl_ref, flat_ref, neg_ref,
                  lim_ref, ea_ref, r2_ref, eb_ref, qt_ref, rank_ref, t_ref, e_ref, pick_ref):
    tb = ut_ref.shape[1]
    nk = PEER_NKEYS
    qt_ref[...] = _dg(wq_ref[...], ut_ref[...])
    sub_iota = lax.broadcasted_iota(jnp.int32, (SUBLANES, tb), 0).astype(F32)
    key_order = [sub_iota + float(SUBLANES * i) for i in range(nk // SUBLANES)]

    def sub_scores(i, carry):
        todo = []
        for g in range(ROUTE_GROUP):
            hh = ROUTE_GROUP * i + g
            q = qt_ref[pl.ds(pl.multiple_of(hh * nk, nk), nk), :]
            qh, ql = _split_bf16(q)
            kh = kh_ref[hh]
            s0 = _dg(kh, qh) + (_dg(kh, ql) + _dg(kl_ref[hh], qh))
            tops = _top_sorted([s0[SUBLANES * r:SUBLANES * (r + 1)] for r in range(nk // SUBLANES)])
            rank = _rank_by_count(s0, tops)
            rank_ref[hh] = rank
            t_ref[hh] = jnp.concatenate([t[0:1] for t in tops], axis=0)
            e_ref[hh] = jnp.exp(s0 - tops[0][0:1])
            todo.append((hh, s0, jnp.logical_and(_strictly_descending(tops), _one_per_round(rank))))

        @pl.when(jnp.logical_not(functools.reduce(jnp.logical_and, [ok for _, _, ok in todo])))
        def _():
            for hh, s0, _ in todo:
                rank_t, tops_t = _top_rounds(s0, key_order, float(nk))
                rank_ref[hh] = rank_t
                t_ref[hh] = jnp.concatenate(tops_t, axis=0)

        return carry

    lax.fori_loop(0, 2 * PEER_HEADS // ROUTE_GROUP, sub_scores, 0)

    def joint(i, carry):
        todo = []
        for g in range(ROUTE_GROUP):
            h = ROUTE_GROUP * i + g
            t1 = t_ref[2 * h]
            t2 = t_ref[2 * h + 1]
            blocks = [t1[0:1] + t2]
            for a in range(1, 8):
                blocks.append(t1[a:a + 1] + t2[0:8])
            blocks.append(t1[8:16] + t2[0:1])
            cand = jnp.concatenate(blocks, axis=0) + neg_ref[...]
            cut = _top_sorted([cand[SUBLANES * r:SUBLANES * (r + 1)] for r in range(CAND_ROWS // SUBLANES)])[-1]
            pick = jnp.where(cand >= cut[0:1], 0.0, float(PEER_TOPK))
            pick_ref[g] = pick
            todo.append((h, cand, t1[0:1] + t2[0:1], _one_per_round(pick)))

        @pl.when(jnp.logical_not(functools.reduce(jnp.logical_and, [ok for _, _, _, ok in todo])))
        def _():
            flat = [flat_ref[SUBLANES * r:SUBLANES * (r + 1), :] for r in range(CAND_ROWS // SUBLANES)]
            for g, (_, cand, _, _) in enumerate(todo):
                pick_ref[g] = _top_rounds(cand, flat, 1e9)[0]

        for g, (h, cand, top, _) in enumerate(todo):
            chosen = jnp.where(pick_ref[g] < float(PEER_TOPK), 1.0, 0.0)
            z = jnp.sum(chosen * jnp.exp(cand - top), axis=0, keepdims=True)
            counts = [jnp.sum(chosen[0:16], axis=0, keepdims=True)]
            for a in range(1, 8):
                counts.append(jnp.sum(chosen[16 + 8 * (a - 1): 24 + 8 * (a - 1)], axis=0, keepdims=True))
            tail = chosen[72:80]
            for a in range(8):
                counts.append(tail[a:a + 1])
            rank1 = rank_ref[2 * h].astype(BF16)
            lim = jnp.zeros((nk, tb), BF16)
            for a in range(PEER_TOPK):
                lim = jnp.where(rank1 == float(a), jnp.broadcast_to(counts[a], (nk, tb)).astype(BF16), lim)
            lim_ref[h] = lim.astype(F32)
            ea_ref[h] = e_ref[2 * h] * (1.0 / z)
            r2_ref[h] = rank_ref[2 * h + 1].astype(BF16)
            eb_ref[h] = e_ref[2 * h + 1].astype(BF16)
        return carry

    lax.fori_loop(0, PEER_HEADS // ROUTE_GROUP, joint, 0)


def _route(ut, wq, kh, kl):
    d, n = ut.shape
    tb = ROUTE_TB
    flat, neg = _cand_tables()
    nk = PEER_NKEYS
    const = lambda a: pl.BlockSpec(a.shape, lambda i: (0,) * a.ndim)
    tok = pl.BlockSpec((d, tb), lambda i: (0, i))
    out_blk = pl.BlockSpec((PEER_HEADS, nk, tb), lambda i: (0, 0, i))
    meta = lambda dt: jax.ShapeDtypeStruct((PEER_HEADS, nk, n), dt)
    return pl.pallas_call(
        _route_kernel,
        grid=(n // tb,),
        in_specs=[tok, const(wq), const(kh), const(kl), const(flat), const(neg)],
        out_specs=[out_blk, out_blk, out_blk, out_blk],
        out_shape=[meta(F32), meta(F32), meta(BF16), meta(BF16)],
        scratch_shapes=[
            pltpu.VMEM((2 * PEER_HEADS * nk, tb), F32),
            pltpu.VMEM((2 * PEER_HEADS, nk, tb), F32),
            pltpu.VMEM((2 * PEER_HEADS, PEER_TOPK, tb), F32),
            pltpu.VMEM((2 * PEER_HEADS, nk, tb), F32),
            pltpu.VMEM((ROUTE_GROUP, CAND_ROWS, tb), F32),
        ],
        compiler_params=_params(("parallel",)),
        name="peer_route",
    )(ut, wq, kh, kl, flat, neg)


PEER_TB = 512
PEER_EB = 1024


def _gelu_tanh(x):
    return 0.5 * x * (1.0 + jnp.tanh(0.7978845608028654 * (x + 0.044715 * (x * x * x))))


def _peer_kernel(alpha, n_eb, ut_ref, up_ref, dnt_ref, lim_ref, ea_ref, r2_ref, eb_ref,
                 x1_ref, g2_ref, lw_ref, lb_ref, out_ref, acc_ref, z_ref):
    e = pl.program_id(1)
    nk = PEER_NKEYS
    tb = ut_ref.shape[1]

    @pl.when(e == 0)
    def _():
        acc_ref[...] = jnp.zeros(acc_ref.shape, F32)

    ht = _dg(up_ref[...], ut_ref[...])
    zero = jnp.zeros((nk, tb), BF16)
    for j in range(PEER_EB // nk):
        act = _gelu_tanh(ht[j * nk:(j + 1) * nk].astype(BF16))
        gate = None
        for h in range(PEER_HEADS):
            lim = jnp.broadcast_to(lim_ref[h, j:j + 1, :], (nk, tb)).astype(BF16)
            ea = jnp.broadcast_to(ea_ref[h, j:j + 1, :], (nk, tb)).astype(BF16)
            term = jnp.where(r2_ref[h] < lim, eb_ref[h], zero) * ea
            gate = term if gate is None else gate + term
        z_ref[j * nk:(j + 1) * nk, :] = act * gate
    acc_ref[...] += _dg(dnt_ref[...], z_ref[...])

    @pl.when(e == n_eb - 1)
    def _():
        y = acc_ref[...].T
        out_ref[...] = _layer_norm(alpha * x1_ref[...] + g2_ref[0] * y, lw_ref[...], lb_ref[...])


def _peer_experts(alpha, ut, up, dnt, lim, ea, r2, eb, x1, g2, lw, lb, tokens_per_batch):
    n, d = x1.shape
    tb = min(PEER_TB, tokens_per_batch)
    n_exp = up.shape[0]
    n_eb = n_exp // PEER_EB
    nk = PEER_NKEYS
    keys_per_step = PEER_EB // nk
    tok = pl.BlockSpec((tb, d), lambda i, e: (i, 0))
    row_meta = pl.BlockSpec((PEER_HEADS, keys_per_step, tb), lambda i, e: (0, e, i))
    col_meta = pl.BlockSpec((PEER_HEADS, nk, tb), lambda i, e: (0, 0, i))
    const = lambda a: pl.BlockSpec(a.shape, lambda i, e: (0,) * a.ndim)
    return pl.pallas_call(
        functools.partial(_peer_kernel, alpha, n_eb),
        grid=(n // tb, n_eb),
        in_specs=[pl.BlockSpec((d, tb), lambda i, e: (0, i)),
                  pl.BlockSpec((PEER_EB, d), lambda i, e: (e, 0)),
                  pl.BlockSpec((d, PEER_EB), lambda i, e: (0, e)),
                  row_meta, row_meta, col_meta, col_meta,
                  tok,
                  pl.BlockSpec((1, 1, d), lambda i, e: (i * tb // tokens_per_batch, 0, 0)),
                  const(lw), const(lb)],
        out_specs=tok,
        out_shape=jax.ShapeDtypeStruct((n, d), F32),
        scratch_shapes=[pltpu.VMEM((d, tb), F32), pltpu.VMEM((PEER_EB, tb), BF16)],
        compiler_params=_params(("parallel", "arbitrary")),
        name="peer_experts",
    )(ut, up, dnt, lim, ea, r2, eb, x1, g2, lw, lb)


def kernel(x, c, ctx, c_ctx, ada_w, ada_b, w_in, b_gate, conv_w, ml_norm_w, na_rpb, w_out, ln1_w, ln1_b,
           peer_wq, peer_keys, peer_up, peer_down, ln2_w, ln2_b):
    depth = ada_w.shape[0]
    b, t, d = x.shape
    t_c = ctx.shape[1]
    alpha = (2.0 * depth) ** 0.25
    rows = t // GRID_W

    n_cond = b + 1
    cond = jnp.zeros((-(-n_cond // 8) * 8, d), F32).at[:b].set(c).at[b].set(c_ctx)
    mods = _modulation(cond, ada_w, ada_b)
    cos, sin = _rope_tables(t)
    kscale = jnp.concatenate([jnp.ones((1, ML_QK_W // 2), F32), jnp.full((1, ML_QK_W // 2), ML_DK ** -0.5, F32)], 1)

    g_lo = ML_QK_W + 2 * ML_V_W
    g_hi = g_lo + ML_G_W
    for l in range(depth):
        need_ctx = l < depth - 1
        mod = mods[l].reshape(-1, N_MOD, d)
        lat = [mod[:b, i].reshape(b, 1, d) for i in range(N_MOD)]
        cxm = [jnp.broadcast_to(mod[b, i].reshape(1, 1, d), (b, 1, d)) for i in range(N_MOD)]

        w = w_in[l]
        wm = jnp.concatenate([w[:, :g_lo], w[:, g_hi:]], axis=1).astype(BF16)
        wg = jnp.zeros((d, LANES), F32).at[:, :ML_G_W].set(w[:, g_lo:g_hi])
        wgt = w[:, g_lo:g_hi].T

        qkx, vx, ox, nqx, nkx, nvx, gx, gtx = _inproj(x, lat[1], lat[0], wm, wg, wgt)
        qkc, vc, oc, nqc, nkc, nvc, gc, gtc = _inproj(ctx, cxm[1], cxm[0], wm, wg, wgt)

        ml_x, ml_c = _mlstm(need_ctx, (qkx, vx, ox, gx, gtx), (qkc, vc, oc, gc, gtc),
                            cos, sin, conv_w[l], kscale, b_gate[l], ml_norm_w[l])
        na_x = _na_latent(nqx, nkx, nvx, nkc, nvc, _na_bias(na_rpb[l]))

        wo = w_out[l].astype(BF16)
        lw1, lb1 = ln1_w[l].reshape(1, d), ln1_b[l].reshape(1, d)
        lw2, lb2 = ln2_w[l].reshape(1, d), ln2_b[l].reshape(1, d)
        wq_t = peer_wq[l].T.astype(BF16)
        keys = peer_keys[l].reshape(2 * PEER_HEADS, PEER_NKEYS, -1)
        kh = keys.astype(BF16)
        kl = (keys - kh.astype(F32)).astype(BF16)
        up = peer_up[l].astype(BF16)
        dnt = peer_down[l].T.astype(BF16)

        def channel(xin, ml, na, m, tokens):
            x1, ut = _outproj(alpha, ml, na, xin, m[2], wo, lw1, lb1, m[4], m[3])
            lim, ea, r2, eb = _route(ut, wq_t, kh, kl)
            out = _peer_experts(alpha, ut, up, dnt, lim, ea, r2, eb, x1.reshape(-1, d), m[5], lw2, lb2, tokens)
            return out.reshape(xin.shape)

        x_new = channel(x, ml_x, na_x, lat, t)
        if need_ctx:
            na_c = _na_ctx(nqc, nkc, nvc)
            ctx = channel(ctx, ml_c, na_c, cxm, t_c)
        x = x_new
    return x
```

```python
import functools

import jax
import jax.numpy as jnp
import numpy as np
from jax import lax
from jax.experimental import pallas as pl
from jax.experimental.pallas import tpu as pltpu

F32 = jnp.float32
BF16 = jnp.bfloat16

D_MODEL = 1024
GRID_W = 64
ML_HEADS = 4
ML_DV = 128
ML_DK = 64
ML_CHUNK = 256
CONV_W = 5
NA_HEADS = 8
NA_DH = 64
NA_WIN_H = 8
NA_WIN_W = 16
ROPE_BASE = 10000.0
PEER_HEADS = 8
PEER_NKEYS = 128
PEER_TOPK = 16
N_MOD = 6
ML_QK_W = 2 * ML_HEADS * ML_DK
ML_V_W = ML_HEADS * ML_DV
ML_G_W = 4 * ML_HEADS
NA_W = NA_HEADS * NA_DH
LN_EPS = 1e-5
NEG_BIG = -1e30

LANES = 128
SUBLANES = 8
VMEM_LIMIT = 56 * 1024 * 1024

NN = (((1,), (0,)), ((), ()))
NT = (((1,), (1,)), ((), ()))


def _dg(a, b, dims=NN):
    return lax.dot_general(a, b, dims, preferred_element_type=F32)


def _split_bf16(a):
    hi = a.astype(BF16)
    lo = (a - hi.astype(F32)).astype(BF16)
    return hi, lo


def _dot3(a, b, dims=NN):
    ah, al = _split_bf16(a)
    bh, bl = _split_bf16(b)
    return _dg(ah, bh, dims) + (_dg(ah, bl, dims) + _dg(al, bh, dims))


def _sigmoid(x):
    return 1.0 / (1.0 + jnp.exp(-x))


def _log_sigmoid(x):
    return jnp.minimum(x, 0.0) - jnp.log(1.0 + jnp.exp(-jnp.abs(x)))


def _layer_norm(z, w, b):
    mu = jnp.mean(z, axis=-1, keepdims=True)
    zc = z - mu
    var = jnp.mean(zc * zc, axis=-1, keepdims=True)
    return zc * lax.rsqrt(var + LN_EPS) * w + b


def _params(sem):
    return pltpu.CompilerParams(dimension_semantics=sem, vmem_limit_bytes=VMEM_LIMIT)


def _mod_kernel(c_ref, w_ref, b_ref, o_ref):
    c = c_ref[...]
    s = c * _sigmoid(c)
    o_ref[0] = _dot3(s, w_ref[0]) + b_ref[0]


def _modulation(cond, ada_w, ada_b):
    depth, d, n = ada_w.shape
    rows = cond.shape[0]
    tn = 1024
    return pl.pallas_call(
        _mod_kernel,
        grid=(depth, n // tn),
        in_specs=[
            pl.BlockSpec((rows, d), lambda l, j: (0, 0)),
            pl.BlockSpec((1, d, tn), lambda l, j: (l, 0, j)),
            pl.BlockSpec((1, 1, tn), lambda l, j: (l, 0, j)),
        ],
        out_specs=pl.BlockSpec((1, rows, tn), lambda l, j: (l, 0, j)),
        out_shape=jax.ShapeDtypeStruct((depth, rows, n), F32),
        compiler_params=_params(("parallel", "parallel")),
        name="modulation",
    )(cond, ada_w, ada_b.reshape(depth, 1, n))


def _inproj_kernel(x_ref, sc_ref, sh_ref, wm_ref, wg_ref, wgt_ref,
                   qk_ref, v_ref, o_ref, nq_ref, nk_ref, nv_ref, g_ref, gt_ref):
    u = x_ref[0] * (1.0 + sc_ref[0]) + sh_ref[0]
    p = _dg(u.astype(BF16), wm_ref[...])
    qk_ref[0] = p[:, 0:512]
    v_ref[0] = p[:, 512:1024].astype(BF16)
    o_ref[0] = p[:, 1024:1536].astype(BF16)
    nq_ref[0] = (p[:, 1536:2048] * (NA_DH ** -0.5)).astype(BF16)
    nk_ref[0] = p[:, 2048:2560].astype(BF16)
    nv_ref[0] = p[:, 2560:3072].astype(BF16)
    g_ref[0] = _dot3(u, wg_ref[...])
    gt_ref[0] = _dot3(wgt_ref[...], u, NT)


def _inproj(x, sc, sh, wm, wg, wgt):
    b, t, d = x.shape
    tm = min(t, 512)
    tok = lambda w, dt: jax.ShapeDtypeStruct((b, t, w), dt)
    blk = lambda w: pl.BlockSpec((1, tm, w), lambda i, j: (i, j, 0))
    return pl.pallas_call(
        _inproj_kernel,
        grid=(b, t // tm),
        in_specs=[
            blk(d),
            pl.BlockSpec((1, 1, d), lambda i, j: (i, 0, 0)),
            pl.BlockSpec((1, 1, d), lambda i, j: (i, 0, 0)),
            pl.BlockSpec(wm.shape, lambda i, j: (0, 0)),
            pl.BlockSpec(wg.shape, lambda i, j: (0, 0)),
            pl.BlockSpec(wgt.shape, lambda i, j: (0, 0)),
        ],
        out_specs=[blk(512), blk(512), blk(512), blk(512), blk(512), blk(512), blk(LANES),
                   pl.BlockSpec((1, ML_G_W, tm), lambda i, j: (i, 0, j))],
        out_shape=[tok(512, F32), tok(512, BF16), tok(512, BF16), tok(512, BF16), tok(512, BF16),
                   tok(512, BF16), tok(LANES, F32), jax.ShapeDtypeStruct((b, ML_G_W, t), F32)],
        compiler_params=_params(("parallel", "parallel")),
        name="inproj",
    )(x, sc, sh, wm, wg, wgt)


CONV_TILE = 128


def _conv_silu(qk_ref, cw_ref, pad_ref, t):
    width = qk_ref.shape[-1]
    zero = jnp.zeros((8, width), F32)
    pad_ref[pl.ds(0, 8), :] = zero
    pad_ref[pl.ds(8 + t, 8), :] = zero

    def copy(i, carry):
        r0 = pl.multiple_of(i * CONV_TILE, CONV_TILE)
        pad_ref[pl.ds(r0 + 8, CONV_TILE), :] = qk_ref[0, pl.ds(r0, CONV_TILE), :]
        return carry

    lax.fori_loop(0, t // CONV_TILE, copy, 0)


def _conv_tile(pad_ref, cw_ref, r0):
    n = CONV_TILE + 16
    win = pad_ref[pl.ds(r0, n), :]
    acc = None
    for j in range(CONV_W):
        k = 6 + j
        sh = pltpu.roll(win, n - k, axis=0)[:CONV_TILE]
        term = sh * cw_ref[pl.ds(j, 1), :]
        acc = term if acc is None else acc + term
    return acc * _sigmoid(acc)


def _rope_tile(y, cos, sin):
    lane = lax.broadcasted_iota(jnp.int32, (1, y.shape[1]), 1)
    first = (lane % ML_DK) < (ML_DK // 2)
    n = y.shape[1]
    partner = jnp.where(first, pltpu.roll(y, n - ML_DK // 2, axis=1), pltpu.roll(y, ML_DK // 2, axis=1))
    cos4 = jnp.concatenate([cos] * (n // LANES), axis=1)
    sin4 = jnp.concatenate([sin] * (n // LANES), axis=1)
    return y * cos4 + partner * sin4


def _mlstm_kernel(need_ctx, t_x, t_c,
                  qkx_ref, vx_ref, ox_ref, gx_ref, gtx_ref,
                  qkc_ref, vc_ref, oc_ref, gc_ref, gtc_ref,
                  cos_ref, sin_ref, cw_ref, ks_ref, bg_ref, bgt_ref, nw_ref,
                  *rest):
    if need_ctx:
        mlx_ref, mlc_ref = rest[:2]
        scratch = rest[2:]
    else:
        mlx_ref, mlc_ref = rest[0], None
        scratch = rest[1:]
    padx_ref, padc_ref, qsx_ref, qsc_ref, hfx_ref, hbx_ref, hfc_ref, hbc_ref, st_ref, m_ref = scratch

    _conv_silu(qkx_ref, cw_ref, padx_ref, t_x)
    _conv_silu(qkc_ref, cw_ref, padc_ref, t_c)
    kscale = ks_ref[...]

    def prep_x(i, carry):
        r0 = pl.multiple_of(i * CONV_TILE, CONV_TILE)
        y = _conv_tile(padx_ref, cw_ref, r0)
        y = _rope_tile(y, cos_ref[pl.ds(r0, CONV_TILE), :], sin_ref[pl.ds(r0, CONV_TILE), :])
        qsx_ref[pl.ds(r0, CONV_TILE), :] = y * kscale
        return carry

    def prep_c(i, carry):
        r0 = pl.multiple_of(i * CONV_TILE, CONV_TILE)
        qsc_ref[pl.ds(r0, CONV_TILE), :] = _conv_tile(padc_ref, cw_ref, r0) * kscale
        return carry

    lax.fori_loop(0, t_x // CONV_TILE, prep_x, 0)
    lax.fori_loop(0, t_c // CONV_TILE, prep_c, 0)

    st_ref[...] = jnp.zeros(st_ref.shape, F32)
    m_ref[...] = jnp.zeros(m_ref.shape, F32)

    L = ML_CHUNK
    row = lax.broadcasted_iota(jnp.int32, (L, L), 0)
    col = lax.broadcasted_iota(jnp.int32, (L, L), 1)
    lower = col <= row
    upper = col >= row
    ones_pad = (lax.broadcasted_iota(jnp.int32, (L, ML_DV), 1) == 0).astype(BF16)
    bg = bg_ref[...]
    bgt = bgt_ref[...]

    def chunk_pair(qs_ref, v_ref, g_ref, gt_ref, hf_ref, hb_ref, n_chunks, write_h):
        def body(i, carry):
            for d in range(2):
                c = i if d == 0 else n_chunks - 1 - i
                r0 = pl.multiple_of(c * L, L)
                mask = lower if d == 0 else upper
                mask_t = upper if d == 0 else lower
                qk = qs_ref[pl.ds(r0, L), :]
                vv = v_ref[0, pl.ds(r0, L), :]
                g = g_ref[0, pl.ds(r0, L), :][:, 0:ML_G_W] + bg
                gt = gt_ref[0, c] + bgt
                lsg = _log_sigmoid(g)
                lsgt = _log_sigmoid(gt)
                h_ref = hf_ref if d == 0 else hb_ref
                for h in range(ML_HEADS):
                    ci = 2 * d * ML_HEADS + h
                    fi = (2 * d + 1) * ML_HEADS + h
                    li_col, lf_col = g[:, ci:ci + 1], lsg[:, fi:fi + 1]
                    li_row, lf_row = gt[ci:ci + 1, :], lsgt[fi:fi + 1, :]
                    b_col = jnp.sum(jnp.where(mask, lf_row, 0.0), axis=1, keepdims=True)
                    b_row = jnp.sum(jnp.where(mask_t, lf_col, 0.0), axis=0, keepdims=True)
                    b_last = jnp.sum(lf_row, axis=1, keepdims=True)
                    sidx = d * ML_HEADS + h
                    m_in = m_ref[sidx][:, 0:1]
                    ct = st_ref[sidx]
                    q_h = qk[:, h * ML_DK:(h + 1) * ML_DK]
                    k_h = qk[:, ML_QK_W // 2 + h * ML_DK: ML_QK_W // 2 + (h + 1) * ML_DK]
                    v_ext = jnp.concatenate([vv[:, h * ML_DV:(h + 1) * ML_DV], ones_pad], axis=1)
                    if write_h:
                        dm = jnp.where(mask, b_col - b_row + li_row, NEG_BIG)
                        m_inter = b_col + m_in
                        m_j = jnp.maximum(m_inter, jnp.max(dm, axis=1, keepdims=True))
                        s = _dg(q_h.astype(BF16), k_h.astype(BF16), NT) * jnp.exp(dm - m_j)
                        inter = jnp.exp(m_inter - m_j)
                        ne = _dg(s.astype(BF16), v_ext) + inter * _dg(q_h.astype(BF16), ct.astype(BF16))
                        den = jnp.maximum(jnp.abs(ne[:, ML_DV:ML_DV + 1]), jnp.exp(-m_j))
                        h_ref[pl.ds(r0, L), h * ML_DV:(h + 1) * ML_DV] = ne[:, 0:ML_DV] / den
                    a_col = b_last - b_col + li_col
                    m_loc = jnp.max(a_col, axis=0, keepdims=True)
                    kw = (k_h * jnp.exp(a_col - m_loc)).T.astype(BF16)
                    c_loc = _dg(kw, v_ext)
                    m_new = jnp.maximum(b_last + m_in, m_loc)
                    st_ref[sidx] = jnp.exp(b_last + m_in - m_new) * ct + jnp.exp(m_loc - m_new) * c_loc
                    m_ref[sidx] = jnp.broadcast_to(m_new, (1, LANES))
            return carry

        lax.fori_loop(0, n_chunks, body, 0)

    chunk_pair(qsc_ref, vc_ref, gc_ref, gtc_ref, hfc_ref, hbc_ref, t_c // L, need_ctx)
    chunk_pair(qsx_ref, vx_ref, gx_ref, gtx_ref, hfx_ref, hbx_ref, t_x // L, True)

    nw = nw_ref[...]

    def finish(hf_ref, hb_ref, o_ref, out_ref, t):
        def body(i, carry):
            r0 = pl.multiple_of(i * CONV_TILE, CONV_TILE)
            hsum = hf_ref[pl.ds(r0, CONV_TILE), :] + hb_ref[pl.ds(r0, CONV_TILE), :]
            parts = []
            for h in range(ML_HEADS):
                hh = hsum[:, h * ML_DV:(h + 1) * ML_DV]
                mu = jnp.mean(hh, axis=1, keepdims=True)
                hc = hh - mu
                var = jnp.mean(hc * hc, axis=1, keepdims=True)
                parts.append(hc * lax.rsqrt(var + LN_EPS))
            y = jnp.concatenate(parts, axis=1) * nw
            y = y * _sigmoid(o_ref[0, pl.ds(r0, CONV_TILE), :].astype(F32))
            out_ref[0, pl.ds(r0, CONV_TILE), :] = y.astype(BF16)
            return carry

        lax.fori_loop(0, t // CONV_TILE, body, 0)

    finish(hfx_ref, hbx_ref, ox_ref, mlx_ref, t_x)
    if need_ctx:
        finish(hfc_ref, hbc_ref, oc_ref, mlc_ref, t_c)


def _mlstm(need_ctx, px, pc, cos, sin, conv_w, kscale, b_gate, ml_norm_w):
    qkx, vx, ox, gx, gtx = px
    qkc, vc, oc, gc, gtc = pc
    b, t_x, _ = qkx.shape
    t_c = qkc.shape[1]
    L = ML_CHUNK
    gtx = gtx.reshape(b, ML_G_W, t_x // L, L).transpose(0, 2, 1, 3)
    gtc = gtc.reshape(b, ML_G_W, t_c // L, L).transpose(0, 2, 1, 3)
    cw = jnp.zeros((8, ML_QK_W), F32).at[:CONV_W].set(conv_w)
    tokx = lambda w: pl.BlockSpec((1, t_x, w), lambda i: (i, 0, 0))
    tokc = lambda w: pl.BlockSpec((1, t_c, w), lambda i: (i, 0, 0))
    const = lambda a: pl.BlockSpec(a.shape, lambda i: (0,) * a.ndim)
    bg = b_gate.reshape(1, ML_G_W)
    bgt = b_gate.reshape(ML_G_W, 1)
    nw = ml_norm_w.reshape(1, ML_V_W)
    out_specs = [tokx(ML_V_W)]
    out_shape = [jax.ShapeDtypeStruct((b, t_x, ML_V_W), BF16)]
    if need_ctx:
        out_specs.append(tokc(ML_V_W))
        out_shape.append(jax.ShapeDtypeStruct((b, t_c, ML_V_W), BF16))
    outs = pl.pallas_call(
        functools.partial(_mlstm_kernel, need_ctx, t_x, t_c),
        grid=(b,),
        in_specs=[
            tokx(ML_QK_W), tokx(ML_V_W), tokx(ML_V_W), tokx(LANES),
            pl.BlockSpec((1, t_x // L, ML_G_W, L), lambda i: (i, 0, 0, 0)),
            tokc(ML_QK_W), tokc(ML_V_W), tokc(ML_V_W), tokc(LANES),
            pl.BlockSpec((1, t_c // L, ML_G_W, L), lambda i: (i, 0, 0, 0)),
            const(cos), const(sin), const(cw), const(kscale), const(bg), const(bgt), const(nw),
        ],
        out_specs=out_specs,
        out_shape=out_shape,
        scratch_shapes=[
            pltpu.VMEM((t_x + 16, ML_QK_W), F32), pltpu.VMEM((t_c + 16, ML_QK_W), F32),
            pltpu.VMEM((t_x, ML_QK_W), F32), pltpu.VMEM((t_c, ML_QK_W), F32),
            pltpu.VMEM((t_x, ML_V_W), F32), pltpu.VMEM((t_x, ML_V_W), F32),
            pltpu.VMEM((t_c, ML_V_W), F32), pltpu.VMEM((t_c, ML_V_W), F32),
            pltpu.VMEM((2 * ML_HEADS, ML_DK, 2 * ML_DV), F32),
            pltpu.VMEM((2 * ML_HEADS, 1, LANES), F32),
        ],
        compiler_params=_params(("parallel",)),
        name="mlstm",
    )(qkx, vx, ox, gx, gtx, qkc, vc, oc, gc, gtc, cos, sin, cw, kscale, bg, bgt, nw)
    return (outs[0], outs[1]) if need_ctx else (outs[0], None)


def _rope_tables(t):
    pos = np.arange(t)
    n_freq = ML_DK // 4
    inv = ROPE_BASE ** (-np.arange(n_freq, dtype=np.float32) / n_freq)
    ang = np.concatenate([(pos // GRID_W)[:, None] * inv, (pos % GRID_W)[:, None] * inv], -1).astype(np.float32)
    cos, sin = np.cos(ang), np.sin(ang)
    cos_h = np.concatenate([cos, cos], -1)
    sin_h = np.concatenate([-sin, sin], -1)
    return (jnp.asarray(np.concatenate([cos_h, cos_h], -1), F32),
            jnp.asarray(np.concatenate([sin_h, sin_h], -1), F32))


NA_GROUP = 4
NA_BAND = NA_WIN_H + NA_GROUP - 1
NA_FINISH_TILE = 512


def _head_masks():
    lane = lax.broadcasted_iota(jnp.int32, (1, 2 * NA_DH), 1)
    return lane < NA_DH, lane >= NA_DH


def _na_kernel(rows, q_ref, k_ref, v_ref, kc_ref, vc_ref, bm_ref, o_ref, sctx_ref, pctx_ref, oloc_ref, den_ref):
    masks = _head_masks()
    t = rows * GRID_W
    n_groups = rows // NA_GROUP
    gq = NA_GROUP * GRID_W
    band = NA_BAND * GRID_W
    kc = kc_ref[0]
    vc = vc_ref[0]

    for hh in range(2):
        q_all = q_ref[0]
        sctx_ref[hh] = _dg(jnp.where(masks[hh], q_all, jnp.zeros_like(q_all)), kc, NT)

    def group(g, carry):
        b0 = jnp.clip(NA_GROUP * g - NA_WIN_H // 2, 0, rows - NA_BAND)
        kind = jnp.where(g == 0, 0, jnp.where(g == n_groups - 1, 2, 1))
        qrows = pl.ds(pl.multiple_of(g * gq, gq), gq)
        krows = pl.ds(pl.multiple_of(b0 * GRID_W, GRID_W), band)
        q = q_ref[0, qrows, :]
        kb = k_ref[0, krows, :]
        vb = v_ref[0, krows, :]
        for hh in range(2):
            qm = jnp.where(masks[hh], q, jnp.zeros_like(q))
            s_loc = _dg(qm, kb, NT) + bm_ref[hh, kind]
            s_ctx = sctx_ref[hh, qrows, :]
            m = jnp.maximum(jnp.max(s_loc, axis=1, keepdims=True), jnp.max(s_ctx, axis=1, keepdims=True))
            p_loc = jnp.exp(s_loc - m)
            p_ctx = jnp.exp(s_ctx - m)
            den = jnp.sum(p_loc, axis=1, keepdims=True) + jnp.sum(p_ctx, axis=1, keepdims=True)
            pctx_ref[hh, qrows, :] = p_ctx.astype(BF16)
            den_ref[hh, qrows, :] = jnp.broadcast_to(den, (gq, 2 * NA_DH))
            oloc_ref[hh, qrows, :] = _dg(p_loc.astype(BF16), vb)
        return carry

    lax.fori_loop(0, n_groups, group, 0)

    def finish(i, carry):
        trows = pl.ds(pl.multiple_of(i * NA_FINISH_TILE, NA_FINISH_TILE), NA_FINISH_TILE)
        outs = []
        for hh in range(2):
            o = oloc_ref[hh, trows, :] + _dg(pctx_ref[hh, trows, :], vc)
            outs.append(o / den_ref[hh, trows, :])
        o_ref[0, trows, :] = jnp.where(masks[0], outs[0], outs[1]).astype(BF16)
        return carry

    lax.fori_loop(0, t // NA_FINISH_TILE, finish, 0)


def _na_latent(nq, nk, nv, nkc, nvc, bias):
    b, t, _ = nq.shape
    t_c = nkc.shape[1]
    rows = t // GRID_W
    assert rows % NA_GROUP == 0 and rows >= NA_BAND and t % NA_FINISH_TILE == 0
    tok = lambda tt: pl.BlockSpec((1, tt, 2 * NA_DH), lambda i, j: (i, 0, j))
    return pl.pallas_call(
        functools.partial(_na_kernel, rows),
        grid=(b, NA_HEADS // 2),
        in_specs=[tok(t), tok(t), tok(t), tok(t_c), tok(t_c),
                  pl.BlockSpec((2,) + bias.shape[1:], lambda i, j: (j, 0, 0, 0))],
        out_specs=tok(t),
        out_shape=jax.ShapeDtypeStruct((b, t, NA_W), BF16),
        scratch_shapes=[pltpu.VMEM((2, t, t_c), F32), pltpu.VMEM((2, t, t_c), BF16),
                        pltpu.VMEM((2, t, 2 * NA_DH), F32), pltpu.VMEM((2, t, 2 * NA_DH), F32)],
        compiler_params=_params(("parallel", "parallel")),
        name="na_latent",
    )(nq, nk, nv, nkc, nvc, bias)


def _nactx_kernel(q_ref, k_ref, v_ref, o_ref):
    masks = _head_masks()
    q, k, v = q_ref[0], k_ref[0], v_ref[0]
    outs = []
    for hh in range(2):
        qm = jnp.where(masks[hh], q, jnp.zeros_like(q))
        s = _dg(qm, k, NT)
        p = jnp.exp(s - jnp.max(s, axis=1, keepdims=True))
        outs.append(_dg(p.astype(BF16), v) / jnp.sum(p, axis=1, keepdims=True))
    o_ref[0] = jnp.where(masks[0], outs[0], outs[1]).astype(BF16)


def _na_ctx(nqc, nkc, nvc):
    b, t_c, _ = nqc.shape
    tok = pl.BlockSpec((1, t_c, 2 * NA_DH), lambda i, j: (i, 0, j))
    return pl.pallas_call(
        _nactx_kernel,
        grid=(b, NA_HEADS // 2),
        in_specs=[tok, tok, tok],
        out_specs=tok,
        out_shape=jax.ShapeDtypeStruct((b, t_c, NA_W), BF16),
        compiler_params=_params(("parallel", "parallel")),
        name="na_ctx",
    )(nqc, nkc, nvc)


def _na_bias(rpb):
    c = np.arange(GRID_W)
    win_lo = np.clip(c - NA_WIN_W // 2, 0, GRID_W - NA_WIN_W)
    ok = (c[None, :] >= win_lo[:, None]) & (c[None, :] < win_lo[:, None] + NA_WIN_W)
    span = np.clip(np.arange(2 * GRID_W - 1) - (GRID_W - 1), 1 - NA_WIN_W, NA_WIN_W - 1) + NA_WIN_W - 1
    ext = rpb[:, :, span]
    cols = jnp.stack([ext[:, :, GRID_W - 1 - q: 2 * GRID_W - 1 - q] for q in range(GRID_W)], axis=2)
    cols = jnp.where(ok[None, None], cols, NEG_BIG).transpose(0, 2, 1, 3)
    half = NA_WIN_H // 2
    kinds = [lambda rq: (rq, 0),
             lambda rq: (half + rq, rq),
             lambda rq: (NA_BAND - NA_GROUP + rq, NA_BAND - NA_WIN_H)]
    pieces = []
    for kind in kinds:
        for rq in range(NA_GROUP):
            q_rel, lo = kind(rq)
            first = lo - q_rel + NA_WIN_H - 1
            window = cols[:, :, first:first + NA_WIN_H]
            pad = ((0, 0), (0, 0), (lo, NA_BAND - NA_WIN_H - lo), (0, 0))
            pieces.append(jnp.pad(window, pad, constant_values=NEG_BIG))
    bias = jnp.stack(pieces, axis=1)
    return bias.reshape(NA_HEADS, len(kinds), NA_GROUP * GRID_W, NA_BAND * GRID_W)


def _outproj_kernel(alpha, ml_ref, na_ref, x_ref, g1_ref, w_ref, lw_ref, lb_ref, sc_ref, sh_ref,
                    x1_ref, ut_ref):
    a = jnp.concatenate([ml_ref[0], na_ref[0]], axis=1)
    y = _dg(a, w_ref[...])
    x1 = _layer_norm(alpha * x_ref[0] + g1_ref[0] * y, lw_ref[...], lb_ref[...])
    x1_ref[0] = x1
    ut_ref[...] = (x1 * (1.0 + sc_ref[0]) + sh_ref[0]).T.astype(BF16)


def _outproj(alpha, ml, na, x, g1, w_out, lw, lb, sc2, sh2):
    b, t, d = x.shape
    tm = min(t, 512)
    blk = lambda w: pl.BlockSpec((1, tm, w), lambda i, j: (i, j, 0))
    per_b = pl.BlockSpec((1, 1, d), lambda i, j: (i, 0, 0))
    const = lambda a: pl.BlockSpec(a.shape, lambda i, j: (0,) * a.ndim)
    tposed = pl.BlockSpec((d, tm), lambda i, j: (0, i * (t // tm) + j))
    return pl.pallas_call(
        functools.partial(_outproj_kernel, alpha),
        grid=(b, t // tm),
        in_specs=[blk(ML_V_W), blk(NA_W), blk(d), per_b, const(w_out), const(lw), const(lb), per_b, per_b],
        out_specs=[blk(d), tposed],
        out_shape=[jax.ShapeDtypeStruct((b, t, d), F32), jax.ShapeDtypeStruct((d, b * t), BF16)],
        compiler_params=_params(("parallel", "parallel")),
        name="outproj",
    )(ml, na, x, g1, w_out, lw, lb, sc2, sh2)


ROUTE_TB = 256
ROUTE_GROUP = 8
CAND_ROWS = 16 + 7 * 8 + 8


def _cand_tables():
    a_idx = np.zeros(CAND_ROWS, np.int64)
    b_idx = np.zeros(CAND_ROWS, np.int64)
    a_idx[0:16], b_idx[0:16] = 0, np.arange(16)
    for a in range(1, 8):
        a_idx[16 + 8 * (a - 1): 24 + 8 * (a - 1)] = a
        b_idx[16 + 8 * (a - 1): 24 + 8 * (a - 1)] = np.arange(8)
    a_idx[72:80], b_idx[72:80] = np.arange(8, 16), 0
    valid = (a_idx + 1) * (b_idx + 1) <= PEER_TOPK
    flat = (a_idx * PEER_TOPK + b_idx).astype(np.float32)
    neg = np.where(valid, 0.0, -np.inf).astype(np.float32)
    tile = lambda v: jnp.asarray(np.tile(v[:, None], (1, ROUTE_TB)), F32)
    return tile(flat), tile(neg)


def _tree(op, xs):
    while len(xs) > 1:
        xs = [op(xs[i], xs[i + 1]) for i in range(0, len(xs) - 1, 2)] + ([xs[-1]] if len(xs) % 2 else [])
    return xs[0]


def _top_rounds(s, orders=None, order_end=None):
    n = s.shape[0] // SUBLANES
    vals = [s[SUBLANES * i:SUBLANES * (i + 1)] for i in range(n)]
    ranks = [jnp.full(vals[0].shape, float(PEER_TOPK), F32)] * n
    tops = []
    for k in range(PEER_TOPK):
        m = jnp.max(_tree(jnp.maximum, vals), axis=0, keepdims=True)
        hits = [v == m for v in vals]
        if orders is not None:
            first = _tree(jnp.minimum, [jnp.where(h, o, order_end) for h, o in zip(hits, orders)])
            first = jnp.min(first, axis=0, keepdims=True)
            hits = [o == first for o in orders]
        ranks = [jnp.where(h, float(k), r) for h, r in zip(hits, ranks)]
        vals = [jnp.where(h, -jnp.inf, v) for h, v in zip(hits, vals)]
        tops.append(m)
    return jnp.concatenate(ranks, axis=0), tops


def _one_per_round(rank):
    taken = jnp.sum(jnp.where(rank < float(PEER_TOPK), 1.0, 0.0), axis=0, keepdims=True)
    return jnp.max(jnp.abs(taken - float(PEER_TOPK))) == 0.0


def _sort16_network():
    pairs = []

    def merge(lo, n, r):
        step = r * 2
        if step < n:
            merge(lo, n, step)
            merge(lo + r, n, step)
            pairs.extend((i, i + r) for i in range(lo + r, lo + n - r, step))
        else:
            pairs.append((lo, lo + r))

    def sort(lo, n):
        if n > 1:
            sort(lo, n // 2)
            sort(lo + n // 2, n // 2)
            merge(lo, n, 1)

    sort(0, PEER_TOPK)
    return pairs


def _larger_smaller(a, b):
    if b is None:
        return a, None
    if a is None:
        return b, None
    return jnp.maximum(a, b), jnp.minimum(a, b)


def _top_sorted(slabs):
    v = list(slabs) + [None] * (PEER_TOPK - len(slabs))
    for i, j in _sort16_network():
        v[i], v[j] = _larger_smaller(v[i], v[j])
    for shift in (4, 2, 1):
        moved = [None if a is None else pltpu.roll(a, shift, axis=0) for a in v]
        v = [_larger_smaller(v[k], moved[PEER_TOPK - 1 - k])[0] for k in range(PEER_TOPK)]
        for d in (8, 4, 2, 1):
            for i in range(PEER_TOPK):
                if i & d == 0:
                    v[i], v[i + d] = _larger_smaller(v[i], v[i + d])
    return v


def _rank_by_count(s, tops):
    n = s.shape[0] // SUBLANES
    ranks = []
    for i in range(n):
        v = s[SUBLANES * i:SUBLANES * (i + 1)]
        r = jnp.zeros(v.shape, F32)
        for a, t in enumerate(tops):
            r = jnp.where(t > v, float(a + 1), r)
        ranks.append(r)
    return jnp.concatenate(ranks, axis=0)


def _strictly_descending(tops):
    steps = [jnp.where(a > b, 1.0, 0.0) for a, b in zip(tops[:-1], tops[1:])]
    return jnp.min(_tree(jnp.minimum, steps)) == 1.0


def _route_kernel(ut_ref, wq_ref, kh_ref, kl_ref, flat_ref, neg_ref,
                  lim_ref, ea_ref, r2_ref, eb_ref, qt_ref, rank_ref, t_ref, e_ref, pick_ref):
    tb = ut_ref.shape[1]
    nk = PEER_NKEYS
    qt_ref[...] = _dg(wq_ref[...], ut_ref[...])
    sub_iota = lax.broadcasted_iota(jnp.int32, (SUBLANES, tb), 0).astype(F32)
    key_order = [sub_iota + float(SUBLANES * i) for i in range(nk // SUBLANES)]

    def sub_scores(i, carry):
        todo = []
        for g in range(ROUTE_GROUP):
            hh = ROUTE_GROUP * i + g
            q = qt_ref[pl.ds(pl.multiple_of(hh * nk, nk), nk), :]
            qh, ql = _split_bf16(q)
            kh = kh_ref[hh]
            s0 = _dg(kh, qh) + (_dg(kh, ql) + _dg(kl_ref[hh], qh))
            tops = _top_sorted([s0[SUBLANES * r:SUBLANES * (r + 1)] for r in range(nk // SUBLANES)])
            rank = _rank_by_count(s0, tops)
            rank_ref[hh] = rank
            t_ref[hh] = jnp.concatenate([t[0:1] for t in tops], axis=0)
            e_ref[hh] = jnp.exp(s0 - tops[0][0:1])
            todo.append((hh, s0, jnp.logical_and(_strictly_descending(tops), _one_per_round(rank))))

        @pl.when(jnp.logical_not(functools.reduce(jnp.logical_and, [ok for _, _, ok in todo])))
        def _():
            for hh, s0, _ in todo:
                rank_t, tops_t = _top_rounds(s0, key_order, float(nk))
                rank_ref[hh] = rank_t
                t_ref[hh] = jnp.concatenate(tops_t, axis=0)

        return carry

    lax.fori_loop(0, 2 * PEER_HEADS // ROUTE_GROUP, sub_scores, 0)

    def joint(i, carry):
        todo = []
        for g in range(ROUTE_GROUP):
            h = ROUTE_GROUP * i + g
            t1 = t_ref[2 * h]
            t2 = t_ref[2 * h + 1]
            blocks = [t1[0:1] + t2]
            for a in range(1, 8):
                blocks.append(t1[a:a + 1] + t2[0:8])
            blocks.append(t1[8:16] + t2[0:1])
            cand = jnp.concatenate(blocks, axis=0) + neg_ref[...]
            cut = _top_sorted([cand[SUBLANES * r:SUBLANES * (r + 1)] for r in range(CAND_ROWS // SUBLANES)])[-1]
            pick = jnp.where(cand >= cut[0:1], 0.0, float(PEER_TOPK))
            pick_ref[g] = pick
            todo.append((h, cand, t1[0:1] + t2[0:1], _one_per_round(pick)))

        @pl.when(jnp.logical_not(functools.reduce(jnp.logical_and, [ok for _, _, _, ok in todo])))
        def _():
            flat = [flat_ref[SUBLANES * r:SUBLANES * (r + 1), :] for r in range(CAND_ROWS // SUBLANES)]
            for g, (_, cand, _, _) in enumerate(todo):
                pick_ref[g] = _top_rounds(cand, flat, 1e9)[0]

        for g, (h, cand, top, _) in enumerate(todo):
            chosen = jnp.where(pick_ref[g] < float(PEER_TOPK), 1.0, 0.0)
            z = jnp.sum(chosen * jnp.exp(cand - top), axis=0, keepdims=True)
            counts = [jnp.sum(chosen[0:16], axis=0, keepdims=True)]
            for a in range(1, 8):
                counts.append(jnp.sum(chosen[16 + 8 * (a - 1): 24 + 8 * (a - 1)], axis=0, keepdims=True))
            tail = chosen[72:80]
            for a in range(8):
                counts.append(tail[a:a + 1])
            rank1 = rank_ref[2 * h].astype(BF16)
            lim = jnp.zeros((nk, tb), BF16)
            for a in range(PEER_TOPK):
                lim = jnp.where(rank1 == float(a), jnp.broadcast_to(counts[a], (nk, tb)).astype(BF16), lim)
            lim_ref[h] = lim.astype(F32)
            ea_ref[h] = e_ref[2 * h] * (1.0 / z)
            r2_ref[h] = rank_ref[2 * h + 1].astype(BF16)
            eb_ref[h] = e_ref[2 * h + 1].astype(BF16)
        return carry

    lax.fori_loop(0, PEER_HEADS // ROUTE_GROUP, joint, 0)


def _route(ut, wq, kh, kl):
    d, n = ut.shape
    tb = ROUTE_TB
    flat, neg = _cand_tables()
    nk = PEER_NKEYS
    const = lambda a: pl.BlockSpec(a.shape, lambda i: (0,) * a.ndim)
    tok = pl.BlockSpec((d, tb), lambda i: (0, i))
    out_blk = pl.BlockSpec((PEER_HEADS, nk, tb), lambda i: (0, 0, i))
    meta = lambda dt: jax.ShapeDtypeStruct((PEER_HEADS, nk, n), dt)
    return pl.pallas_call(
        _route_kernel,
        grid=(n // tb,),
        in_specs=[tok, const(wq), const(kh), const(kl), const(flat), const(neg)],
        out_specs=[out_blk, out_blk, out_blk, out_blk],
        out_shape=[meta(F32), meta(F32), meta(BF16), meta(BF16)],
        scratch_shapes=[
            pltpu.VMEM((2 * PEER_HEADS * nk, tb), F32),
            pltpu.VMEM((2 * PEER_HEADS, nk, tb), F32),
            pltpu.VMEM((2 * PEER_HEADS, PEER_TOPK, tb), F32),
            pltpu.VMEM((2 * PEER_HEADS, nk, tb), F32),
            pltpu.VMEM((ROUTE_GROUP, CAND_ROWS, tb), F32),
        ],
        compiler_params=_params(("parallel",)),
        name="peer_route",
    )(ut, wq, kh, kl, flat, neg)


PEER_TB = 512
PEER_EB = 1024


def _gelu_tanh(x):
    return 0.5 * x * (1.0 + jnp.tanh(0.7978845608028654 * (x + 0.044715 * (x * x * x))))


def _peer_kernel(alpha, n_eb, ut_ref, up_ref, dnt_ref, lim_ref, ea_ref, r2_ref, eb_ref,
                 x1_ref, g2_ref, lw_ref, lb_ref, out_ref, acc_ref, z_ref):
    e = pl.program_id(1)
    nk = PEER_NKEYS
    tb = ut_ref.shape[1]

    @pl.when(e == 0)
    def _():
        acc_ref[...] = jnp.zeros(acc_ref.shape, F32)

    ht = _dg(up_ref[...], ut_ref[...])
    zero = jnp.zeros((nk, tb), BF16)
    for j in range(PEER_EB // nk):
        act = _gelu_tanh(ht[j * nk:(j + 1) * nk].astype(BF16))
        gate = None
        for h in range(PEER_HEADS):
            lim = jnp.broadcast_to(lim_ref[h, j:j + 1, :], (nk, tb)).astype(BF16)
            ea = jnp.broadcast_to(ea_ref[h, j:j + 1, :], (nk, tb)).astype(BF16)
            term = jnp.where(r2_ref[h] < lim, eb_ref[h], zero) * ea
            gate = term if gate is None else gate + term
        z_ref[j * nk:(j + 1) * nk, :] = act * gate
    acc_ref[...] += _dg(dnt_ref[...], z_ref[...])

    @pl.when(e == n_eb - 1)
    def _():
        y = acc_ref[...].T
        out_ref[...] = _layer_norm(alpha * x1_ref[...] + g2_ref[0] * y, lw_ref[...], lb_ref[...])


def _peer_experts(alpha, ut, up, dnt, lim, ea, r2, eb, x1, g2, lw, lb, tokens_per_batch):
    n, d = x1.shape
    tb = min(PEER_TB, tokens_per_batch)
    n_exp = up.shape[0]
    n_eb = n_exp // PEER_EB
    nk = PEER_NKEYS
    keys_per_step = PEER_EB // nk
    tok = pl.BlockSpec((tb, d), lambda i, e: (i, 0))
    row_meta = pl.BlockSpec((PEER_HEADS, keys_per_step, tb), lambda i, e: (0, e, i))
    col_meta = pl.BlockSpec((PEER_HEADS, nk, tb), lambda i, e: (0, 0, i))
    const = lambda a: pl.BlockSpec(a.shape, lambda i, e: (0,) * a.ndim)
    return pl.pallas_call(
        functools.partial(_peer_kernel, alpha, n_eb),
        grid=(n // tb, n_eb),
        in_specs=[pl.BlockSpec((d, tb), lambda i, e: (0, i)),
                  pl.BlockSpec((PEER_EB, d), lambda i, e: (e, 0)),
                  pl.BlockSpec((d, PEER_EB), lambda i, e: (0, e)),
                  row_meta, row_meta, col_meta, col_meta,
                  tok,
                  pl.BlockSpec((1, 1, d), lambda i, e: (i * tb // tokens_per_batch, 0, 0)),
                  const(lw), const(lb)],
        out_specs=tok,
        out_shape=jax.ShapeDtypeStruct((n, d), F32),
        scratch_shapes=[pltpu.VMEM((d, tb), F32), pltpu.VMEM((PEER_EB, tb), BF16)],
        compiler_params=_params(("parallel", "arbitrary")),
        name="peer_experts",
    )(ut, up, dnt, lim, ea, r2, eb, x1, g2, lw, lb)


def kernel(x, c, ctx, c_ctx, ada_w, ada_b, w_in, b_gate, conv_w, ml_norm_w, na_rpb, w_out, ln1_w, ln1_b,
           peer_wq, peer_keys, peer_up, peer_down, ln2_w, ln2_b):
    depth = ada_w.shape[0]
    b, t, d = x.shape
    t_c = ctx.shape[1]
    alpha = (2.0 * depth) ** 0.25
    rows = t // GRID_W

    n_cond = b + 1
    cond = jnp.zeros((-(-n_cond // 8) * 8, d), F32).at[:b].set(c).at[b].set(c_ctx)
    mods = _modulation(cond, ada_w, ada_b)
    cos, sin = _rope_tables(t)
    kscale = jnp.concatenate([jnp.ones((1, ML_QK_W // 2), F32), jnp.full((1, ML_QK_W // 2), ML_DK ** -0.5, F32)], 1)

    g_lo = ML_QK_W + 2 * ML_V_W
    g_hi = g_lo + ML_G_W
    for l in range(depth):
        need_ctx = l < depth - 1
        mod = mods[l].reshape(-1, N_MOD, d)
        lat = [mod[:b, i].reshape(b, 1, d) for i in range(N_MOD)]
        cxm = [jnp.broadcast_to(mod[b, i].reshape(1, 1, d), (b, 1, d)) for i in range(N_MOD)]

        w = w_in[l]
        wm = jnp.concatenate([w[:, :g_lo], w[:, g_hi:]], axis=1).astype(BF16)
        wg = jnp.zeros((d, LANES), F32).at[:, :ML_G_W].set(w[:, g_lo:g_hi])
        wgt = w[:, g_lo:g_hi].T

        qkx, vx, ox, nqx, nkx, nvx, gx, gtx = _inproj(x, lat[1], lat[0], wm, wg, wgt)
        qkc, vc, oc, nqc, nkc, nvc, gc, gtc = _inproj(ctx, cxm[1], cxm[0], wm, wg, wgt)

        ml_x, ml_c = _mlstm(need_ctx, (qkx, vx, ox, gx, gtx), (qkc, vc, oc, gc, gtc),
                            cos, sin, conv_w[l], kscale, b_gate[l], ml_norm_w[l])
        na_x = _na_latent(nqx, nkx, nvx, nkc, nvc, _na_bias(na_rpb[l]))

        wo = w_out[l].astype(BF16)
        lw1, lb1 = ln1_w[l].reshape(1, d), ln1_b[l].reshape(1, d)
        lw2, lb2 = ln2_w[l].reshape(1, d), ln2_b[l].reshape(1, d)
        wq_t = peer_wq[l].T.astype(BF16)
        keys = peer_keys[l].reshape(2 * PEER_HEADS, PEER_NKEYS, -1)
        kh = keys.astype(BF16)
        kl = (keys - kh.astype(F32)).astype(BF16)
        up = peer_up[l].astype(BF16)
        dnt = peer_down[l].T.astype(BF16)

        def channel(xin, ml, na, m, tokens):
            x1, ut = _outproj(alpha, ml, na, xin, m[2], wo, lw1, lb1, m[4], m[3])
            lim, ea, r2, eb = _route(ut, wq_t, kh, kl)
            out = _peer_experts(alpha, ut, up, dnt, lim, ea, r2, eb, x1.reshape(-1, d), m[5], lw2, lb2, tokens)
            return out.reshape(xin.shape)

        x_new = channel(x, ml_x, na_x, lat, t)
        if need_ctx:
            na_c = _na_ctx(nqc, nkc, nvc)
            ctx = channel(ctx, ml_c, na_c, cxm, t_c)
        x = x_new
    return x
```

```python
import functools

import jax
import jax.numpy as jnp
import numpy as np
from jax import lax
from jax.experimental import pallas as pl
from jax.experimental.pallas import tpu as pltpu

F32 = jnp.float32
BF16 = jnp.bfloat16

GRID_W = 64
ML_HEADS = 4
ML_DV = 128
ML_DK = 64
ML_CHUNK = 256
CONV_W = 5
NA_HEADS = 8
NA_DH = 64
NA_WIN_H = 8
NA_WIN_W = 16
ROPE_BASE = 10000.0
PEER_HEADS = 8
PEER_NKEYS = 128
PEER_TOPK = 16
N_MOD = 6
ML_QK_W = 2 * ML_HEADS * ML_DK
ML_V_W = ML_HEADS * ML_DV
ML_G_W = 4 * ML_HEADS
NA_W = NA_HEADS * NA_DH
LN_EPS = 1e-5
NEG_BIG = -1e30

LANES = 128
SUBLANES = 8
VMEM_LIMIT = 56 * 1024 * 1024
TOKEN_TILE = 512

NN = (((1,), (0,)), ((), ()))
NT = (((1,), (1,)), ((), ()))


def _dg(a, b, dims=NN):
    return lax.dot_general(a, b, dims, preferred_element_type=F32)


def _split_bf16(a):
    hi = a.astype(BF16)
    lo = (a - hi.astype(F32)).astype(BF16)
    return hi, lo


def _dot3(a, b, dims=NN):
    ah, al = _split_bf16(a)
    bh, bl = _split_bf16(b)
    return _dg(ah, bh, dims) + (_dg(ah, bl, dims) + _dg(al, bh, dims))


def _sigmoid(x):
    return 1.0 / (1.0 + jnp.exp(-x))


def _log_sigmoid(x):
    return jnp.minimum(x, 0.0) - jnp.log(1.0 + jnp.exp(-jnp.abs(x)))


def _layer_norm(z, w, b):
    mu = jnp.mean(z, axis=-1, keepdims=True)
    zc = z - mu
    var = jnp.mean(zc * zc, axis=-1, keepdims=True)
    return zc * lax.rsqrt(var + LN_EPS) * w + b


def _params(sem):
    return pltpu.CompilerParams(dimension_semantics=sem, vmem_limit_bytes=VMEM_LIMIT)


def _mod_kernel(c_ref, w_ref, b_ref, o_ref):
    c = c_ref[...]
    s = c * _sigmoid(c)
    o_ref[0] = _dot3(s, w_ref[0]) + b_ref[0]


def _modulation(cond, ada_w, ada_b):
    depth, d, n = ada_w.shape
    rows = cond.shape[0]
    tn = 1024
    return pl.pallas_call(
        _mod_kernel,
        grid=(depth, n // tn),
        in_specs=[
            pl.BlockSpec((rows, d), lambda l, j: (0, 0)),
            pl.BlockSpec((1, d, tn), lambda l, j: (l, 0, j)),
            pl.BlockSpec((1, 1, tn), lambda l, j: (l, 0, j)),
        ],
        out_specs=pl.BlockSpec((1, rows, tn), lambda l, j: (l, 0, j)),
        out_shape=jax.ShapeDtypeStruct((depth, rows, n), F32),
        compiler_params=_params(("parallel", "parallel")),
        name="modulation",
    )(cond, ada_w, ada_b.reshape(depth, 1, n))


def _inproj_kernel(x_ref, sc_ref, sh_ref, wm_ref, wg_ref, wgt_ref,
                   qk_ref, v_ref, o_ref, nq_ref, nk_ref, nv_ref, g_ref, gt_ref):
    u = x_ref[0] * (1.0 + sc_ref[0]) + sh_ref[0]
    p = _dg(u.astype(BF16), wm_ref[...])
    offs = np.cumsum([0, ML_QK_W, ML_V_W, ML_V_W, NA_W, NA_W, NA_W])
    qk, v, o, nq, nk_, nv = [p[:, a:b] for a, b in zip(offs[:-1], offs[1:])]
    qk_ref[0] = qk
    v_ref[0] = v.astype(BF16)
    o_ref[0] = o.astype(BF16)
    nq_ref[0] = (nq * (NA_DH ** -0.5)).astype(BF16)
    nk_ref[0] = nk_.astype(BF16)
    nv_ref[0] = nv.astype(BF16)
    g_ref[0] = _dot3(u, wg_ref[...])
    gt_ref[0] = _dot3(wgt_ref[...], u, NT)


def _inproj(x, sc, sh, wm, wg, wgt):
    b, t, d = x.shape
    tm = min(t, TOKEN_TILE)
    tok = lambda w, dt: jax.ShapeDtypeStruct((b, t, w), dt)
    blk = lambda w: pl.BlockSpec((1, tm, w), lambda i, j: (i, j, 0))
    return pl.pallas_call(
        _inproj_kernel,
        grid=(b, t // tm),
        in_specs=[
            blk(d),
            pl.BlockSpec((1, 1, d), lambda i, j: (i, 0, 0)),
            pl.BlockSpec((1, 1, d), lambda i, j: (i, 0, 0)),
            pl.BlockSpec(wm.shape, lambda i, j: (0, 0)),
            pl.BlockSpec(wg.shape, lambda i, j: (0, 0)),
            pl.BlockSpec(wgt.shape, lambda i, j: (0, 0)),
        ],
        out_specs=[blk(ML_QK_W), blk(ML_V_W), blk(ML_V_W), blk(NA_W), blk(NA_W), blk(NA_W), blk(LANES),
                   pl.BlockSpec((1, ML_G_W, tm), lambda i, j: (i, 0, j))],
        out_shape=[tok(ML_QK_W, F32), tok(ML_V_W, BF16), tok(ML_V_W, BF16), tok(NA_W, BF16), tok(NA_W, BF16),
                   tok(NA_W, BF16), tok(LANES, F32), jax.ShapeDtypeStruct((b, ML_G_W, t), F32)],
        compiler_params=_params(("parallel", "parallel")),
        name="inproj",
    )(x, sc, sh, wm, wg, wgt)


CONV_TILE = 128


def _fill_padded(qk_ref, pad_ref, t):
    width = qk_ref.shape[-1]
    zero = jnp.zeros((8, width), F32)
    pad_ref[pl.ds(0, 8), :] = zero
    pad_ref[pl.ds(8 + t, 8), :] = zero

    def copy(i, carry):
        r0 = pl.multiple_of(i * CONV_TILE, CONV_TILE)
        pad_ref[pl.ds(r0 + 8, CONV_TILE), :] = qk_ref[0, pl.ds(r0, CONV_TILE), :]
        return carry

    lax.fori_loop(0, t // CONV_TILE, copy, 0)


def _conv_tile(pad_ref, cw_ref, r0):
    n = CONV_TILE + 16
    win = pad_ref[pl.ds(r0, n), :]
    acc = None
    for j in range(CONV_W):
        k = 6 + j
        sh = pltpu.roll(win, n - k, axis=0)[:CONV_TILE]
        term = sh * cw_ref[pl.ds(j, 1), :]
        acc = term if acc is None else acc + term
    return acc * _sigmoid(acc)


def _rope_tile(y, cos, sin):
    lane = lax.broadcasted_iota(jnp.int32, (1, y.shape[1]), 1)
    first = (lane % ML_DK) < (ML_DK // 2)
    n = y.shape[1]
    partner = jnp.where(first, pltpu.roll(y, n - ML_DK // 2, axis=1), pltpu.roll(y, ML_DK // 2, axis=1))
    cos4 = jnp.concatenate([cos] * (n // LANES), axis=1)
    sin4 = jnp.concatenate([sin] * (n // LANES), axis=1)
    return y * cos4 + partner * sin4


def _mlstm_kernel(need_ctx, t_x, t_c,
                  qkx_ref, vx_ref, ox_ref, gx_ref, gtx_ref,
                  qkc_ref, vc_ref, oc_ref, gc_ref, gtc_ref,
                  cos_ref, sin_ref, cw_ref, ks_ref, bg_ref, bgt_ref, nw_ref,
                  *rest):
    if need_ctx:
        mlx_ref, mlc_ref = rest[:2]
        scratch = rest[2:]
    else:
        mlx_ref, mlc_ref = rest[0], None
        scratch = rest[1:]
    padx_ref, padc_ref, qsx_ref, qsc_ref, hfx_ref, hbx_ref, hfc_ref, hbc_ref, st_ref, m_ref = scratch

    _fill_padded(qkx_ref, padx_ref, t_x)
    _fill_padded(qkc_ref, padc_ref, t_c)
    kscale = ks_ref[...]

    def prep_x(i, carry):
        r0 = pl.multiple_of(i * CONV_TILE, CONV_TILE)
        y = _conv_tile(padx_ref, cw_ref, r0)
        y = _rope_tile(y, cos_ref[pl.ds(r0, CONV_TILE), :], sin_ref[pl.ds(r0, CONV_TILE), :])
        qsx_ref[pl.ds(r0, CONV_TILE), :] = y * kscale
        return carry

    def prep_c(i, carry):
        r0 = pl.multiple_of(i * CONV_TILE, CONV_TILE)
        qsc_ref[pl.ds(r0, CONV_TILE), :] = _conv_tile(padc_ref, cw_ref, r0) * kscale
        return carry

    lax.fori_loop(0, t_x // CONV_TILE, prep_x, 0)
    lax.fori_loop(0, t_c // CONV_TILE, prep_c, 0)

    st_ref[...] = jnp.zeros(st_ref.shape, F32)
    m_ref[...] = jnp.zeros(m_ref.shape, F32)

    L = ML_CHUNK
    row = lax.broadcasted_iota(jnp.int32, (L, L), 0)
    col = lax.broadcasted_iota(jnp.int32, (L, L), 1)
    lower = col <= row
    upper = col >= row
    ones_pad = (lax.broadcasted_iota(jnp.int32, (L, ML_DV), 1) == 0).astype(BF16)
    bg = bg_ref[...]
    bgt = bgt_ref[...]

    def chunk_pair(qs_ref, v_ref, g_ref, gt_ref, hf_ref, hb_ref, n_chunks, write_h):
        def body(i, carry):
            for d in range(2):
                c = i if d == 0 else n_chunks - 1 - i
                r0 = pl.multiple_of(c * L, L)
                mask = lower if d == 0 else upper
                mask_t = upper if d == 0 else lower
                qk = qs_ref[pl.ds(r0, L), :]
                vv = v_ref[0, pl.ds(r0, L), :]
                g = g_ref[0, pl.ds(r0, L), :][:, 0:ML_G_W] + bg
                gt = gt_ref[0, c] + bgt
                lsg = _log_sigmoid(g)
                lsgt = _log_sigmoid(gt)
                h_ref = hf_ref if d == 0 else hb_ref
                for h in range(ML_HEADS):
                    ci = 2 * d * ML_HEADS + h
                    fi = (2 * d + 1) * ML_HEADS + h
                    li_col, lf_col = g[:, ci:ci + 1], lsg[:, fi:fi + 1]
                    li_row, lf_row = gt[ci:ci + 1, :], lsgt[fi:fi + 1, :]
                    b_col = jnp.sum(jnp.where(mask, lf_row, 0.0), axis=1, keepdims=True)
                    b_row = jnp.sum(jnp.where(mask_t, lf_col, 0.0), axis=0, keepdims=True)
                    b_last = jnp.sum(lf_row, axis=1, keepdims=True)
                    sidx = d * ML_HEADS + h
                    m_in = m_ref[sidx][:, 0:1]
                    ct = st_ref[sidx]
                    q_h = qk[:, h * ML_DK:(h + 1) * ML_DK]
                    k_h = qk[:, ML_QK_W // 2 + h * ML_DK: ML_QK_W // 2 + (h + 1) * ML_DK]
                    v_ext = jnp.concatenate([vv[:, h * ML_DV:(h + 1) * ML_DV], ones_pad], axis=1)
                    if write_h:
                        dm = jnp.where(mask, b_col - b_row + li_row, NEG_BIG)
                        m_inter = b_col + m_in
                        m_j = jnp.maximum(m_inter, jnp.max(dm, axis=1, keepdims=True))
                        s = _dg(q_h.astype(BF16), k_h.astype(BF16), NT) * jnp.exp(dm - m_j)
                        inter = jnp.exp(m_inter - m_j)
                        ne = _dg(s.astype(BF16), v_ext) + inter * _dg(q_h.astype(BF16), ct.astype(BF16))
                        den = jnp.maximum(jnp.abs(ne[:, ML_DV:ML_DV + 1]), jnp.exp(-m_j))
                        h_ref[pl.ds(r0, L), h * ML_DV:(h + 1) * ML_DV] = ne[:, 0:ML_DV] / den
                    a_col = b_last - b_col + li_col
                    m_loc = jnp.max(a_col, axis=0, keepdims=True)
                    kw = (k_h * jnp.exp(a_col - m_loc)).T.astype(BF16)
                    c_loc = _dg(kw, v_ext)
                    m_new = jnp.maximum(b_last + m_in, m_loc)
                    st_ref[sidx] = jnp.exp(b_last + m_in - m_new) * ct + jnp.exp(m_loc - m_new) * c_loc
                    m_ref[sidx] = jnp.broadcast_to(m_new, (1, LANES))
            return carry

        lax.fori_loop(0, n_chunks, body, 0)

    chunk_pair(qsc_ref, vc_ref, gc_ref, gtc_ref, hfc_ref, hbc_ref, t_c // L, need_ctx)
    chunk_pair(qsx_ref, vx_ref, gx_ref, gtx_ref, hfx_ref, hbx_ref, t_x // L, True)

    nw = nw_ref[...]

    def finish(hf_ref, hb_ref, o_ref, out_ref, t):
        def body(i, carry):
            r0 = pl.multiple_of(i * CONV_TILE, CONV_TILE)
            hsum = hf_ref[pl.ds(r0, CONV_TILE), :] + hb_ref[pl.ds(r0, CONV_TILE), :]
            parts = []
            for h in range(ML_HEADS):
                hh = hsum[:, h * ML_DV:(h + 1) * ML_DV]
                mu = jnp.mean(hh, axis=1, keepdims=True)
                hc = hh - mu
                var = jnp.mean(hc * hc, axis=1, keepdims=True)
                parts.append(hc * lax.rsqrt(var + LN_EPS))
            y = jnp.concatenate(parts, axis=1) * nw
            y = y * _sigmoid(o_ref[0, pl.ds(r0, CONV_TILE), :].astype(F32))
            out_ref[0, pl.ds(r0, CONV_TILE), :] = y.astype(BF16)
            return carry

        lax.fori_loop(0, t // CONV_TILE, body, 0)

    finish(hfx_ref, hbx_ref, ox_ref, mlx_ref, t_x)
    if need_ctx:
        finish(hfc_ref, hbc_ref, oc_ref, mlc_ref, t_c)


def _mlstm(need_ctx, px, pc, cos, sin, conv_w, kscale, b_gate, ml_norm_w):
    qkx, vx, ox, gx, gtx = px
    qkc, vc, oc, gc, gtc = pc
    b, t_x, _ = qkx.shape
    t_c = qkc.shape[1]
    L = ML_CHUNK
    gtx = gtx.reshape(b, ML_G_W, t_x // L, L).transpose(0, 2, 1, 3)
    gtc = gtc.reshape(b, ML_G_W, t_c // L, L).transpose(0, 2, 1, 3)
    cw = jnp.zeros((8, ML_QK_W), F32).at[:CONV_W].set(conv_w)
    tokx = lambda w: pl.BlockSpec((1, t_x, w), lambda i: (i, 0, 0))
    tokc = lambda w: pl.BlockSpec((1, t_c, w), lambda i: (i, 0, 0))
    const = lambda a: pl.BlockSpec(a.shape, lambda i: (0,) * a.ndim)
    bg = b_gate.reshape(1, ML_G_W)
    bgt = b_gate.reshape(ML_G_W, 1)
    nw = ml_norm_w.reshape(1, ML_V_W)
    out_specs = [tokx(ML_V_W)]
    out_shape = [jax.ShapeDtypeStruct((b, t_x, ML_V_W), BF16)]
    if need_ctx:
        out_specs.append(tokc(ML_V_W))
        out_shape.append(jax.ShapeDtypeStruct((b, t_c, ML_V_W), BF16))
    outs = pl.pallas_call(
        functools.partial(_mlstm_kernel, need_ctx, t_x, t_c),
        grid=(b,),
        in_specs=[
            tokx(ML_QK_W), tokx(ML_V_W), tokx(ML_V_W), tokx(LANES),
            pl.BlockSpec((1, t_x // L, ML_G_W, L), lambda i: (i, 0, 0, 0)),
            tokc(ML_QK_W), tokc(ML_V_W), tokc(ML_V_W), tokc(LANES),
            pl.BlockSpec((1, t_c // L, ML_G_W, L), lambda i: (i, 0, 0, 0)),
            const(cos), const(sin), const(cw), const(kscale), const(bg), const(bgt), const(nw),
        ],
        out_specs=out_specs,
        out_shape=out_shape,
        scratch_shapes=[
            pltpu.VMEM((t_x + 16, ML_QK_W), F32), pltpu.VMEM((t_c + 16, ML_QK_W), F32),
            pltpu.VMEM((t_x, ML_QK_W), F32), pltpu.VMEM((t_c, ML_QK_W), F32),
            pltpu.VMEM((t_x, ML_V_W), F32), pltpu.VMEM((t_x, ML_V_W), F32),
            pltpu.VMEM((t_c, ML_V_W), F32), pltpu.VMEM((t_c, ML_V_W), F32),
            pltpu.VMEM((2 * ML_HEADS, ML_DK, 2 * ML_DV), F32),
            pltpu.VMEM((2 * ML_HEADS, 1, LANES), F32),
        ],
        compiler_params=_params(("parallel",)),
        name="mlstm",
    )(qkx, vx, ox, gx, gtx, qkc, vc, oc, gc, gtc, cos, sin, cw, kscale, bg, bgt, nw)
    return (outs[0], outs[1]) if need_ctx else (outs[0], None)


def _rope_tables(t):
    pos = np.arange(t)
    n_freq = ML_DK // 4
    inv = ROPE_BASE ** (-np.arange(n_freq, dtype=np.float32) / n_freq)
    ang = np.concatenate([(pos // GRID_W)[:, None] * inv, (pos % GRID_W)[:, None] * inv], -1).astype(np.float32)
    cos, sin = np.cos(ang), np.sin(ang)
    cos_h = np.concatenate([cos, cos], -1)
    sin_h = np.concatenate([-sin, sin], -1)
    return (jnp.asarray(np.concatenate([cos_h, cos_h], -1), F32),
            jnp.asarray(np.concatenate([sin_h, sin_h], -1), F32))


NA_GROUP = 4
NA_BAND = NA_WIN_H + NA_GROUP - 1
NA_FINISH_TILE = 512


def _head_masks():
    lane = lax.broadcasted_iota(jnp.int32, (1, 2 * NA_DH), 1)
    return lane < NA_DH, lane >= NA_DH


def _na_kernel(rows, q_ref, k_ref, v_ref, kc_ref, vc_ref, bm_ref, o_ref, sctx_ref, pctx_ref, oloc_ref, den_ref):
    masks = _head_masks()
    t = rows * GRID_W
    n_groups = rows // NA_GROUP
    gq = NA_GROUP * GRID_W
    band = NA_BAND * GRID_W
    kc = kc_ref[0]
    vc = vc_ref[0]

    for hh in range(2):
        q_all = q_ref[0]
        sctx_ref[hh] = _dg(jnp.where(masks[hh], q_all, jnp.zeros_like(q_all)), kc, NT)

    def group(g, carry):
        b0 = jnp.clip(NA_GROUP * g - NA_WIN_H // 2, 0, rows - NA_BAND)
        kind = jnp.where(g == 0, 0, jnp.where(g == n_groups - 1, 2, 1))
        qrows = pl.ds(pl.multiple_of(g * gq, gq), gq)
        krows = pl.ds(pl.multiple_of(b0 * GRID_W, GRID_W), band)
        q = q_ref[0, qrows, :]
        kb = k_ref[0, krows, :]
        vb = v_ref[0, krows, :]
        for hh in range(2):
            qm = jnp.where(masks[hh], q, jnp.zeros_like(q))
            s_loc = _dg(qm, kb, NT) + bm_ref[hh, kind]
            s_ctx = sctx_ref[hh, qrows, :]
            m = jnp.maximum(jnp.max(s_loc, axis=1, keepdims=True), jnp.max(s_ctx, axis=1, keepdims=True))
            p_loc = jnp.exp(s_loc - m)
            p_ctx = jnp.exp(s_ctx - m)
            den = jnp.sum(p_loc, axis=1, keepdims=True) + jnp.sum(p_ctx, axis=1, keepdims=True)
            pctx_ref[hh, qrows, :] = p_ctx.astype(BF16)
            den_ref[hh, qrows, :] = jnp.broadcast_to(den, (gq, 2 * NA_DH))
            oloc_ref[hh, qrows, :] = _dg(p_loc.astype(BF16), vb)
        return carry

    lax.fori_loop(0, n_groups, group, 0)

    def finish(i, carry):
        trows = pl.ds(pl.multiple_of(i * NA_FINISH_TILE, NA_FINISH_TILE), NA_FINISH_TILE)
        outs = []
        for hh in range(2):
            o = oloc_ref[hh, trows, :] + _dg(pctx_ref[hh, trows, :], vc)
            outs.append(o / den_ref[hh, trows, :])
        o_ref[0, trows, :] = jnp.where(masks[0], outs[0], outs[1]).astype(BF16)
        return carry

    lax.fori_loop(0, t // NA_FINISH_TILE, finish, 0)


def _na_latent(nq, nk, nv, nkc, nvc, bias):
    b, t, _ = nq.shape
    t_c = nkc.shape[1]
    rows = t // GRID_W
    assert rows % NA_GROUP == 0 and rows >= NA_BAND and t % NA_FINISH_TILE == 0
    tok = lambda tt: pl.BlockSpec((1, tt, 2 * NA_DH), lambda i, j: (i, 0, j))
    return pl.pallas_call(
        functools.partial(_na_kernel, rows),
        grid=(b, NA_HEADS // 2),
        in_specs=[tok(t), tok(t), tok(t), tok(t_c), tok(t_c),
                  pl.BlockSpec((2,) + bias.shape[1:], lambda i, j: (j, 0, 0, 0))],
        out_specs=tok(t),
        out_shape=jax.ShapeDtypeStruct((b, t, NA_W), BF16),
        scratch_shapes=[pltpu.VMEM((2, t, t_c), F32), pltpu.VMEM((2, t, t_c), BF16),
                        pltpu.VMEM((2, t, 2 * NA_DH), F32), pltpu.VMEM((2, t, 2 * NA_DH), F32)],
        compiler_params=_params(("parallel", "parallel")),
        name="na_latent",
    )(nq, nk, nv, nkc, nvc, bias)


def _nactx_kernel(q_ref, k_ref, v_ref, o_ref):
    masks = _head_masks()
    q, k, v = q_ref[0], k_ref[0], v_ref[0]
    outs = []
    for hh in range(2):
        qm = jnp.where(masks[hh], q, jnp.zeros_like(q))
        s = _dg(qm, k, NT)
        p = jnp.exp(s - jnp.max(s, axis=1, keepdims=True))
        outs.append(_dg(p.astype(BF16), v) / jnp.sum(p, axis=1, keepdims=True))
    o_ref[0] = jnp.where(masks[0], outs[0], outs[1]).astype(BF16)


def _na_ctx(nqc, nkc, nvc):
    b, t_c, _ = nqc.shape
    tok = pl.BlockSpec((1, t_c, 2 * NA_DH), lambda i, j: (i, 0, j))
    return pl.pallas_call(
        _nactx_kernel,
        grid=(b, NA_HEADS // 2),
        in_specs=[tok, tok, tok],
        out_specs=tok,
        out_shape=jax.ShapeDtypeStruct((b, t_c, NA_W), BF16),
        compiler_params=_params(("parallel", "parallel")),
        name="na_ctx",
    )(nqc, nkc, nvc)


def _na_bias(rpb):
    c = np.arange(GRID_W)
    win_lo = np.clip(c - NA_WIN_W // 2, 0, GRID_W - NA_WIN_W)
    ok = (c[None, :] >= win_lo[:, None]) & (c[None, :] < win_lo[:, None] + NA_WIN_W)
    span = np.clip(np.arange(2 * GRID_W - 1) - (GRID_W - 1), 1 - NA_WIN_W, NA_WIN_W - 1) + NA_WIN_W - 1
    ext = rpb[:, :, span]
    cols = jnp.stack([ext[:, :, GRID_W - 1 - q: 2 * GRID_W - 1 - q] for q in range(GRID_W)], axis=2)
    cols = jnp.where(ok[None, None], cols, NEG_BIG).transpose(0, 2, 1, 3)
    half = NA_WIN_H // 2
    kinds = [lambda rq: (rq, 0),
             lambda rq: (half + rq, rq),
             lambda rq: (NA_BAND - NA_GROUP + rq, NA_BAND - NA_WIN_H)]
    pieces = []
    for kind in kinds:
        for rq in range(NA_GROUP):
            q_rel, lo = kind(rq)
            first = lo - q_rel + NA_WIN_H - 1
            window = cols[:, :, first:first + NA_WIN_H]
            pad = ((0, 0), (0, 0), (lo, NA_BAND - NA_WIN_H - lo), (0, 0))
            pieces.append(jnp.pad(window, pad, constant_values=NEG_BIG))
    bias = jnp.stack(pieces, axis=1)
    return bias.reshape(NA_HEADS, len(kinds), NA_GROUP * GRID_W, NA_BAND * GRID_W)


def _outproj_kernel(alpha, ml_ref, na_ref, x_ref, g1_ref, w_ref, lw_ref, lb_ref, sc_ref, sh_ref,
                    x1_ref, ut_ref):
    a = jnp.concatenate([ml_ref[0], na_ref[0]], axis=1)
    y = _dg(a, w_ref[...])
    x1 = _layer_norm(alpha * x_ref[0] + g1_ref[0] * y, lw_ref[...], lb_ref[...])
    x1_ref[0] = x1
    ut_ref[...] = (x1 * (1.0 + sc_ref[0]) + sh_ref[0]).T.astype(BF16)


def _outproj(alpha, ml, na, x, g1, w_out, lw, lb, sc2, sh2):
    b, t, d = x.shape
    tm = min(t, TOKEN_TILE)
    blk = lambda w: pl.BlockSpec((1, tm, w), lambda i, j: (i, j, 0))
    per_b = pl.BlockSpec((1, 1, d), lambda i, j: (i, 0, 0))
    const = lambda a: pl.BlockSpec(a.shape, lambda i, j: (0,) * a.ndim)
    tposed = pl.BlockSpec((d, tm), lambda i, j: (0, i * (t // tm) + j))
    return pl.pallas_call(
        functools.partial(_outproj_kernel, alpha),
        grid=(b, t // tm),
        in_specs=[blk(ML_V_W), blk(NA_W), blk(d), per_b, const(w_out), const(lw), const(lb), per_b, per_b],
        out_specs=[blk(d), tposed],
        out_shape=[jax.ShapeDtypeStruct((b, t, d), F32), jax.ShapeDtypeStruct((d, b * t), BF16)],
        compiler_params=_params(("parallel", "parallel")),
        name="outproj",
    )(ml, na, x, g1, w_out, lw, lb, sc2, sh2)


ROUTE_TB = 256
ROUTE_GROUP = 8
CAND_ROWS = 16 + 7 * 8 + 8


def _cand_tables():
    a_idx = np.zeros(CAND_ROWS, np.int64)
    b_idx = np.zeros(CAND_ROWS, np.int64)
    a_idx[0:16], b_idx[0:16] = 0, np.arange(16)
    for a in range(1, 8):
        a_idx[16 + 8 * (a - 1): 24 + 8 * (a - 1)] = a
        b_idx[16 + 8 * (a - 1): 24 + 8 * (a - 1)] = np.arange(8)
    a_idx[72:80], b_idx[72:80] = np.arange(8, 16), 0
    valid = (a_idx + 1) * (b_idx + 1) <= PEER_TOPK
    flat = (a_idx * PEER_TOPK + b_idx).astype(np.float32)
    neg = np.where(valid, 0.0, -np.inf).astype(np.float32)
    tile = lambda v: jnp.asarray(np.tile(v[:, None], (1, ROUTE_TB)), F32)
    return tile(flat), tile(neg)


def _tree(op, xs):
    while len(xs) > 1:
        xs = [op(xs[i], xs[i + 1]) for i in range(0, len(xs) - 1, 2)] + ([xs[-1]] if len(xs) % 2 else [])
    return xs[0]


def _top_rounds(s, orders, order_end):
    n = s.shape[0] // SUBLANES
    vals = [s[SUBLANES * i:SUBLANES * (i + 1)] for i in range(n)]
    ranks = [jnp.full(vals[0].shape, float(PEER_TOPK), F32)] * n
    tops = []
    for k in range(PEER_TOPK):
        m = jnp.max(_tree(jnp.maximum, vals), axis=0, keepdims=True)
        first = _tree(jnp.minimum, [jnp.where(v == m, o, order_end) for v, o in zip(vals, orders)])
        first = jnp.min(first, axis=0, keepdims=True)
        hits = [o == first for o in orders]
        ranks = [jnp.where(h, float(k), r) for h, r in zip(hits, ranks)]
        vals = [jnp.where(h, -jnp.inf, v) for h, v in zip(hits, vals)]
        tops.append(m)
    return jnp.concatenate(ranks, axis=0), tops


def _one_per_round(rank):
    taken = jnp.sum(jnp.where(rank < float(PEER_TOPK), 1.0, 0.0), axis=0, keepdims=True)
    return jnp.max(jnp.abs(taken - float(PEER_TOPK))) == 0.0


def _sort16_network():
    pairs = []

    def merge(lo, n, r):
        step = r * 2
        if step < n:
            merge(lo, n, step)
            merge(lo + r, n, step)
            pairs.extend((i, i + r) for i in range(lo + r, lo + n - r, step))
        else:
            pairs.append((lo, lo + r))

    def sort(lo, n):
        if n > 1:
            sort(lo, n // 2)
            sort(lo + n // 2, n // 2)
            merge(lo, n, 1)

    sort(0, PEER_TOPK)
    return pairs


def _larger_smaller(a, b):
    if b is None:
        return a, None
    if a is None:
        return b, None
    return jnp.maximum(a, b), jnp.minimum(a, b)


def _top_sorted(slabs):
    v = list(slabs) + [None] * (PEER_TOPK - len(slabs))
    for i, j in _sort16_network():
        v[i], v[j] = _larger_smaller(v[i], v[j])
    for shift in (4, 2, 1):
        moved = [None if a is None else pltpu.roll(a, shift, axis=0) for a in v]
        v = [_larger_smaller(v[k], moved[PEER_TOPK - 1 - k])[0] for k in range(PEER_TOPK)]
        for d in (8, 4, 2, 1):
            for i in range(PEER_TOPK):
                if i & d == 0:
                    v[i], v[i + d] = _larger_smaller(v[i], v[i + d])
    return v


def _rank_by_count(s, tops):
    n = s.shape[0] // SUBLANES
    ranks = []
    for i in range(n):
        v = s[SUBLANES * i:SUBLANES * (i + 1)]
        r = jnp.zeros(v.shape, F32)
        for a, t in enumerate(tops):
            r = jnp.where(t > v, float(a + 1), r)
        ranks.append(r)
    return jnp.concatenate(ranks, axis=0)


def _strictly_descending(tops):
    steps = [jnp.where(a > b, 1.0, 0.0) for a, b in zip(tops[:-1], tops[1:])]
    return jnp.min(_tree(jnp.minimum, steps)) == 1.0


def _route_kernel(ut_ref, wq_ref, kh_ref, kl_ref, flat_ref, neg_ref,
                  lim_ref, ea_ref, r2_ref, eb_ref, qt_ref, rank_ref, t_ref, e_ref, pick_ref):
    tb = ut_ref.shape[1]
    nk = PEER_NKEYS
    qt_ref[...] = _dg(wq_ref[...], ut_ref[...])
    sub_iota = lax.broadcasted_iota(jnp.int32, (SUBLANES, tb), 0).astype(F32)
    key_order = [sub_iota + float(SUBLANES * i) for i in range(nk // SUBLANES)]

    def sub_scores(i, carry):
        todo = []
        for g in range(ROUTE_GROUP):
            hh = ROUTE_GROUP * i + g
            q = qt_ref[pl.ds(pl.multiple_of(hh * nk, nk), nk), :]
            qh, ql = _split_bf16(q)
            kh = kh_ref[hh]
            s0 = _dg(kh, qh) + (_dg(kh, ql) + _dg(kl_ref[hh], qh))
            tops = _top_sorted([s0[SUBLANES * r:SUBLANES * (r + 1)] for r in range(nk // SUBLANES)])
            rank = _rank_by_count(s0, tops)
            rank_ref[hh] = rank
            t_ref[hh] = jnp.concatenate([t[0:1] for t in tops], axis=0)
            e_ref[hh] = jnp.exp(s0 - tops[0][0:1])
            todo.append((hh, s0, jnp.logical_and(_strictly_descending(tops), _one_per_round(rank))))

        @pl.when(jnp.logical_not(functools.reduce(jnp.logical_and, [ok for _, _, ok in todo])))
        def _():
            for hh, s0, _ in todo:
                rank_t, tops_t = _top_rounds(s0, key_order, float(nk))
                rank_ref[hh] = rank_t
                t_ref[hh] = jnp.concatenate(tops_t, axis=0)

        return carry

    lax.fori_loop(0, 2 * PEER_HEADS // ROUTE_GROUP, sub_scores, 0)

    def joint(i, carry):
        todo = []
        for g in range(ROUTE_GROUP):
            h = ROUTE_GROUP * i + g
            t1 = t_ref[2 * h]
            t2 = t_ref[2 * h + 1]
            blocks = [t1[0:1] + t2]
            for a in range(1, 8):
                blocks.append(t1[a:a + 1] + t2[0:8])
            blocks.append(t1[8:16] + t2[0:1])
            cand = jnp.concatenate(blocks, axis=0) + neg_ref[...]
            cut = _top_sorted([cand[SUBLANES * r:SUBLANES * (r + 1)] for r in range(CAND_ROWS // SUBLANES)])[-1]
            pick = jnp.where(cand >= cut[0:1], 0.0, float(PEER_TOPK))
            pick_ref[g] = pick
            todo.append((h, cand, t1[0:1] + t2[0:1], _one_per_round(pick)))

        @pl.when(jnp.logical_not(functools.reduce(jnp.logical_and, [ok for _, _, _, ok in todo])))
        def _():
            flat = [flat_ref[SUBLANES * r:SUBLANES * (r + 1), :] for r in range(CAND_ROWS // SUBLANES)]
            for g, (_, cand, _, _) in enumerate(todo):
                pick_ref[g] = _top_rounds(cand, flat, 1e9)[0]

        for g, (h, cand, top, _) in enumerate(todo):
            chosen = jnp.where(pick_ref[g] < float(PEER_TOPK), 1.0, 0.0)
            z = jnp.sum(chosen * jnp.exp(cand - top), axis=0, keepdims=True)
            counts = [jnp.sum(chosen[0:16], axis=0, keepdims=True)]
            for a in range(1, 8):
                counts.append(jnp.sum(chosen[16 + 8 * (a - 1): 24 + 8 * (a - 1)], axis=0, keepdims=True))
            tail = chosen[72:80]
            for a in range(8):
                counts.append(tail[a:a + 1])
            rank1 = rank_ref[2 * h].astype(BF16)
            lim = jnp.zeros((nk, tb), BF16)
            for a in range(PEER_TOPK):
                lim = jnp.where(rank1 == float(a), jnp.broadcast_to(counts[a], (nk, tb)).astype(BF16), lim)
            lim_ref[h] = lim.astype(F32)
            ea_ref[h] = e_ref[2 * h] * (1.0 / z)
            r2_ref[h] = rank_ref[2 * h + 1].astype(BF16)
            eb_ref[h] = e_ref[2 * h + 1].astype(BF16)
        return carry

    lax.fori_loop(0, PEER_HEADS // ROUTE_GROUP, joint, 0)


def _route(ut, wq, kh, kl):
    d, n = ut.shape
    tb = ROUTE_TB
    flat, neg = _cand_tables()
    nk = PEER_NKEYS
    const = lambda a: pl.BlockSpec(a.shape, lambda i: (0,) * a.ndim)
    tok = pl.BlockSpec((d, tb), lambda i: (0, i))
    out_blk = pl.BlockSpec((PEER_HEADS, nk, tb), lambda i: (0, 0, i))
    meta = lambda dt: jax.ShapeDtypeStruct((PEER_HEADS, nk, n), dt)
    return pl.pallas_call(
        _route_kernel,
        grid=(n // tb,),
        in_specs=[tok, const(wq), const(kh), const(kl), const(flat), const(neg)],
        out_specs=[out_blk, out_blk, out_blk, out_blk],
        out_shape=[meta(F32), meta(F32), meta(BF16), meta(BF16)],
        scratch_shapes=[
            pltpu.VMEM((2 * PEER_HEADS * nk, tb), F32),
            pltpu.VMEM((2 * PEER_HEADS, nk, tb), F32),
            pltpu.VMEM((2 * PEER_HEADS, PEER_TOPK, tb), F32),
            pltpu.VMEM((2 * PEER_HEADS, nk, tb), F32),
            pltpu.VMEM((ROUTE_GROUP, CAND_ROWS, tb), F32),
        ],
        compiler_params=_params(("parallel",)),
        name="peer_route",
    )(ut, wq, kh, kl, flat, neg)


PEER_TB = 512
PEER_EB = 1024


def _gelu_tanh(x):
    return 0.5 * x * (1.0 + jnp.tanh(0.7978845608028654 * (x + 0.044715 * (x * x * x))))


def _peer_kernel(alpha, n_eb, ut_ref, up_ref, dnt_ref, lim_ref, ea_ref, r2_ref, eb_ref,
                 x1_ref, g2_ref, lw_ref, lb_ref, out_ref, acc_ref, z_ref):
    e = pl.program_id(1)
    nk = PEER_NKEYS
    tb = ut_ref.shape[1]

    @pl.when(e == 0)
    def _():
        acc_ref[...] = jnp.zeros(acc_ref.shape, F32)

    ht = _dg(up_ref[...], ut_ref[...])
    zero = jnp.zeros((nk, tb), BF16)
    for j in range(PEER_EB // nk):
        act = _gelu_tanh(ht[j * nk:(j + 1) * nk].astype(BF16))
        gate = None
        for h in range(PEER_HEADS):
            lim = jnp.broadcast_to(lim_ref[h, j:j + 1, :], (nk, tb)).astype(BF16)
            ea = jnp.broadcast_to(ea_ref[h, j:j + 1, :], (nk, tb)).astype(BF16)
            term = jnp.where(r2_ref[h] < lim, eb_ref[h], zero) * ea
            gate = term if gate is None else gate + term
        z_ref[j * nk:(j + 1) * nk, :] = act * gate
    acc_ref[...] += _dg(dnt_ref[...], z_ref[...])

    @pl.when(e == n_eb - 1)
    def _():
        y = acc_ref[...].T
        out_ref[...] = _layer_norm(alpha * x1_ref[...] + g2_ref[0] * y, lw_ref[...], lb_ref[...])


def _peer_experts(alpha, ut, up, dnt, lim, ea, r2, eb, x1, g2, lw, lb, tokens_per_batch):
    n, d = x1.shape
    tb = min(PEER_TB, tokens_per_batch)
    n_exp = up.shape[0]
    n_eb = n_exp // PEER_EB
    nk = PEER_NKEYS
    keys_per_step = PEER_EB // nk
    tok = pl.BlockSpec((tb, d), lambda i, e: (i, 0))
    row_meta = pl.BlockSpec((PEER_HEADS, keys_per_step, tb), lambda i, e: (0, e, i))
    col_meta = pl.BlockSpec((PEER_HEADS, nk, tb), lambda i, e: (0, 0, i))
    const = lambda a: pl.BlockSpec(a.shape, lambda i, e: (0,) * a.ndim)
    return pl.pallas_call(
        functools.partial(_peer_kernel, alpha, n_eb),
        grid=(n // tb, n_eb),
        in_specs=[pl.BlockSpec((d, tb), lambda i, e: (0, i)),
                  pl.BlockSpec((PEER_EB, d), lambda i, e: (e, 0)),
                  pl.BlockSpec((d, PEER_EB), lambda i, e: (0, e)),
                  row_meta, row_meta, col_meta, col_meta,
                  tok,
                  pl.BlockSpec((1, 1, d), lambda i, e: (i * tb // tokens_per_batch, 0, 0)),
                  const(lw), const(lb)],
        out_specs=tok,
        out_shape=jax.ShapeDtypeStruct((n, d), F32),
        scratch_shapes=[pltpu.VMEM((d, tb), F32), pltpu.VMEM((PEER_EB, tb), BF16)],
        compiler_params=_params(("parallel", "arbitrary")),
        name="peer_experts",
    )(ut, up, dnt, lim, ea, r2, eb, x1, g2, lw, lb)


def kernel(x, c, ctx, c_ctx, ada_w, ada_b, w_in, b_gate, conv_w, ml_norm_w, na_rpb, w_out, ln1_w, ln1_b,
           peer_wq, peer_keys, peer_up, peer_down, ln2_w, ln2_b):
    depth = ada_w.shape[0]
    b, t, d = x.shape
    t_c = ctx.shape[1]
    alpha = (2.0 * depth) ** 0.25
    rows = t // GRID_W

    n_cond = b + 1
    cond = jnp.zeros((-(-n_cond // 8) * 8, d), F32).at[:b].set(c).at[b].set(c_ctx)
    mods = _modulation(cond, ada_w, ada_b)
    cos, sin = _rope_tables(t)
    kscale = jnp.concatenate([jnp.ones((1, ML_QK_W // 2), F32), jnp.full((1, ML_QK_W // 2), ML_DK ** -0.5, F32)], 1)

    g_lo = ML_QK_W + 2 * ML_V_W
    g_hi = g_lo + ML_G_W
    for l in range(depth):
        need_ctx = l < depth - 1
        mod = mods[l].reshape(-1, N_MOD, d)
        lat = [mod[:b, i].reshape(b, 1, d) for i in range(N_MOD)]
        cxm = [jnp.broadcast_to(mod[b, i].reshape(1, 1, d), (b, 1, d)) for i in range(N_MOD)]

        w = w_in[l]
        wm = jnp.concatenate([w[:, :g_lo], w[:, g_hi:]], axis=1).astype(BF16)
        wg = jnp.zeros((d, LANES), F32).at[:, :ML_G_W].set(w[:, g_lo:g_hi])
        wgt = w[:, g_lo:g_hi].T

        qkx, vx, ox, nqx, nkx, nvx, gx, gtx = _inproj(x, lat[1], lat[0], wm, wg, wgt)
        qkc, vc, oc, nqc, nkc, nvc, gc, gtc = _inproj(ctx, cxm[1], cxm[0], wm, wg, wgt)

        ml_x, ml_c = _mlstm(need_ctx, (qkx, vx, ox, gx, gtx), (qkc, vc, oc, gc, gtc),
                            cos, sin, conv_w[l], kscale, b_gate[l], ml_norm_w[l])
        na_x = _na_latent(nqx, nkx, nvx, nkc, nvc, _na_bias(na_rpb[l]))

        wo = w_out[l].astype(BF16)
        lw1, lb1 = ln1_w[l].reshape(1, d), ln1_b[l].reshape(1, d)
        lw2, lb2 = ln2_w[l].reshape(1, d), ln2_b[l].reshape(1, d)
        wq_t = peer_wq[l].T.astype(BF16)
        keys = peer_keys[l].reshape(2 * PEER_HEADS, PEER_NKEYS, -1)
        kh = keys.astype(BF16)
        kl = (keys - kh.astype(F32)).astype(BF16)
        up = peer_up[l].astype(BF16)
        dnt = peer_down[l].T.astype(BF16)

        def channel(xin, ml, na, m, tokens):
            x1, ut = _outproj(alpha, ml, na, xin, m[2], wo, lw1, lb1, m[4], m[3])
            lim, ea, r2, eb = _route(ut, wq_t, kh, kl)
            out = _peer_experts(alpha, ut, up, dnt, lim, ea, r2, eb, x1.reshape(-1, d), m[5], lw2, lb2, tokens)
            return out.reshape(xin.shape)

        x_new = channel(x, ml_x, na_x, lat, t)
        if need_ctx:
            na_c = _na_ctx(nqc, nkc, nvc)
            ctx = channel(ctx, ml_c, na_c, cxm, t_c)
        x = x_new
    return x
```

```python
import functools

import jax
import jax.numpy as jnp
import numpy as np
from jax import lax
from jax.experimental import pallas as pl
from jax.experimental.pallas import tpu as pltpu

F32 = jnp.float32
BF16 = jnp.bfloat16

GRID_W = 64
ML_HEADS = 4
ML_DV = 128
ML_DK = 64
ML_CHUNK = 256
CONV_W = 5
NA_HEADS = 8
NA_DH = 64
NA_WIN_H = 8
NA_WIN_W = 16
ROPE_BASE = 10000.0
PEER_HEADS = 8
PEER_NKEYS = 128
PEER_TOPK = 16
N_MOD = 6
ML_QK_W = 2 * ML_HEADS * ML_DK
ML_V_W = ML_HEADS * ML_DV
ML_G_W = 4 * ML_HEADS
NA_W = NA_HEADS * NA_DH
LN_EPS = 1e-5
NEG_BIG = -1e30

LANES = 128
SUBLANES = 8
VMEM_LIMIT = 56 * 1024 * 1024
TOKEN_TILE = 512

NN = (((1,), (0,)), ((), ()))
NT = (((1,), (1,)), ((), ()))


def _dg(a, b, dims=NN):
    return lax.dot_general(a, b, dims, preferred_element_type=F32)


def _split_bf16(a):
    hi = a.astype(BF16)
    lo = (a - hi.astype(F32)).astype(BF16)
    return hi, lo


def _dot3(a, b, dims=NN):
    ah, al = _split_bf16(a)
    bh, bl = _split_bf16(b)
    return _dg(ah, bh, dims) + (_dg(ah, bl, dims) + _dg(al, bh, dims))


def _sigmoid(x):
    return 1.0 / (1.0 + jnp.exp(-x))


def _log_sigmoid(x):
    return jnp.minimum(x, 0.0) - jnp.log(1.0 + jnp.exp(-jnp.abs(x)))


def _layer_norm(z, w, b):
    mu = jnp.mean(z, axis=-1, keepdims=True)
    zc = z - mu
    var = jnp.mean(zc * zc, axis=-1, keepdims=True)
    return zc * lax.rsqrt(var + LN_EPS) * w + b


def _params(sem):
    return pltpu.CompilerParams(dimension_semantics=sem, vmem_limit_bytes=VMEM_LIMIT)


def _mod_kernel(c_ref, w_ref, b_ref, o_ref):
    c = c_ref[...]
    s = c * _sigmoid(c)
    o_ref[0] = _dot3(s, w_ref[0]) + b_ref[0]


def _modulation(cond, ada_w, ada_b):
    depth, d, n = ada_w.shape
    rows = cond.shape[0]
    tn = 1024
    return pl.pallas_call(
        _mod_kernel,
        grid=(depth, n // tn),
        in_specs=[
            pl.BlockSpec((rows, d), lambda l, j: (0, 0)),
            pl.BlockSpec((1, d, tn), lambda l, j: (l, 0, j)),
            pl.BlockSpec((1, 1, tn), lambda l, j: (l, 0, j)),
        ],
        out_specs=pl.BlockSpec((1, rows, tn), lambda l, j: (l, 0, j)),
        out_shape=jax.ShapeDtypeStruct((depth, rows, n), F32),
        compiler_params=_params(("parallel", "parallel")),
        name="modulation",
    )(cond, ada_w, ada_b.reshape(depth, 1, n))


def _inproj_kernel(x_ref, sc_ref, sh_ref, wm_ref, wg_ref, wgt_ref,
                   qk_ref, v_ref, o_ref, nq_ref, nk_ref, nv_ref, g_ref, gt_ref):
    u = x_ref[0] * (1.0 + sc_ref[0]) + sh_ref[0]
    p = _dg(u.astype(BF16), wm_ref[...])
    offs = np.cumsum([0, ML_QK_W, ML_V_W, ML_V_W, NA_W, NA_W, NA_W])
    qk, v, o, nq, nk_, nv = [p[:, a:b] for a, b in zip(offs[:-1], offs[1:])]
    qk_ref[0] = qk
    v_ref[0] = v.astype(BF16)
    o_ref[0] = o.astype(BF16)
    nq_ref[0] = (nq * (NA_DH ** -0.5)).astype(BF16)
    nk_ref[0] = nk_.astype(BF16)
    nv_ref[0] = nv.astype(BF16)
    g_ref[0] = _dot3(u, wg_ref[...])
    gt_ref[0] = _dot3(wgt_ref[...], u, NT)


def _inproj(x, sc, sh, wm, wg, wgt):
    b, t, d = x.shape
    tm = min(t, TOKEN_TILE)
    tok = lambda w, dt: jax.ShapeDtypeStruct((b, t, w), dt)
    blk = lambda w: pl.BlockSpec((1, tm, w), lambda i, j: (i, j, 0))
    return pl.pallas_call(
        _inproj_kernel,
        grid=(b, t // tm),
        in_specs=[
            blk(d),
            pl.BlockSpec((1, 1, d), lambda i, j: (i, 0, 0)),
            pl.BlockSpec((1, 1, d), lambda i, j: (i, 0, 0)),
            pl.BlockSpec(wm.shape, lambda i, j: (0, 0)),
            pl.BlockSpec(wg.shape, lambda i, j: (0, 0)),
            pl.BlockSpec(wgt.shape, lambda i, j: (0, 0)),
        ],
        out_specs=[blk(ML_QK_W), blk(ML_V_W), blk(ML_V_W), blk(NA_W), blk(NA_W), blk(NA_W), blk(LANES),
                   pl.BlockSpec((1, ML_G_W, tm), lambda i, j: (i, 0, j))],
        out_shape=[tok(ML_QK_W, F32), tok(ML_V_W, BF16), tok(ML_V_W, BF16), tok(NA_W, BF16), tok(NA_W, BF16),
                   tok(NA_W, BF16), tok(LANES, F32), jax.ShapeDtypeStruct((b, ML_G_W, t), F32)],
        compiler_params=_params(("parallel", "parallel")),
        name="inproj",
    )(x, sc, sh, wm, wg, wgt)


CONV_TILE = 128


def _fill_padded(qk_ref, pad_ref, t):
    width = qk_ref.shape[-1]
    zero = jnp.zeros((8, width), F32)
    pad_ref[pl.ds(0, 8), :] = zero
    pad_ref[pl.ds(8 + t, 8), :] = zero

    def copy(i, carry):
        r0 = pl.multiple_of(i * CONV_TILE, CONV_TILE)
        pad_ref[pl.ds(r0 + 8, CONV_TILE), :] = qk_ref[0, pl.ds(r0, CONV_TILE), :]
        return carry

    lax.fori_loop(0, t // CONV_TILE, copy, 0)


def _conv_tile(pad_ref, cw_ref, r0):
    n = CONV_TILE + 16
    win = pad_ref[pl.ds(r0, n), :]
    acc = None
    for j in range(CONV_W):
        k = 6 + j
        sh = pltpu.roll(win, n - k, axis=0)[:CONV_TILE]
        term = sh * cw_ref[pl.ds(j, 1), :]
        acc = term if acc is None else acc + term
    return acc * _sigmoid(acc)


def _rope_tile(y, cos, sin):
    lane = lax.broadcasted_iota(jnp.int32, (1, y.shape[1]), 1)
    first = (lane % ML_DK) < (ML_DK // 2)
    n = y.shape[1]
    partner = jnp.where(first, pltpu.roll(y, n - ML_DK // 2, axis=1), pltpu.roll(y, ML_DK // 2, axis=1))
    cos4 = jnp.concatenate([cos] * (n // LANES), axis=1)
    sin4 = jnp.concatenate([sin] * (n // LANES), axis=1)
    return y * cos4 + partner * sin4


def _mlstm_kernel(need_ctx, t_x, t_c,
                  qkx_ref, vx_ref, ox_ref, gx_ref, gtx_ref,
                  qkc_ref, vc_ref, oc_ref, gc_ref, gtc_ref,
                  cos_ref, sin_ref, cw_ref, ks_ref, bg_ref, bgt_ref, nw_ref,
                  *rest):
    if need_ctx:
        mlx_ref, mlc_ref = rest[:2]
        scratch = rest[2:]
    else:
        mlx_ref, mlc_ref = rest[0], None
        scratch = rest[1:]
    padx_ref, padc_ref, qsx_ref, qsc_ref, hfx_ref, hbx_ref, hfc_ref, hbc_ref, st_ref, m_ref = scratch

    _fill_padded(qkx_ref, padx_ref, t_x)
    _fill_padded(qkc_ref, padc_ref, t_c)
    kscale = ks_ref[...]

    def prep_x(i, carry):
        r0 = pl.multiple_of(i * CONV_TILE, CONV_TILE)
        y = _conv_tile(padx_ref, cw_ref, r0)
        y = _rope_tile(y, cos_ref[pl.ds(r0, CONV_TILE), :], sin_ref[pl.ds(r0, CONV_TILE), :])
        qsx_ref[pl.ds(r0, CONV_TILE), :] = y * kscale
        return carry

    def prep_c(i, carry):
        r0 = pl.multiple_of(i * CONV_TILE, CONV_TILE)
        qsc_ref[pl.ds(r0, CONV_TILE), :] = _conv_tile(padc_ref, cw_ref, r0) * kscale
        return carry

    lax.fori_loop(0, t_x // CONV_TILE, prep_x, 0)
    lax.fori_loop(0, t_c // CONV_TILE, prep_c, 0)

    st_ref[...] = jnp.zeros(st_ref.shape, F32)
    m_ref[...] = jnp.zeros(m_ref.shape, F32)

    L = ML_CHUNK
    row = lax.broadcasted_iota(jnp.int32, (L, L), 0)
    col = lax.broadcasted_iota(jnp.int32, (L, L), 1)
    lower = col <= row
    upper = col >= row
    ones_pad = (lax.broadcasted_iota(jnp.int32, (L, ML_DV), 1) == 0).astype(BF16)
    bg = bg_ref[...]
    bgt = bgt_ref[...]

    def chunk_pair(qs_ref, v_ref, g_ref, gt_ref, hf_ref, hb_ref, n_chunks, write_h):
        def body(i, carry):
            for d in range(2):
                c = i if d == 0 else n_chunks - 1 - i
                r0 = pl.multiple_of(c * L, L)
                mask = lower if d == 0 else upper
                mask_t = upper if d == 0 else lower
                qk = qs_ref[pl.ds(r0, L), :]
                vv = v_ref[0, pl.ds(r0, L), :]
                g = g_ref[0, pl.ds(r0, L), :][:, 0:ML_G_W] + bg
                gt = gt_ref[0, c] + bgt
                lsg = _log_sigmoid(g)
                lsgt = _log_sigmoid(gt)
                h_ref = hf_ref if d == 0 else hb_ref
                for h in range(ML_HEADS):
                    ci = 2 * d * ML_HEADS + h
                    fi = (2 * d + 1) * ML_HEADS + h
                    li_col, lf_col = g[:, ci:ci + 1], lsg[:, fi:fi + 1]
                    li_row, lf_row = gt[ci:ci + 1, :], lsgt[fi:fi + 1, :]
                    b_col = jnp.sum(jnp.where(mask, lf_row, 0.0), axis=1, keepdims=True)
                    b_row = jnp.sum(jnp.where(mask_t, lf_col, 0.0), axis=0, keepdims=True)
                    b_last = jnp.sum(lf_row, axis=1, keepdims=True)
                    sidx = d * ML_HEADS + h
                    m_in = m_ref[sidx][:, 0:1]
                    ct = st_ref[sidx]
                    q_h = qk[:, h * ML_DK:(h + 1) * ML_DK]
                    k_h = qk[:, ML_QK_W // 2 + h * ML_DK: ML_QK_W // 2 + (h + 1) * ML_DK]
                    v_ext = jnp.concatenate([vv[:, h * ML_DV:(h + 1) * ML_DV], ones_pad], axis=1)
                    if write_h:
                        dm = jnp.where(mask, b_col - b_row + li_row, NEG_BIG)
                        m_inter = b_col + m_in
                        m_j = jnp.maximum(m_inter, jnp.max(dm, axis=1, keepdims=True))
                        s = _dg(q_h.astype(BF16), k_h.astype(BF16), NT) * jnp.exp(dm - m_j)
                        inter = jnp.exp(m_inter - m_j)
                        ne = _dg(s.astype(BF16), v_ext) + inter * _dg(q_h.astype(BF16), ct.astype(BF16))
                        den = jnp.maximum(jnp.abs(ne[:, ML_DV:ML_DV + 1]), jnp.exp(-m_j))
                        h_ref[pl.ds(r0, L), h * ML_DV:(h + 1) * ML_DV] = ne[:, 0:ML_DV] / den
                    a_col = b_last - b_col + li_col
                    m_loc = jnp.max(a_col, axis=0, keepdims=True)
                    kw = (k_h * jnp.exp(a_col - m_loc)).T.astype(BF16)
                    c_loc = _dg(kw, v_ext)
                    m_new = jnp.maximum(b_last + m_in, m_loc)
                    st_ref[sidx] = jnp.exp(b_last + m_in - m_new) * ct + jnp.exp(m_loc - m_new) * c_loc
                    m_ref[sidx] = jnp.broadcast_to(m_new, (1, LANES))
            return carry

        lax.fori_loop(0, n_chunks, body, 0)

    chunk_pair(qsc_ref, vc_ref, gc_ref, gtc_ref, hfc_ref, hbc_ref, t_c // L, need_ctx)
    chunk_pair(qsx_ref, vx_ref, gx_ref, gtx_ref, hfx_ref, hbx_ref, t_x // L, True)

    nw = nw_ref[...]

    def finish(hf_ref, hb_ref, o_ref, out_ref, t):
        def body(i, carry):
            r0 = pl.multiple_of(i * CONV_TILE, CONV_TILE)
            hsum = hf_ref[pl.ds(r0, CONV_TILE), :] + hb_ref[pl.ds(r0, CONV_TILE), :]
            parts = []
            for h in range(ML_HEADS):
                hh = hsum[:, h * ML_DV:(h + 1) * ML_DV]
                mu = jnp.mean(hh, axis=1, keepdims=True)
                hc = hh - mu
                var = jnp.mean(hc * hc, axis=1, keepdims=True)
                parts.append(hc * lax.rsqrt(var + LN_EPS))
            y = jnp.concatenate(parts, axis=1) * nw
            y = y * _sigmoid(o_ref[0, pl.ds(r0, CONV_TILE), :].astype(F32))
            out_ref[0, pl.ds(r0, CONV_TILE), :] = y.astype(BF16)
            return carry

        lax.fori_loop(0, t // CONV_TILE, body, 0)

    finish(hfx_ref, hbx_ref, ox_ref, mlx_ref, t_x)
    if need_ctx:
        finish(hfc_ref, hbc_ref, oc_ref, mlc_ref, t_c)


def _mlstm(need_ctx, px, pc, cos, sin, conv_w, kscale, b_gate, ml_norm_w):
    qkx, vx, ox, gx, gtx = px
    qkc, vc, oc, gc, gtc = pc
    b, t_x, _ = qkx.shape
    t_c = qkc.shape[1]
    L = ML_CHUNK
    gtx = gtx.reshape(b, ML_G_W, t_x // L, L).transpose(0, 2, 1, 3)
    gtc = gtc.reshape(b, ML_G_W, t_c // L, L).transpose(0, 2, 1, 3)
    cw = jnp.zeros((8, ML_QK_W), F32).at[:CONV_W].set(conv_w)
    tokx = lambda w: pl.BlockSpec((1, t_x, w), lambda i: (i, 0, 0))
    tokc = lambda w: pl.BlockSpec((1, t_c, w), lambda i: (i, 0, 0))
    const = lambda a: pl.BlockSpec(a.shape, lambda i: (0,) * a.ndim)
    bg = b_gate.reshape(1, ML_G_W)
    bgt = b_gate.reshape(ML_G_W, 1)
    nw = ml_norm_w.reshape(1, ML_V_W)
    out_specs = [tokx(ML_V_W)]
    out_shape = [jax.ShapeDtypeStruct((b, t_x, ML_V_W), BF16)]
    if need_ctx:
        out_specs.append(tokc(ML_V_W))
        out_shape.append(jax.ShapeDtypeStruct((b, t_c, ML_V_W), BF16))
    outs = pl.pallas_call(
        functools.partial(_mlstm_kernel, need_ctx, t_x, t_c),
        grid=(b,),
        in_specs=[
            tokx(ML_QK_W), tokx(ML_V_W), tokx(ML_V_W), tokx(LANES),
            pl.BlockSpec((1, t_x // L, ML_G_W, L), lambda i: (i, 0, 0, 0)),
            tokc(ML_QK_W), tokc(ML_V_W), tokc(ML_V_W), tokc(LANES),
            pl.BlockSpec((1, t_c // L, ML_G_W, L), lambda i: (i, 0, 0, 0)),
            const(cos), const(sin), const(cw), const(kscale), const(bg), const(bgt), const(nw),
        ],
        out_specs=out_specs,
        out_shape=out_shape,
        scratch_shapes=[
            pltpu.VMEM((t_x + 16, ML_QK_W), F32), pltpu.VMEM((t_c + 16, ML_QK_W), F32),
            pltpu.VMEM((t_x, ML_QK_W), F32), pltpu.VMEM((t_c, ML_QK_W), F32),
            pltpu.VMEM((t_x, ML_V_W), F32), pltpu.VMEM((t_x, ML_V_W), F32),
            pltpu.VMEM((t_c, ML_V_W), F32), pltpu.VMEM((t_c, ML_V_W), F32),
            pltpu.VMEM((2 * ML_HEADS, ML_DK, 2 * ML_DV), F32),
            pltpu.VMEM((2 * ML_HEADS, 1, LANES), F32),
        ],
        compiler_params=_params(("parallel",)),
        name="mlstm",
    )(qkx, vx, ox, gx, gtx, qkc, vc, oc, gc, gtc, cos, sin, cw, kscale, bg, bgt, nw)
    return (outs[0], outs[1]) if need_ctx else (outs[0], None)


def _rope_tables(t):
    pos = np.arange(t)
    n_freq = ML_DK // 4
    inv = ROPE_BASE ** (-np.arange(n_freq, dtype=np.float32) / n_freq)
    ang = np.concatenate([(pos // GRID_W)[:, None] * inv, (pos % GRID_W)[:, None] * inv], -1).astype(np.float32)
    cos, sin = np.cos(ang), np.sin(ang)
    cos_h = np.concatenate([cos, cos], -1)
    sin_h = np.concatenate([-sin, sin], -1)
    return (jnp.asarray(np.concatenate([cos_h, cos_h], -1), F32),
            jnp.asarray(np.concatenate([sin_h, sin_h], -1), F32))


NA_GROUP = 4
NA_BAND = NA_WIN_H + NA_GROUP - 1
NA_FINISH_TILE = 512


def _head_masks():
    lane = lax.broadcasted_iota(jnp.int32, (1, 2 * NA_DH), 1)
    return lane < NA_DH, lane >= NA_DH


def _na_kernel(rows, q_ref, k_ref, v_ref, kc_ref, vc_ref, bm_ref, o_ref, sctx_ref, pctx_ref, oloc_ref, den_ref):
    masks = _head_masks()
    t = rows * GRID_W
    n_groups = rows // NA_GROUP
    gq = NA_GROUP * GRID_W
    band = NA_BAND * GRID_W
    kc = kc_ref[0]
    vc = vc_ref[0]

    for hh in range(2):
        q_all = q_ref[0]
        sctx_ref[hh] = _dg(jnp.where(masks[hh], q_all, jnp.zeros_like(q_all)), kc, NT)

    def group(g, carry):
        b0 = jnp.clip(NA_GROUP * g - NA_WIN_H // 2, 0, rows - NA_BAND)
        kind = jnp.where(g == 0, 0, jnp.where(g == n_groups - 1, 2, 1))
        qrows = pl.ds(pl.multiple_of(g * gq, gq), gq)
        krows = pl.ds(pl.multiple_of(b0 * GRID_W, GRID_W), band)
        q = q_ref[0, qrows, :]
        kb = k_ref[0, krows, :]
        vb = v_ref[0, krows, :]
        for hh in range(2):
            qm = jnp.where(masks[hh], q, jnp.zeros_like(q))
            s_loc = _dg(qm, kb, NT) + bm_ref[hh, kind]
            s_ctx = sctx_ref[hh, qrows, :]
            m = jnp.maximum(jnp.max(s_loc, axis=1, keepdims=True), jnp.max(s_ctx, axis=1, keepdims=True))
            p_loc = jnp.exp(s_loc - m)
            p_ctx = jnp.exp(s_ctx - m)
            den = jnp.sum(p_loc, axis=1, keepdims=True) + jnp.sum(p_ctx, axis=1, keepdims=True)
            pctx_ref[hh, qrows, :] = p_ctx.astype(BF16)
            den_ref[hh, qrows, :] = jnp.broadcast_to(den, (gq, 2 * NA_DH))
            oloc_ref[hh, qrows, :] = _dg(p_loc.astype(BF16), vb)
        return carry

    lax.fori_loop(0, n_groups, group, 0)

    def finish(i, carry):
        trows = pl.ds(pl.multiple_of(i * NA_FINISH_TILE, NA_FINISH_TILE), NA_FINISH_TILE)
        outs = []
        for hh in range(2):
            o = oloc_ref[hh, trows, :] + _dg(pctx_ref[hh, trows, :], vc)
            outs.append(o / den_ref[hh, trows, :])
        o_ref[0, trows, :] = jnp.where(masks[0], outs[0], outs[1]).astype(BF16)
        return carry

    lax.fori_loop(0, t // NA_FINISH_TILE, finish, 0)


def _na_latent(nq, nk, nv, nkc, nvc, bias):
    b, t, _ = nq.shape
    t_c = nkc.shape[1]
    rows = t // GRID_W
    assert rows % NA_GROUP == 0 and rows >= NA_BAND and t % NA_FINISH_TILE == 0
    tok = lambda tt: pl.BlockSpec((1, tt, 2 * NA_DH), lambda i, j: (i, 0, j))
    return pl.pallas_call(
        functools.partial(_na_kernel, rows),
        grid=(b, NA_HEADS // 2),
        in_specs=[tok(t), tok(t), tok(t), tok(t_c), tok(t_c),
                  pl.BlockSpec((2,) + bias.shape[1:], lambda i, j: (j, 0, 0, 0))],
        out_specs=tok(t),
        out_shape=jax.ShapeDtypeStruct((b, t, NA_W), BF16),
        scratch_shapes=[pltpu.VMEM((2, t, t_c), F32), pltpu.VMEM((2, t, t_c), BF16),
                        pltpu.VMEM((2, t, 2 * NA_DH), F32), pltpu.VMEM((2, t, 2 * NA_DH), F32)],
        compiler_params=_params(("parallel", "parallel")),
        name="na_latent",
    )(nq, nk, nv, nkc, nvc, bias)


def _nactx_kernel(q_ref, k_ref, v_ref, o_ref):
    masks = _head_masks()
    q, k, v = q_ref[0], k_ref[0], v_ref[0]
    outs = []
    for hh in range(2):
        qm = jnp.where(masks[hh], q, jnp.zeros_like(q))
        s = _dg(qm, k, NT)
        p = jnp.exp(s - jnp.max(s, axis=1, keepdims=True))
        outs.append(_dg(p.astype(BF16), v) / jnp.sum(p, axis=1, keepdims=True))
    o_ref[0] = jnp.where(masks[0], outs[0], outs[1]).astype(BF16)


def _na_ctx(nqc, nkc, nvc):
    b, t_c, _ = nqc.shape
    tok = pl.BlockSpec((1, t_c, 2 * NA_DH), lambda i, j: (i, 0, j))
    return pl.pallas_call(
        _nactx_kernel,
        grid=(b, NA_HEADS // 2),
        in_specs=[tok, tok, tok],
        out_specs=tok,
        out_shape=jax.ShapeDtypeStruct((b, t_c, NA_W), BF16),
        compiler_params=_params(("parallel", "parallel")),
        name="na_ctx",
    )(nqc, nkc, nvc)


def _na_bias(rpb):
    c = np.arange(GRID_W)
    win_lo = np.clip(c - NA_WIN_W // 2, 0, GRID_W - NA_WIN_W)
    ok = (c[None, :] >= win_lo[:, None]) & (c[None, :] < win_lo[:, None] + NA_WIN_W)
    span = np.clip(np.arange(2 * GRID_W - 1) - (GRID_W - 1), 1 - NA_WIN_W, NA_WIN_W - 1) + NA_WIN_W - 1
    ext = rpb[:, :, span]
    cols = jnp.stack([ext[:, :, GRID_W - 1 - q: 2 * GRID_W - 1 - q] for q in range(GRID_W)], axis=2)
    cols = jnp.where(ok[None, None], cols, NEG_BIG).transpose(0, 2, 1, 3)
    half = NA_WIN_H // 2
    kinds = [lambda rq: (rq, 0),
             lambda rq: (half + rq, rq),
             lambda rq: (NA_BAND - NA_GROUP + rq, NA_BAND - NA_WIN_H)]
    pieces = []
    for kind in kinds:
        for rq in range(NA_GROUP):
            q_rel, lo = kind(rq)
            first = lo - q_rel + NA_WIN_H - 1
            window = cols[:, :, first:first + NA_WIN_H]
            pad = ((0, 0), (0, 0), (lo, NA_BAND - NA_WIN_H - lo), (0, 0))
            pieces.append(jnp.pad(window, pad, constant_values=NEG_BIG))
    bias = jnp.stack(pieces, axis=1)
    return bias.reshape(NA_HEADS, len(kinds), NA_GROUP * GRID_W, NA_BAND * GRID_W)


def _outproj_kernel(alpha, ml_ref, na_ref, x_ref, g1_ref, w_ref, lw_ref, lb_ref, sc_ref, sh_ref,
                    x1_ref, ut_ref):
    a = jnp.concatenate([ml_ref[0], na_ref[0]], axis=1)
    y = _dg(a, w_ref[...])
    x1 = _layer_norm(alpha * x_ref[0] + g1_ref[0] * y, lw_ref[...], lb_ref[...])
    x1_ref[0] = x1
    ut_ref[...] = (x1 * (1.0 + sc_ref[0]) + sh_ref[0]).T.astype(BF16)


def _outproj(alpha, ml, na, x, g1, w_out, lw, lb, sc2, sh2):
    b, t, d = x.shape
    tm = min(t, TOKEN_TILE)
    blk = lambda w: pl.BlockSpec((1, tm, w), lambda i, j: (i, j, 0))
    per_b = pl.BlockSpec((1, 1, d), lambda i, j: (i, 0, 0))
    const = lambda a: pl.BlockSpec(a.shape, lambda i, j: (0,) * a.ndim)
    tposed = pl.BlockSpec((d, tm), lambda i, j: (0, i * (t // tm) + j))
    return pl.pallas_call(
        functools.partial(_outproj_kernel, alpha),
        grid=(b, t // tm),
        in_specs=[blk(ML_V_W), blk(NA_W), blk(d), per_b, const(w_out), const(lw), const(lb), per_b, per_b],
        out_specs=[blk(d), tposed],
        out_shape=[jax.ShapeDtypeStruct((b, t, d), F32), jax.ShapeDtypeStruct((d, b * t), BF16)],
        compiler_params=_params(("parallel", "parallel")),
        name="outproj",
    )(ml, na, x, g1, w_out, lw, lb, sc2, sh2)


ROUTE_TB = 256
ROUTE_GROUP = 8
CAND_ROWS = 16 + 7 * 8 + 8


def _cand_tables():
    a_idx = np.zeros(CAND_ROWS, np.int64)
    b_idx = np.zeros(CAND_ROWS, np.int64)
    a_idx[0:16], b_idx[0:16] = 0, np.arange(16)
    for a in range(1, 8):
        a_idx[16 + 8 * (a - 1): 24 + 8 * (a - 1)] = a
        b_idx[16 + 8 * (a - 1): 24 + 8 * (a - 1)] = np.arange(8)
    a_idx[72:80], b_idx[72:80] = np.arange(8, 16), 0
    valid = (a_idx + 1) * (b_idx + 1) <= PEER_TOPK
    flat = (a_idx * PEER_TOPK + b_idx).astype(np.float32)
    neg = np.where(valid, 0.0, -np.inf).astype(np.float32)
    tile = lambda v: jnp.asarray(np.tile(v[:, None], (1, ROUTE_TB)), F32)
    return tile(flat), tile(neg)


def _tree(op, xs):
    while len(xs) > 1:
        xs = [op(xs[i], xs[i + 1]) for i in range(0, len(xs) - 1, 2)] + ([xs[-1]] if len(xs) % 2 else [])
    return xs[0]


def _top_rounds(s, orders, order_end):
    n = s.shape[0] // SUBLANES
    vals = [s[SUBLANES * i:SUBLANES * (i + 1)] for i in range(n)]
    ranks = [jnp.full(vals[0].shape, float(PEER_TOPK), F32)] * n
    tops = []
    for k in range(PEER_TOPK):
        m = jnp.max(_tree(jnp.maximum, vals), axis=0, keepdims=True)
        first = _tree(jnp.minimum, [jnp.where(v == m, o, order_end) for v, o in zip(vals, orders)])
        first = jnp.min(first, axis=0, keepdims=True)
        hits = [o == first for o in orders]
        ranks = [jnp.where(h, float(k), r) for h, r in zip(hits, ranks)]
        vals = [jnp.where(h, -jnp.inf, v) for h, v in zip(hits, vals)]
        tops.append(m)
    return jnp.concatenate(ranks, axis=0), tops


def _one_per_round(rank):
    taken = jnp.sum(jnp.where(rank < float(PEER_TOPK), 1.0, 0.0), axis=0, keepdims=True)
    return jnp.max(jnp.abs(taken - float(PEER_TOPK))) == 0.0


def _sort16_network():
    pairs = []

    def merge(lo, n, r):
        step = r * 2
        if step < n:
            merge(lo, n, step)
            merge(lo + r, n, step)
            pairs.extend((i, i + r) for i in range(lo + r, lo + n - r, step))
        else:
            pairs.append((lo, lo + r))

    def sort(lo, n):
        if n > 1:
            sort(lo, n // 2)
            sort(lo + n // 2, n // 2)
            merge(lo, n, 1)

    sort(0, PEER_TOPK)
    return pairs


def _larger_smaller(a, b):
    if b is None:
        return a, None
    if a is None:
        return b, None
    return jnp.maximum(a, b), jnp.minimum(a, b)


def _top_sorted(slabs):
    v = list(slabs) + [None] * (PEER_TOPK - len(slabs))
    for i, j in _sort16_network():
        v[i], v[j] = _larger_smaller(v[i], v[j])
    for shift in (4, 2, 1):
        moved = [None if a is None else pltpu.roll(a, shift, axis=0) for a in v]
        v = [_larger_smaller(v[k], moved[PEER_TOPK - 1 - k])[0] for k in range(PEER_TOPK)]
        for d in (8, 4, 2, 1):
            for i in range(PEER_TOPK):
                if i & d == 0:
                    v[i], v[i + d] = _larger_smaller(v[i], v[i + d])
    return v


def _rank_by_count(s, tops):
    n = s.shape[0] // SUBLANES
    ranks = []
    for i in range(n):
        v = s[SUBLANES * i:SUBLANES * (i + 1)]
        r = jnp.zeros(v.shape, F32)
        for a, t in enumerate(tops):
            r = jnp.where(t > v, float(a + 1), r)
        ranks.append(r)
    return jnp.concatenate(ranks, axis=0)


def _strictly_descending(tops):
    steps = [jnp.where(a > b, 1.0, 0.0) for a, b in zip(tops[:-1], tops[1:])]
    return jnp.min(_tree(jnp.minimum, steps)) == 1.0


def _route_kernel(ut_ref, wq_ref, kh_ref, kl_ref, flat_ref, neg_ref,
                  lim_ref, ea_ref, r2_ref, eb_ref, qt_ref, rank_ref, t_ref, e_ref, pick_ref):
    tb = ut_ref.shape[1]
    nk = PEER_NKEYS
    qt_ref[...] = _dg(wq_ref[...], ut_ref[...])
    sub_iota = lax.broadcasted_iota(jnp.int32, (SUBLANES, tb), 0).astype(F32)
    key_order = [sub_iota + float(SUBLANES * i) for i in range(nk // SUBLANES)]

    def sub_scores(i, carry):
        todo = []
        for g in range(ROUTE_GROUP):
            hh = ROUTE_GROUP * i + g
            q = qt_ref[pl.ds(pl.multiple_of(hh * nk, nk), nk), :]
            qh, ql = _split_bf16(q)
            kh = kh_ref[hh]
            s0 = _dg(kh, qh) + (_dg(kh, ql) + _dg(kl_ref[hh], qh))
            tops = _top_sorted([s0[SUBLANES * r:SUBLANES * (r + 1)] for r in range(nk // SUBLANES)])
            rank = _rank_by_count(s0, tops)
            rank_ref[hh] = rank
            t_ref[hh] = jnp.concatenate([t[0:1] for t in tops], axis=0)
            e_ref[hh] = jnp.exp(s0 - tops[0][0:1])
            todo.append((hh, s0, jnp.logical_and(_strictly_descending(tops), _one_per_round(rank))))

        @pl.when(jnp.logical_not(functools.reduce(jnp.logical_and, [ok for _, _, ok in todo])))
        def _():
            for hh, s0, _ in todo:
                rank_t, tops_t = _top_rounds(s0, key_order, float(nk))
                rank_ref[hh] = rank_t
                t_ref[hh] = jnp.concatenate(tops_t, axis=0)

        return carry

    lax.fori_loop(0, 2 * PEER_HEADS // ROUTE_GROUP, sub_scores, 0)

    def joint(i, carry):
        todo = []
        for g in range(ROUTE_GROUP):
            h = ROUTE_GROUP * i + g
            t1 = t_ref[2 * h]
            t2 = t_ref[2 * h + 1]
            blocks = [t1[0:1] + t2]
            for a in range(1, 8):
                blocks.append(t1[a:a + 1] + t2[0:8])
            blocks.append(t1[8:16] + t2[0:1])
            cand = jnp.concatenate(blocks, axis=0) + neg_ref[...]
            cut = _top_sorted([cand[SUBLANES * r:SUBLANES * (r + 1)] for r in range(CAND_ROWS // SUBLANES)])[-1]
            pick = jnp.where(cand >= cut[0:1], 0.0, float(PEER_TOPK))
            pick_ref[g] = pick
            todo.append((h, cand, t1[0:1] + t2[0:1], _one_per_round(pick)))

        @pl.when(jnp.logical_not(functools.reduce(jnp.logical_and, [ok for _, _, _, ok in todo])))
        def _():
            flat = [flat_ref[SUBLANES * r:SUBLANES * (r + 1), :] for r in range(CAND_ROWS // SUBLANES)]
            for g, (_, cand, _, _) in enumerate(todo):
                pick_ref[g] = _top_rounds(cand, flat, 1e9)[0]

        for g, (h, cand, top, _) in enumerate(todo):
            chosen = jnp.where(pick_ref[g] < float(PEER_TOPK), 1.0, 0.0)
            z = jnp.sum(chosen * jnp.exp(cand - top), axis=0, keepdims=True)
            counts = [jnp.sum(chosen[0:16], axis=0, keepdims=True)]
            for a in range(1, 8):
                counts.append(jnp.sum(chosen[16 + 8 * (a - 1): 24 + 8 * (a - 1)], axis=0, keepdims=True))
            tail = chosen[72:80]
            for a in range(8):
                counts.append(tail[a:a + 1])
            rank1 = rank_ref[2 * h].astype(BF16)
            lim = jnp.zeros((nk, tb), BF16)
            for a in range(PEER_TOPK):
                lim = jnp.where(rank1 == float(a), jnp.broadcast_to(counts[a], (nk, tb)).astype(BF16), lim)
            lim = lim.astype(F32)
            ea = e_ref[2 * h] * (1.0 / z)
            for lt in range(tb // LANES):
                lim_ref[h, lt] = lim[:, lt * LANES:(lt + 1) * LANES]
                ea_ref[h, lt] = ea[:, lt * LANES:(lt + 1) * LANES]
            r2_ref[h] = rank_ref[2 * h + 1].astype(BF16)
            eb_ref[h] = e_ref[2 * h + 1].astype(BF16)
        return carry

    lax.fori_loop(0, PEER_HEADS // ROUTE_GROUP, joint, 0)


def _route(ut, wq, kh, kl):
    d, n = ut.shape
    tb = ROUTE_TB
    flat, neg = _cand_tables()
    nk = PEER_NKEYS
    const = lambda a: pl.BlockSpec(a.shape, lambda i: (0,) * a.ndim)
    tok = pl.BlockSpec((d, tb), lambda i: (0, i))
    out_blk = pl.BlockSpec((PEER_HEADS, nk, tb), lambda i: (0, 0, i))
    tiled_blk = pl.BlockSpec((PEER_HEADS, tb // LANES, nk, LANES), lambda i: (0, i, 0, 0))
    tiled = jax.ShapeDtypeStruct((PEER_HEADS, n // LANES, nk, LANES), F32)
    meta = lambda dt: jax.ShapeDtypeStruct((PEER_HEADS, nk, n), dt)
    return pl.pallas_call(
        _route_kernel,
        grid=(n // tb,),
        in_specs=[tok, const(wq), const(kh), const(kl), const(flat), const(neg)],
        out_specs=[tiled_blk, tiled_blk, out_blk, out_blk],
        out_shape=[tiled, tiled, meta(BF16), meta(BF16)],
        scratch_shapes=[
            pltpu.VMEM((2 * PEER_HEADS * nk, tb), F32),
            pltpu.VMEM((2 * PEER_HEADS, nk, tb), F32),
            pltpu.VMEM((2 * PEER_HEADS, PEER_TOPK, tb), F32),
            pltpu.VMEM((2 * PEER_HEADS, nk, tb), F32),
            pltpu.VMEM((ROUTE_GROUP, CAND_ROWS, tb), F32),
        ],
        compiler_params=_params(("parallel",)),
        name="peer_route",
    )(ut, wq, kh, kl, flat, neg)


PEER_TB = 512
PEER_EB = 1024


def _gelu_tanh(x):
    return 0.5 * x * (1.0 + jnp.tanh(0.7978845608028654 * (x + 0.044715 * (x * x * x))))


def _key_rows(ref, h, j, nk, tb):
    reps = 2 * SUBLANES
    tiles = [jnp.tile(jnp.broadcast_to(ref[h, lt, j:j + 1, :], (reps, LANES)).astype(BF16), (nk // reps, 1))
             for lt in range(tb // LANES)]
    return jnp.concatenate(tiles, axis=1)


def _peer_kernel(alpha, n_eb, ut_ref, up_ref, dnt_ref, lim_ref, ea_ref, r2_ref, eb_ref,
                 x1_ref, g2_ref, lw_ref, lb_ref, out_ref, acc_ref, z_ref):
    e = pl.program_id(1)
    nk = PEER_NKEYS
    tb = ut_ref.shape[1]

    @pl.when(e == 0)
    def _():
        acc_ref[...] = jnp.zeros(acc_ref.shape, F32)

    ht = _dg(up_ref[...], ut_ref[...])
    zero = jnp.zeros((nk, tb), BF16)
    for j in range(PEER_EB // nk):
        act = _gelu_tanh(ht[j * nk:(j + 1) * nk].astype(BF16))
        gate = None
        for h in range(PEER_HEADS):
            lim, ea = [_key_rows(ref, h, j, nk, tb) for ref in (lim_ref, ea_ref)]
            term = jnp.where(r2_ref[h] < lim, eb_ref[h], zero) * ea
            gate = term if gate is None else gate + term
        z_ref[j * nk:(j + 1) * nk, :] = act * gate
    acc_ref[...] += _dg(dnt_ref[...], z_ref[...])

    @pl.when(e == n_eb - 1)
    def _():
        y = acc_ref[...].T
        out_ref[...] = _layer_norm(alpha * x1_ref[...] + g2_ref[0] * y, lw_ref[...], lb_ref[...])


def _peer_experts(alpha, ut, up, dnt, lim, ea, r2, eb, x1, g2, lw, lb, tokens_per_batch):
    n, d = x1.shape
    tb = min(PEER_TB, tokens_per_batch)
    n_exp = up.shape[0]
    n_eb = n_exp // PEER_EB
    nk = PEER_NKEYS
    keys_per_step = PEER_EB // nk
    tok = pl.BlockSpec((tb, d), lambda i, e: (i, 0))
    row_meta = pl.BlockSpec((PEER_HEADS, tb // LANES, keys_per_step, LANES), lambda i, e: (0, i, e, 0))
    col_meta = pl.BlockSpec((PEER_HEADS, nk, tb), lambda i, e: (0, 0, i))
    const = lambda a: pl.BlockSpec(a.shape, lambda i, e: (0,) * a.ndim)
    return pl.pallas_call(
        functools.partial(_peer_kernel, alpha, n_eb),
        grid=(n // tb, n_eb),
        in_specs=[pl.BlockSpec((d, tb), lambda i, e: (0, i)),
                  pl.BlockSpec((PEER_EB, d), lambda i, e: (e, 0)),
                  pl.BlockSpec((d, PEER_EB), lambda i, e: (0, e)),
                  row_meta, row_meta, col_meta, col_meta,
                  tok,
                  pl.BlockSpec((1, 1, d), lambda i, e: (i * tb // tokens_per_batch, 0, 0)),
                  const(lw), const(lb)],
        out_specs=tok,
        out_shape=jax.ShapeDtypeStruct((n, d), F32),
        scratch_shapes=[pltpu.VMEM((d, tb), F32), pltpu.VMEM((PEER_EB, tb), BF16)],
        compiler_params=_params(("parallel", "arbitrary")),
        name="peer_experts",
    )(ut, up, dnt, lim, ea, r2, eb, x1, g2, lw, lb)


def kernel(x, c, ctx, c_ctx, ada_w, ada_b, w_in, b_gate, conv_w, ml_norm_w, na_rpb, w_out, ln1_w, ln1_b,
           peer_wq, peer_keys, peer_up, peer_down, ln2_w, ln2_b):
    depth = ada_w.shape[0]
    b, t, d = x.shape
    t_c = ctx.shape[1]
    alpha = (2.0 * depth) ** 0.25
    rows = t // GRID_W

    n_cond = b + 1
    cond = jnp.zeros((-(-n_cond // 8) * 8, d), F32).at[:b].set(c).at[b].set(c_ctx)
    mods = _modulation(cond, ada_w, ada_b)
    cos, sin = _rope_tables(t)
    kscale = jnp.concatenate([jnp.ones((1, ML_QK_W // 2), F32), jnp.full((1, ML_QK_W // 2), ML_DK ** -0.5, F32)], 1)

    g_lo = ML_QK_W + 2 * ML_V_W
    g_hi = g_lo + ML_G_W
    for l in range(depth):
        need_ctx = l < depth - 1
        mod = mods[l].reshape(-1, N_MOD, d)
        lat = [mod[:b, i].reshape(b, 1, d) for i in range(N_MOD)]
        cxm = [jnp.broadcast_to(mod[b, i].reshape(1, 1, d), (b, 1, d)) for i in range(N_MOD)]

        w = w_in[l]
        wm = jnp.concatenate([w[:, :g_lo], w[:, g_hi:]], axis=1).astype(BF16)
        wg = jnp.zeros((d, LANES), F32).at[:, :ML_G_W].set(w[:, g_lo:g_hi])
        wgt = w[:, g_lo:g_hi].T

        qkx, vx, ox, nqx, nkx, nvx, gx, gtx = _inproj(x, lat[1], lat[0], wm, wg, wgt)
        qkc, vc, oc, nqc, nkc, nvc, gc, gtc = _inproj(ctx, cxm[1], cxm[0], wm, wg, wgt)

        ml_x, ml_c = _mlstm(need_ctx, (qkx, vx, ox, gx, gtx), (qkc, vc, oc, gc, gtc),
                            cos, sin, conv_w[l], kscale, b_gate[l], ml_norm_w[l])
        na_x = _na_latent(nqx, nkx, nvx, nkc, nvc, _na_bias(na_rpb[l]))

        wo = w_out[l].astype(BF16)
        lw1, lb1 = ln1_w[l].reshape(1, d), ln1_b[l].reshape(1, d)
        lw2, lb2 = ln2_w[l].reshape(1, d), ln2_b[l].reshape(1, d)
        wq_t = peer_wq[l].T.astype(BF16)
        keys = peer_keys[l].reshape(2 * PEER_HEADS, PEER_NKEYS, -1)
        kh = keys.astype(BF16)
        kl = (keys - kh.astype(F32)).astype(BF16)
        up = peer_up[l].astype(BF16)
        dnt = peer_down[l].T.astype(BF16)

        def channel(xin, ml, na, m, tokens):
            x1, ut = _outproj(alpha, ml, na, xin, m[2], wo, lw1, lb1, m[4], m[3])
            lim, ea, r2, eb = _route(ut, wq_t, kh, kl)
            out = _peer_experts(alpha, ut, up, dnt, lim, ea, r2, eb, x1.reshape(-1, d), m[5], lw2, lb2, tokens)
            return out.reshape(xin.shape)

        x_new = channel(x, ml_x, na_x, lat, t)
        if need_ctx:
            na_c = _na_ctx(nqc, nkc, nvc)
            ctx = channel(ctx, ml_c, na_c, cxm, t_c)
        x = x_new
    return x
```

```python
import functools

import jax
import jax.numpy as jnp
import numpy as np
from jax import lax
from jax.experimental import pallas as pl
from jax.experimental.pallas import tpu as pltpu

F32 = jnp.float32
BF16 = jnp.bfloat16

GRID_W = 64
ML_HEADS = 4
ML_DV = 128
ML_DK = 64
ML_CHUNK = 256
CONV_W = 5
NA_HEADS = 8
NA_DH = 64
NA_WIN_H = 8
NA_WIN_W = 16
ROPE_BASE = 10000.0
PEER_HEADS = 8
PEER_NKEYS = 128
PEER_TOPK = 16
N_MOD = 6
ML_QK_W = 2 * ML_HEADS * ML_DK
ML_V_W = ML_HEADS * ML_DV
ML_G_W = 4 * ML_HEADS
NA_W = NA_HEADS * NA_DH
LN_EPS = 1e-5
NEG_BIG = -1e30

LANES = 128
SUBLANES = 8
VMEM_LIMIT = 56 * 1024 * 1024
TOKEN_TILE = 512

NN = (((1,), (0,)), ((), ()))
NT = (((1,), (1,)), ((), ()))


def _dg(a, b, dims=NN):
    return lax.dot_general(a, b, dims, preferred_element_type=F32)


def _split_bf16(a):
    hi = a.astype(BF16)
    lo = (a - hi.astype(F32)).astype(BF16)
    return hi, lo


def _dot3(a, b, dims=NN):
    ah, al = _split_bf16(a)
    bh, bl = _split_bf16(b)
    return _dg(ah, bh, dims) + (_dg(ah, bl, dims) + _dg(al, bh, dims))


def _sigmoid(x):
    return 1.0 / (1.0 + jnp.exp(-x))


def _log_sigmoid(x):
    return jnp.minimum(x, 0.0) - jnp.log(1.0 + jnp.exp(-jnp.abs(x)))


def _layer_norm(z, w, b):
    mu = jnp.mean(z, axis=-1, keepdims=True)
    zc = z - mu
    var = jnp.mean(zc * zc, axis=-1, keepdims=True)
    return zc * lax.rsqrt(var + LN_EPS) * w + b


def _params(sem):
    return pltpu.CompilerParams(dimension_semantics=sem, vmem_limit_bytes=VMEM_LIMIT)


def _mod_kernel(c_ref, w_ref, b_ref, o_ref):
    c = c_ref[...]
    s = c * _sigmoid(c)
    o_ref[0] = _dot3(s, w_ref[0]) + b_ref[0]


def _modulation(cond, ada_w, ada_b):
    depth, d, n = ada_w.shape
    rows = cond.shape[0]
    tn = 1024
    return pl.pallas_call(
        _mod_kernel,
        grid=(depth, n // tn),
        in_specs=[
            pl.BlockSpec((rows, d), lambda l, j: (0, 0)),
            pl.BlockSpec((1, d, tn), lambda l, j: (l, 0, j)),
            pl.BlockSpec((1, 1, tn), lambda l, j: (l, 0, j)),
        ],
        out_specs=pl.BlockSpec((1, rows, tn), lambda l, j: (l, 0, j)),
        out_shape=jax.ShapeDtypeStruct((depth, rows, n), F32),
        compiler_params=_params(("parallel", "parallel")),
        name="modulation",
    )(cond, ada_w, ada_b.reshape(depth, 1, n))


def _inproj_kernel(x_ref, sc_ref, sh_ref, wm_ref, wg_ref, wgt_ref,
                   qk_ref, v_ref, o_ref, nq_ref, nk_ref, nv_ref, g_ref, gt_ref):
    u = x_ref[0] * (1.0 + sc_ref[0]) + sh_ref[0]
    p = _dg(u.astype(BF16), wm_ref[...])
    offs = np.cumsum([0, ML_QK_W, ML_V_W, ML_V_W, NA_W, NA_W, NA_W])
    qk, v, o, nq, nk_, nv = [p[:, a:b] for a, b in zip(offs[:-1], offs[1:])]
    qk_ref[0] = qk
    v_ref[0] = v.astype(BF16)
    o_ref[0] = o.astype(BF16)
    nq_ref[0] = (nq * (NA_DH ** -0.5)).astype(BF16)
    nk_ref[0] = nk_.astype(BF16)
    nv_ref[0] = nv.astype(BF16)
    g_ref[0] = _dot3(u, wg_ref[...])
    gt_ref[0] = _dot3(wgt_ref[...], u, NT)


def _inproj(x, sc, sh, wm, wg, wgt):
    b, t, d = x.shape
    tm = min(t, TOKEN_TILE)
    tok = lambda w, dt: jax.ShapeDtypeStruct((b, t, w), dt)
    blk = lambda w: pl.BlockSpec((1, tm, w), lambda i, j: (i, j, 0))
    return pl.pallas_call(
        _inproj_kernel,
        grid=(b, t // tm),
        in_specs=[
            blk(d),
            pl.BlockSpec((1, 1, d), lambda i, j: (i, 0, 0)),
            pl.BlockSpec((1, 1, d), lambda i, j: (i, 0, 0)),
            pl.BlockSpec(wm.shape, lambda i, j: (0, 0)),
            pl.BlockSpec(wg.shape, lambda i, j: (0, 0)),
            pl.BlockSpec(wgt.shape, lambda i, j: (0, 0)),
        ],
        out_specs=[blk(ML_QK_W), blk(ML_V_W), blk(ML_V_W), blk(NA_W), blk(NA_W), blk(NA_W), blk(LANES),
                   pl.BlockSpec((1, ML_G_W, tm), lambda i, j: (i, 0, j))],
        out_shape=[tok(ML_QK_W, F32), tok(ML_V_W, BF16), tok(ML_V_W, BF16), tok(NA_W, BF16), tok(NA_W, BF16),
                   tok(NA_W, BF16), tok(LANES, F32), jax.ShapeDtypeStruct((b, ML_G_W, t), F32)],
        compiler_params=_params(("parallel", "parallel")),
        name="inproj",
    )(x, sc, sh, wm, wg, wgt)


CONV_TILE = 128


def _fill_padded(qk_ref, pad_ref, t):
    width = qk_ref.shape[-1]
    zero = jnp.zeros((8, width), F32)
    pad_ref[pl.ds(0, 8), :] = zero
    pad_ref[pl.ds(8 + t, 8), :] = zero

    def copy(i, carry):
        r0 = pl.multiple_of(i * CONV_TILE, CONV_TILE)
        pad_ref[pl.ds(r0 + 8, CONV_TILE), :] = qk_ref[0, pl.ds(r0, CONV_TILE), :]
        return carry

    lax.fori_loop(0, t // CONV_TILE, copy, 0)


def _conv_tile(pad_ref, cw_ref, r0):
    n = CONV_TILE + 16
    win = pad_ref[pl.ds(r0, n), :]
    acc = None
    for j in range(CONV_W):
        k = 6 + j
        sh = pltpu.roll(win, n - k, axis=0)[:CONV_TILE]
        term = sh * cw_ref[pl.ds(j, 1), :]
        acc = term if acc is None else acc + term
    return acc * _sigmoid(acc)


def _rope_tile(y, cos, sin):
    lane = lax.broadcasted_iota(jnp.int32, (1, y.shape[1]), 1)
    first = (lane % ML_DK) < (ML_DK // 2)
    n = y.shape[1]
    partner = jnp.where(first, pltpu.roll(y, n - ML_DK // 2, axis=1), pltpu.roll(y, ML_DK // 2, axis=1))
    cos4 = jnp.concatenate([cos] * (n // LANES), axis=1)
    sin4 = jnp.concatenate([sin] * (n // LANES), axis=1)
    return y * cos4 + partner * sin4


def _mlstm_kernel(need_ctx, t_x, t_c,
                  qkx_ref, vx_ref, ox_ref, gx_ref, gtx_ref,
                  qkc_ref, vc_ref, oc_ref, gc_ref, gtc_ref,
                  cos_ref, sin_ref, cw_ref, ks_ref, bg_ref, bgt_ref, nw_ref,
                  *rest):
    if need_ctx:
        mlx_ref, mlc_ref = rest[:2]
        scratch = rest[2:]
    else:
        mlx_ref, mlc_ref = rest[0], None
        scratch = rest[1:]
    padx_ref, padc_ref, qsx_ref, qsc_ref, hfx_ref, hbx_ref, hfc_ref, hbc_ref, st_ref, m_ref = scratch

    _fill_padded(qkx_ref, padx_ref, t_x)
    _fill_padded(qkc_ref, padc_ref, t_c)
    kscale = ks_ref[...]

    def prep_x(i, carry):
        r0 = pl.multiple_of(i * CONV_TILE, CONV_TILE)
        y = _conv_tile(padx_ref, cw_ref, r0)
        y = _rope_tile(y, cos_ref[pl.ds(r0, CONV_TILE), :], sin_ref[pl.ds(r0, CONV_TILE), :])
        qsx_ref[pl.ds(r0, CONV_TILE), :] = y * kscale
        return carry

    def prep_c(i, carry):
        r0 = pl.multiple_of(i * CONV_TILE, CONV_TILE)
        qsc_ref[pl.ds(r0, CONV_TILE), :] = _conv_tile(padc_ref, cw_ref, r0) * kscale
        return carry

    lax.fori_loop(0, t_x // CONV_TILE, prep_x, 0)
    lax.fori_loop(0, t_c // CONV_TILE, prep_c, 0)

    st_ref[...] = jnp.zeros(st_ref.shape, F32)
    m_ref[...] = jnp.zeros(m_ref.shape, F32)

    L = ML_CHUNK
    row = lax.broadcasted_iota(jnp.int32, (L, L), 0)
    col = lax.broadcasted_iota(jnp.int32, (L, L), 1)
    lower = col <= row
    upper = col >= row
    ones_pad = (lax.broadcasted_iota(jnp.int32, (L, ML_DV), 1) == 0).astype(BF16)
    bg = bg_ref[...]
    bgt = bgt_ref[...]

    def chunk_pair(qs_ref, v_ref, g_ref, gt_ref, hf_ref, hb_ref, n_chunks, write_h):
        def body(i, carry):
            for d in range(2):
                c = i if d == 0 else n_chunks - 1 - i
                r0 = pl.multiple_of(c * L, L)
                mask = lower if d == 0 else upper
                mask_t = upper if d == 0 else lower
                qk = qs_ref[pl.ds(r0, L), :]
                vv = v_ref[0, pl.ds(r0, L), :]
                g = g_ref[0, pl.ds(r0, L), :][:, 0:ML_G_W] + bg
                gt = gt_ref[0, c] + bgt
                lsg = _log_sigmoid(g)
                lsgt = _log_sigmoid(gt)
                h_ref = hf_ref if d == 0 else hb_ref
                for h in range(ML_HEADS):
                    ci = 2 * d * ML_HEADS + h
                    fi = (2 * d + 1) * ML_HEADS + h
                    li_col, lf_col = g[:, ci:ci + 1], lsg[:, fi:fi + 1]
                    li_row, lf_row = gt[ci:ci + 1, :], lsgt[fi:fi + 1, :]
                    b_col = jnp.sum(jnp.where(mask, lf_row, 0.0), axis=1, keepdims=True)
                    b_row = jnp.sum(jnp.where(mask_t, lf_col, 0.0), axis=0, keepdims=True)
                    b_last = jnp.sum(lf_row, axis=1, keepdims=True)
                    sidx = d * ML_HEADS + h
                    m_in = m_ref[sidx][:, 0:1]
                    ct = st_ref[sidx]
                    q_h = qk[:, h * ML_DK:(h + 1) * ML_DK]
                    k_h = qk[:, ML_QK_W // 2 + h * ML_DK: ML_QK_W // 2 + (h + 1) * ML_DK]
                    v_ext = jnp.concatenate([vv[:, h * ML_DV:(h + 1) * ML_DV], ones_pad], axis=1)
                    if write_h:
                        dm = jnp.where(mask, b_col - b_row + li_row, NEG_BIG)
                        m_inter = b_col + m_in
                        m_j = jnp.maximum(m_inter, jnp.max(dm, axis=1, keepdims=True))
                        s = _dg(q_h.astype(BF16), k_h.astype(BF16), NT) * jnp.exp(dm - m_j)
                        inter = jnp.exp(m_inter - m_j)
                        ne = _dg(s.astype(BF16), v_ext) + inter * _dg(q_h.astype(BF16), ct.astype(BF16))
                        den = jnp.maximum(jnp.abs(ne[:, ML_DV:ML_DV + 1]), jnp.exp(-m_j))
                        h_ref[pl.ds(r0, L), h * ML_DV:(h + 1) * ML_DV] = ne[:, 0:ML_DV] / den
                    a_col = b_last - b_col + li_col
                    m_loc = jnp.max(a_col, axis=0, keepdims=True)
                    kw = (k_h * jnp.exp(a_col - m_loc)).T.astype(BF16)
                    c_loc = _dg(kw, v_ext)
                    m_new = jnp.maximum(b_last + m_in, m_loc)
                    st_ref[sidx] = jnp.exp(b_last + m_in - m_new) * ct + jnp.exp(m_loc - m_new) * c_loc
                    m_ref[sidx] = jnp.broadcast_to(m_new, (1, LANES))
            return carry

        lax.fori_loop(0, n_chunks, body, 0)

    chunk_pair(qsc_ref, vc_ref, gc_ref, gtc_ref, hfc_ref, hbc_ref, t_c // L, need_ctx)
    chunk_pair(qsx_ref, vx_ref, gx_ref, gtx_ref, hfx_ref, hbx_ref, t_x // L, True)

    nw = nw_ref[...]

    def finish(hf_ref, hb_ref, o_ref, out_ref, t):
        def body(i, carry):
            r0 = pl.multiple_of(i * CONV_TILE, CONV_TILE)
            hsum = hf_ref[pl.ds(r0, CONV_TILE), :] + hb_ref[pl.ds(r0, CONV_TILE), :]
            parts = []
            for h in range(ML_HEADS):
                hh = hsum[:, h * ML_DV:(h + 1) * ML_DV]
                mu = jnp.mean(hh, axis=1, keepdims=True)
                hc = hh - mu
                var = jnp.mean(hc * hc, axis=1, keepdims=True)
                parts.append(hc * lax.rsqrt(var + LN_EPS))
            y = jnp.concatenate(parts, axis=1) * nw
            y = y * _sigmoid(o_ref[0, pl.ds(r0, CONV_TILE), :].astype(F32))
            out_ref[0, pl.ds(r0, CONV_TILE), :] = y.astype(BF16)
            return carry

        lax.fori_loop(0, t // CONV_TILE, body, 0)

    finish(hfx_ref, hbx_ref, ox_ref, mlx_ref, t_x)
    if need_ctx:
        finish(hfc_ref, hbc_ref, oc_ref, mlc_ref, t_c)


def _mlstm(need_ctx, px, pc, cos, sin, conv_w, kscale, b_gate, ml_norm_w):
    qkx, vx, ox, gx, gtx = px
    qkc, vc, oc, gc, gtc = pc
    b, t_x, _ = qkx.shape
    t_c = qkc.shape[1]
    L = ML_CHUNK
    gtx = gtx.reshape(b, ML_G_W, t_x // L, L).transpose(0, 2, 1, 3)
    gtc = gtc.reshape(b, ML_G_W, t_c // L, L).transpose(0, 2, 1, 3)
    cw = jnp.zeros((8, ML_QK_W), F32).at[:CONV_W].set(conv_w)
    tokx = lambda w: pl.BlockSpec((1, t_x, w), lambda i: (i, 0, 0))
    tokc = lambda w: pl.BlockSpec((1, t_c, w), lambda i: (i, 0, 0))
    const = lambda a: pl.BlockSpec(a.shape, lambda i: (0,) * a.ndim)
    bg = b_gate.reshape(1, ML_G_W)
    bgt = b_gate.reshape(ML_G_W, 1)
    nw = ml_norm_w.reshape(1, ML_V_W)
    out_specs = [tokx(ML_V_W)]
    out_shape = [jax.ShapeDtypeStruct((b, t_x, ML_V_W), BF16)]
    if need_ctx:
        out_specs.append(tokc(ML_V_W))
        out_shape.append(jax.ShapeDtypeStruct((b, t_c, ML_V_W), BF16))
    outs = pl.pallas_call(
        functools.partial(_mlstm_kernel, need_ctx, t_x, t_c),
        grid=(b,),
        in_specs=[
            tokx(ML_QK_W), tokx(ML_V_W), tokx(ML_V_W), tokx(LANES),
            pl.BlockSpec((1, t_x // L, ML_G_W, L), lambda i: (i, 0, 0, 0)),
            tokc(ML_QK_W), tokc(ML_V_W), tokc(ML_V_W), tokc(LANES),
            pl.BlockSpec((1, t_c // L, ML_G_W, L), lambda i: (i, 0, 0, 0)),
            const(cos), const(sin), const(cw), const(kscale), const(bg), const(bgt), const(nw),
        ],
        out_specs=out_specs,
        out_shape=out_shape,
        scratch_shapes=[
            pltpu.VMEM((t_x + 16, ML_QK_W), F32), pltpu.VMEM((t_c + 16, ML_QK_W), F32),
            pltpu.VMEM((t_x, ML_QK_W), F32), pltpu.VMEM((t_c, ML_QK_W), F32),
            pltpu.VMEM((t_x, ML_V_W), F32), pltpu.VMEM((t_x, ML_V_W), F32),
            pltpu.VMEM((t_c, ML_V_W), F32), pltpu.VMEM((t_c, ML_V_W), F32),
            pltpu.VMEM((2 * ML_HEADS, ML_DK, 2 * ML_DV), F32),
            pltpu.VMEM((2 * ML_HEADS, 1, LANES), F32),
        ],
        compiler_params=_params(("parallel",)),
        name="mlstm",
    )(qkx, vx, ox, gx, gtx, qkc, vc, oc, gc, gtc, cos, sin, cw, kscale, bg, bgt, nw)
    return (outs[0], outs[1]) if need_ctx else (outs[0], None)


def _rope_tables(t):
    pos = np.arange(t)
    n_freq = ML_DK // 4
    inv = ROPE_BASE ** (-np.arange(n_freq, dtype=np.float32) / n_freq)
    ang = np.concatenate([(pos // GRID_W)[:, None] * inv, (pos % GRID_W)[:, None] * inv], -1).astype(np.float32)
    cos, sin = np.cos(ang), np.sin(ang)
    cos_h = np.concatenate([cos, cos], -1)
    sin_h = np.concatenate([-sin, sin], -1)
    return (jnp.asarray(np.concatenate([cos_h, cos_h], -1), F32),
            jnp.asarray(np.concatenate([sin_h, sin_h], -1), F32))


NA_GROUP = 4
NA_BAND = NA_WIN_H + NA_GROUP - 1
NA_FINISH_TILE = 512


def _head_masks():
    lane = lax.broadcasted_iota(jnp.int32, (1, 2 * NA_DH), 1)
    return lane < NA_DH, lane >= NA_DH


def _na_kernel(rows, q_ref, k_ref, v_ref, kc_ref, vc_ref, bm_ref, o_ref, sctx_ref, pctx_ref, oloc_ref, den_ref):
    masks = _head_masks()
    t = rows * GRID_W
    n_groups = rows // NA_GROUP
    gq = NA_GROUP * GRID_W
    band = NA_BAND * GRID_W
    kc = kc_ref[0]
    vc = vc_ref[0]

    for hh in range(2):
        q_all = q_ref[0]
        sctx_ref[hh] = _dg(jnp.where(masks[hh], q_all, jnp.zeros_like(q_all)), kc, NT)

    def group(g, carry):
        b0 = jnp.clip(NA_GROUP * g - NA_WIN_H // 2, 0, rows - NA_BAND)
        kind = jnp.where(g == 0, 0, jnp.where(g == n_groups - 1, 2, 1))
        qrows = pl.ds(pl.multiple_of(g * gq, gq), gq)
        krows = pl.ds(pl.multiple_of(b0 * GRID_W, GRID_W), band)
        q = q_ref[0, qrows, :]
        kb = k_ref[0, krows, :]
        vb = v_ref[0, krows, :]
        for hh in range(2):
            qm = jnp.where(masks[hh], q, jnp.zeros_like(q))
            s_loc = _dg(qm, kb, NT) + bm_ref[hh, kind]
            s_ctx = sctx_ref[hh, qrows, :]
            m = jnp.maximum(jnp.max(s_loc, axis=1, keepdims=True), jnp.max(s_ctx, axis=1, keepdims=True))
            p_loc = jnp.exp(s_loc - m)
            p_ctx = jnp.exp(s_ctx - m)
            den = jnp.sum(p_loc, axis=1, keepdims=True) + jnp.sum(p_ctx, axis=1, keepdims=True)
            pctx_ref[hh, qrows, :] = p_ctx.astype(BF16)
            den_ref[hh, qrows, :] = jnp.broadcast_to(den, (gq, 2 * NA_DH))
            oloc_ref[hh, qrows, :] = _dg(p_loc.astype(BF16), vb)
        return carry

    lax.fori_loop(0, n_groups, group, 0)

    def finish(i, carry):
        trows = pl.ds(pl.multiple_of(i * NA_FINISH_TILE, NA_FINISH_TILE), NA_FINISH_TILE)
        outs = []
        for hh in range(2):
            o = oloc_ref[hh, trows, :] + _dg(pctx_ref[hh, trows, :], vc)
            outs.append(o / den_ref[hh, trows, :])
        o_ref[0, trows, :] = jnp.where(masks[0], outs[0], outs[1]).astype(BF16)
        return carry

    lax.fori_loop(0, t // NA_FINISH_TILE, finish, 0)


def _na_latent(nq, nk, nv, nkc, nvc, bias):
    b, t, _ = nq.shape
    t_c = nkc.shape[1]
    rows = t // GRID_W
    assert rows % NA_GROUP == 0 and rows >= NA_BAND and t % NA_FINISH_TILE == 0
    tok = lambda tt: pl.BlockSpec((1, tt, 2 * NA_DH), lambda i, j: (i, 0, j))
    return pl.pallas_call(
        functools.partial(_na_kernel, rows),
        grid=(b, NA_HEADS // 2),
        in_specs=[tok(t), tok(t), tok(t), tok(t_c), tok(t_c),
                  pl.BlockSpec((2,) + bias.shape[1:], lambda i, j: (j, 0, 0, 0))],
        out_specs=tok(t),
        out_shape=jax.ShapeDtypeStruct((b, t, NA_W), BF16),
        scratch_shapes=[pltpu.VMEM((2, t, t_c), F32), pltpu.VMEM((2, t, t_c), BF16),
                        pltpu.VMEM((2, t, 2 * NA_DH), F32), pltpu.VMEM((2, t, 2 * NA_DH), F32)],
        compiler_params=_params(("parallel", "parallel")),
        name="na_latent",
    )(nq, nk, nv, nkc, nvc, bias)


def _nactx_kernel(q_ref, k_ref, v_ref, o_ref):
    masks = _head_masks()
    q, k, v = q_ref[0], k_ref[0], v_ref[0]
    outs = []
    for hh in range(2):
        qm = jnp.where(masks[hh], q, jnp.zeros_like(q))
        s = _dg(qm, k, NT)
        p = jnp.exp(s - jnp.max(s, axis=1, keepdims=True))
        outs.append(_dg(p.astype(BF16), v) / jnp.sum(p, axis=1, keepdims=True))
    o_ref[0] = jnp.where(masks[0], outs[0], outs[1]).astype(BF16)


def _na_ctx(nqc, nkc, nvc):
    b, t_c, _ = nqc.shape
    tok = pl.BlockSpec((1, t_c, 2 * NA_DH), lambda i, j: (i, 0, j))
    return pl.pallas_call(
        _nactx_kernel,
        grid=(b, NA_HEADS // 2),
        in_specs=[tok, tok, tok],
        out_specs=tok,
        out_shape=jax.ShapeDtypeStruct((b, t_c, NA_W), BF16),
        compiler_params=_params(("parallel", "parallel")),
        name="na_ctx",
    )(nqc, nkc, nvc)


def _na_bias(rpb):
    c = np.arange(GRID_W)
    win_lo = np.clip(c - NA_WIN_W // 2, 0, GRID_W - NA_WIN_W)
    ok = (c[None, :] >= win_lo[:, None]) & (c[None, :] < win_lo[:, None] + NA_WIN_W)
    span = np.clip(np.arange(2 * GRID_W - 1) - (GRID_W - 1), 1 - NA_WIN_W, NA_WIN_W - 1) + NA_WIN_W - 1
    ext = rpb[:, :, span]
    cols = jnp.stack([ext[:, :, GRID_W - 1 - q: 2 * GRID_W - 1 - q] for q in range(GRID_W)], axis=2)
    cols = jnp.where(ok[None, None], cols, NEG_BIG).transpose(0, 2, 1, 3)
    half = NA_WIN_H // 2
    kinds = [lambda rq: (rq, 0),
             lambda rq: (half + rq, rq),
             lambda rq: (NA_BAND - NA_GROUP + rq, NA_BAND - NA_WIN_H)]
    pieces = []
    for kind in kinds:
        for rq in range(NA_GROUP):
            q_rel, lo = kind(rq)
            first = lo - q_rel + NA_WIN_H - 1
            window = cols[:, :, first:first + NA_WIN_H]
            pad = ((0, 0), (0, 0), (lo, NA_BAND - NA_WIN_H - lo), (0, 0))
            pieces.append(jnp.pad(window, pad, constant_values=NEG_BIG))
    bias = jnp.stack(pieces, axis=1)
    return bias.reshape(NA_HEADS, len(kinds), NA_GROUP * GRID_W, NA_BAND * GRID_W)


def _outproj_kernel(alpha, ml_ref, na_ref, x_ref, g1_ref, w_ref, lw_ref, lb_ref, sc_ref, sh_ref,
                    x1_ref, ut_ref):
    a = jnp.concatenate([ml_ref[0], na_ref[0]], axis=1)
    y = _dg(a, w_ref[...])
    x1 = _layer_norm(alpha * x_ref[0] + g1_ref[0] * y, lw_ref[...], lb_ref[...])
    x1_ref[0] = x1
    ut_ref[...] = (x1 * (1.0 + sc_ref[0]) + sh_ref[0]).T.astype(BF16)


def _outproj(alpha, ml, na, x, g1, w_out, lw, lb, sc2, sh2):
    b, t, d = x.shape
    tm = min(t, TOKEN_TILE)
    blk = lambda w: pl.BlockSpec((1, tm, w), lambda i, j: (i, j, 0))
    per_b = pl.BlockSpec((1, 1, d), lambda i, j: (i, 0, 0))
    const = lambda a: pl.BlockSpec(a.shape, lambda i, j: (0,) * a.ndim)
    tposed = pl.BlockSpec((d, tm), lambda i, j: (0, i * (t // tm) + j))
    return pl.pallas_call(
        functools.partial(_outproj_kernel, alpha),
        grid=(b, t // tm),
        in_specs=[blk(ML_V_W), blk(NA_W), blk(d), per_b, const(w_out), const(lw), const(lb), per_b, per_b],
        out_specs=[blk(d), tposed],
        out_shape=[jax.ShapeDtypeStruct((b, t, d), F32), jax.ShapeDtypeStruct((d, b * t), BF16)],
        compiler_params=_params(("parallel", "parallel")),
        name="outproj",
    )(ml, na, x, g1, w_out, lw, lb, sc2, sh2)


ROUTE_TB = 256
ROUTE_GROUP = 8
CAND_ROWS = 16 + 7 * 8 + 8


def _cand_tables():
    a_idx = np.zeros(CAND_ROWS, np.int64)
    b_idx = np.zeros(CAND_ROWS, np.int64)
    a_idx[0:16], b_idx[0:16] = 0, np.arange(16)
    for a in range(1, 8):
        a_idx[16 + 8 * (a - 1): 24 + 8 * (a - 1)] = a
        b_idx[16 + 8 * (a - 1): 24 + 8 * (a - 1)] = np.arange(8)
    a_idx[72:80], b_idx[72:80] = np.arange(8, 16), 0
    valid = (a_idx + 1) * (b_idx + 1) <= PEER_TOPK
    flat = (a_idx * PEER_TOPK + b_idx).astype(np.float32)
    neg = np.where(valid, 0.0, -np.inf).astype(np.float32)
    tile = lambda v: jnp.asarray(np.tile(v[:, None], (1, ROUTE_TB)), F32)
    return tile(flat), tile(neg)


def _tree(op, xs):
    while len(xs) > 1:
        xs = [op(xs[i], xs[i + 1]) for i in range(0, len(xs) - 1, 2)] + ([xs[-1]] if len(xs) % 2 else [])
    return xs[0]


def _top_rounds(s, orders, order_end):
    n = s.shape[0] // SUBLANES
    vals = [s[SUBLANES * i:SUBLANES * (i + 1)] for i in range(n)]
    ranks = [jnp.full(vals[0].shape, float(PEER_TOPK), F32)] * n
    tops = []
    for k in range(PEER_TOPK):
        m = jnp.max(_tree(jnp.maximum, vals), axis=0, keepdims=True)
        first = _tree(jnp.minimum, [jnp.where(v == m, o, order_end) for v, o in zip(vals, orders)])
        first = jnp.min(first, axis=0, keepdims=True)
        hits = [o == first for o in orders]
        ranks = [jnp.where(h, float(k), r) for h, r in zip(hits, ranks)]
        vals = [jnp.where(h, -jnp.inf, v) for h, v in zip(hits, vals)]
        tops.append(m)
    return jnp.concatenate(ranks, axis=0), tops


def _one_per_round(rank):
    taken = jnp.sum(jnp.where(rank < float(PEER_TOPK), 1.0, 0.0), axis=0, keepdims=True)
    return jnp.max(jnp.abs(taken - float(PEER_TOPK))) == 0.0


def _sort16_network():
    pairs = []

    def merge(lo, n, r):
        step = r * 2
        if step < n:
            merge(lo, n, step)
            merge(lo + r, n, step)
            pairs.extend((i, i + r) for i in range(lo + r, lo + n - r, step))
        else:
            pairs.append((lo, lo + r))

    def sort(lo, n):
        if n > 1:
            sort(lo, n // 2)
            sort(lo + n // 2, n // 2)
            merge(lo, n, 1)

    sort(0, PEER_TOPK)
    return pairs


def _larger_smaller(a, b):
    if b is None:
        return a, None
    if a is None:
        return b, None
    return jnp.maximum(a, b), jnp.minimum(a, b)


def _top_sorted(slabs):
    v = list(slabs) + [None] * (PEER_TOPK - len(slabs))
    for i, j in _sort16_network():
        v[i], v[j] = _larger_smaller(v[i], v[j])
    for shift in (4, 2, 1):
        moved = [None if a is None else pltpu.roll(a, shift, axis=0) for a in v]
        v = [_larger_smaller(v[k], moved[PEER_TOPK - 1 - k])[0] for k in range(PEER_TOPK)]
        for d in (8, 4, 2, 1):
            for i in range(PEER_TOPK):
                if i & d == 0:
                    v[i], v[i + d] = _larger_smaller(v[i], v[i + d])
    return v


def _rank_by_count(s, tops):
    n = s.shape[0] // SUBLANES
    ranks = []
    for i in range(n):
        v = s[SUBLANES * i:SUBLANES * (i + 1)]
        r = jnp.zeros(v.shape, F32)
        for a, t in enumerate(tops):
            r = jnp.where(t > v, float(a + 1), r)
        ranks.append(r)
    return jnp.concatenate(ranks, axis=0)


def _strictly_descending(tops):
    steps = [jnp.where(a > b, 1.0, 0.0) for a, b in zip(tops[:-1], tops[1:])]
    return jnp.min(_tree(jnp.minimum, steps)) == 1.0


def _route_kernel(ut_ref, wq_ref, kh_ref, kl_ref, flat_ref, neg_ref,
                  lim_ref, ea_ref, r2_ref, eb_ref, qt_ref, rank_ref, t_ref, e_ref, pick_ref):
    tb = ut_ref.shape[1]
    nk = PEER_NKEYS
    qt_ref[...] = _dg(wq_ref[...], ut_ref[...])
    sub_iota = lax.broadcasted_iota(jnp.int32, (SUBLANES, tb), 0).astype(F32)
    key_order = [sub_iota + float(SUBLANES * i) for i in range(nk // SUBLANES)]

    def sub_scores(i, carry):
        todo = []
        for g in range(ROUTE_GROUP):
            hh = ROUTE_GROUP * i + g
            q = qt_ref[pl.ds(pl.multiple_of(hh * nk, nk), nk), :]
            qh, ql = _split_bf16(q)
            kh = kh_ref[hh]
            s0 = _dg(kh, qh) + (_dg(kh, ql) + _dg(kl_ref[hh], qh))
            tops = _top_sorted([s0[SUBLANES * r:SUBLANES * (r + 1)] for r in range(nk // SUBLANES)])
            rank = _rank_by_count(s0, tops)
            rank_ref[hh] = rank
            t_ref[hh] = jnp.concatenate([t[0:1] for t in tops], axis=0)
            e_ref[hh] = jnp.exp(s0 - tops[0][0:1])
            todo.append((hh, s0, jnp.logical_and(_strictly_descending(tops), _one_per_round(rank))))

        @pl.when(jnp.logical_not(functools.reduce(jnp.logical_and, [ok for _, _, ok in todo])))
        def _():
            for hh, s0, _ in todo:
                rank_t, tops_t = _top_rounds(s0, key_order, float(nk))
                rank_ref[hh] = rank_t
                t_ref[hh] = jnp.concatenate(tops_t, axis=0)

        return carry

    lax.fori_loop(0, 2 * PEER_HEADS // ROUTE_GROUP, sub_scores, 0)

    def joint(i, carry):
        todo = []
        for g in range(ROUTE_GROUP):
            h = ROUTE_GROUP * i + g
            t1 = t_ref[2 * h]
            t2 = t_ref[2 * h + 1]
            blocks = [t1[0:1] + t2]
            for a in range(1, 8):
                blocks.append(t1[a:a + 1] + t2[0:8])
            blocks.append(t1[8:16] + t2[0:1])
            cand = jnp.concatenate(blocks, axis=0) + neg_ref[...]
            cut = _top_sorted([cand[SUBLANES * r:SUBLANES * (r + 1)] for r in range(CAND_ROWS // SUBLANES)])[-1]
            pick = jnp.where(cand >= cut[0:1], 0.0, float(PEER_TOPK))
            pick_ref[g] = pick
            todo.append((h, cand, t1[0:1] + t2[0:1], _one_per_round(pick)))

        @pl.when(jnp.logical_not(functools.reduce(jnp.logical_and, [ok for _, _, _, ok in todo])))
        def _():
            flat = [flat_ref[SUBLANES * r:SUBLANES * (r + 1), :] for r in range(CAND_ROWS // SUBLANES)]
            for g, (_, cand, _, _) in enumerate(todo):
                pick_ref[g] = _top_rounds(cand, flat, 1e9)[0]

        for g, (h, cand, top, _) in enumerate(todo):
            chosen = jnp.where(pick_ref[g] < float(PEER_TOPK), 1.0, 0.0)
            z = jnp.sum(chosen * jnp.exp(cand - top), axis=0, keepdims=True)
            counts = [jnp.sum(chosen[0:16], axis=0, keepdims=True)]
            for a in range(1, 8):
                counts.append(jnp.sum(chosen[16 + 8 * (a - 1): 24 + 8 * (a - 1)], axis=0, keepdims=True))
            tail = chosen[72:80]
            for a in range(8):
                counts.append(tail[a:a + 1])
            rank1 = rank_ref[2 * h].astype(BF16)
            lim = jnp.zeros((nk, tb), BF16)
            for a in range(PEER_TOPK):
                lim = jnp.where(rank1 == float(a), jnp.broadcast_to(counts[a], (nk, tb)).astype(BF16), lim)
            lim_ref[h] = lim.astype(F32)
            ea_ref[h] = e_ref[2 * h] * (1.0 / z)
            r2_ref[h] = rank_ref[2 * h + 1].astype(BF16)
            eb_ref[h] = e_ref[2 * h + 1].astype(BF16)
        return carry

    lax.fori_loop(0, PEER_HEADS // ROUTE_GROUP, joint, 0)


def _route(ut, wq, kh, kl):
    d, n = ut.shape
    tb = ROUTE_TB
    flat, neg = _cand_tables()
    nk = PEER_NKEYS
    const = lambda a: pl.BlockSpec(a.shape, lambda i: (0,) * a.ndim)
    tok = pl.BlockSpec((d, tb), lambda i: (0, i))
    out_blk = pl.BlockSpec((PEER_HEADS, nk, tb), lambda i: (0, 0, i))
    meta = lambda dt: jax.ShapeDtypeStruct((PEER_HEADS, nk, n), dt)
    return pl.pallas_call(
        _route_kernel,
        grid=(n // tb,),
        in_specs=[tok, const(wq), const(kh), const(kl), const(flat), const(neg)],
        out_specs=[out_blk, out_blk, out_blk, out_blk],
        out_shape=[meta(F32), meta(F32), meta(BF16), meta(BF16)],
        scratch_shapes=[
            pltpu.VMEM((2 * PEER_HEADS * nk, tb), F32),
            pltpu.VMEM((2 * PEER_HEADS, nk, tb), F32),
            pltpu.VMEM((2 * PEER_HEADS, PEER_TOPK, tb), F32),
            pltpu.VMEM((2 * PEER_HEADS, nk, tb), F32),
            pltpu.VMEM((ROUTE_GROUP, CAND_ROWS, tb), F32),
        ],
        compiler_params=_params(("parallel",)),
        name="peer_route",
    )(ut, wq, kh, kl, flat, neg)


PEER_TB = 512
PEER_EB = 2048


def _gelu_tanh(x):
    return 0.5 * x * (1.0 + jnp.tanh(0.7978845608028654 * (x + 0.044715 * (x * x * x))))


def _peer_kernel(alpha, n_eb, ut_ref, up_ref, dnt_ref, lim_ref, ea_ref, r2_ref, eb_ref,
                 x1_ref, g2_ref, lw_ref, lb_ref, out_ref, acc_ref, z_ref):
    e = pl.program_id(1)
    nk = PEER_NKEYS
    tb = ut_ref.shape[1]

    @pl.when(e == 0)
    def _():
        acc_ref[...] = jnp.zeros(acc_ref.shape, F32)

    ht = _dg(up_ref[...], ut_ref[...])
    zero = jnp.zeros((nk, tb), BF16)
    for j in range(PEER_EB // nk):
        act = _gelu_tanh(ht[j * nk:(j + 1) * nk].astype(BF16))
        gate = None
        for h in range(PEER_HEADS):
            lim = jnp.broadcast_to(lim_ref[h, j:j + 1, :], (nk, tb)).astype(BF16)
            ea = jnp.broadcast_to(ea_ref[h, j:j + 1, :], (nk, tb)).astype(BF16)
            term = jnp.where(r2_ref[h] < lim, eb_ref[h], zero) * ea
            gate = term if gate is None else gate + term
        z_ref[j * nk:(j + 1) * nk, :] = act * gate
    acc_ref[...] += _dg(dnt_ref[...], z_ref[...])

    @pl.when(e == n_eb - 1)
    def _():
        y = acc_ref[...].T
        out_ref[...] = _layer_norm(alpha * x1_ref[...] + g2_ref[0] * y, lw_ref[...], lb_ref[...])


def _peer_experts(alpha, ut, up, dnt, lim, ea, r2, eb, x1, g2, lw, lb, tokens_per_batch):
    n, d = x1.shape
    tb = min(PEER_TB, tokens_per_batch)
    n_exp = up.shape[0]
    n_eb = n_exp // PEER_EB
    nk = PEER_NKEYS
    keys_per_step = PEER_EB // nk
    tok = pl.BlockSpec((tb, d), lambda i, e: (i, 0))
    row_meta = pl.BlockSpec((PEER_HEADS, keys_per_step, tb), lambda i, e: (0, e, i))
    col_meta = pl.BlockSpec((PEER_HEADS, nk, tb), lambda i, e: (0, 0, i))
    const = lambda a: pl.BlockSpec(a.shape, lambda i, e: (0,) * a.ndim)
    return pl.pallas_call(
        functools.partial(_peer_kernel, alpha, n_eb),
        grid=(n // tb, n_eb),
        in_specs=[pl.BlockSpec((d, tb), lambda i, e: (0, i)),
                  pl.BlockSpec((PEER_EB, d), lambda i, e: (e, 0)),
                  pl.BlockSpec((d, PEER_EB), lambda i, e: (0, e)),
                  row_meta, row_meta, col_meta, col_meta,
                  tok,
                  pl.BlockSpec((1, 1, d), lambda i, e: (i * tb // tokens_per_batch, 0, 0)),
                  const(lw), const(lb)],
        out_specs=tok,
        out_shape=jax.ShapeDtypeStruct((n, d), F32),
        scratch_shapes=[pltpu.VMEM((d, tb), F32), pltpu.VMEM((PEER_EB, tb), BF16)],
        compiler_params=_params(("parallel", "arbitrary")),
        name="peer_experts",
    )(ut, up, dnt, lim, ea, r2, eb, x1, g2, lw, lb)


def kernel(x, c, ctx, c_ctx, ada_w, ada_b, w_in, b_gate, conv_w, ml_norm_w, na_rpb, w_out, ln1_w, ln1_b,
           peer_wq, peer_keys, peer_up, peer_down, ln2_w, ln2_b):
    depth = ada_w.shape[0]
    b, t, d = x.shape
    t_c = ctx.shape[1]
    alpha = (2.0 * depth) ** 0.25
    rows = t // GRID_W

    n_cond = b + 1
    cond = jnp.zeros((-(-n_cond // 8) * 8, d), F32).at[:b].set(c).at[b].set(c_ctx)
    mods = _modulation(cond, ada_w, ada_b)
    cos, sin = _rope_tables(t)
    kscale = jnp.concatenate([jnp.ones((1, ML_QK_W // 2), F32), jnp.full((1, ML_QK_W // 2), ML_DK ** -0.5, F32)], 1)

    g_lo = ML_QK_W + 2 * ML_V_W
    g_hi = g_lo + ML_G_W
    for l in range(depth):
        need_ctx = l < depth - 1
        mod = mods[l].reshape(-1, N_MOD, d)
        lat = [mod[:b, i].reshape(b, 1, d) for i in range(N_MOD)]
        cxm = [jnp.broadcast_to(mod[b, i].reshape(1, 1, d), (b, 1, d)) for i in range(N_MOD)]

        w = w_in[l]
        wm = jnp.concatenate([w[:, :g_lo], w[:, g_hi:]], axis=1).astype(BF16)
        wg = jnp.zeros((d, LANES), F32).at[:, :ML_G_W].set(w[:, g_lo:g_hi])
        wgt = w[:, g_lo:g_hi].T

        qkx, vx, ox, nqx, nkx, nvx, gx, gtx = _inproj(x, lat[1], lat[0], wm, wg, wgt)
        qkc, vc, oc, nqc, nkc, nvc, gc, gtc = _inproj(ctx, cxm[1], cxm[0], wm, wg, wgt)

        ml_x, ml_c = _mlstm(need_ctx, (qkx, vx, ox, gx, gtx), (qkc, vc, oc, gc, gtc),
                            cos, sin, conv_w[l], kscale, b_gate[l], ml_norm_w[l])
        na_x = _na_latent(nqx, nkx, nvx, nkc, nvc, _na_bias(na_rpb[l]))

        wo = w_out[l].astype(BF16)
        lw1, lb1 = ln1_w[l].reshape(1, d), ln1_b[l].reshape(1, d)
        lw2, lb2 = ln2_w[l].reshape(1, d), ln2_b[l].reshape(1, d)
        wq_t = peer_wq[l].T.astype(BF16)
        keys = peer_keys[l].reshape(2 * PEER_HEADS, PEER_NKEYS, -1)
        kh = keys.astype(BF16)
        kl = (keys - kh.astype(F32)).astype(BF16)
        up = peer_up[l].astype(BF16)
        dnt = peer_down[l].T.astype(BF16)

        def channel(xin, ml, na, m, tokens):
            x1, ut = _outproj(alpha, ml, na, xin, m[2], wo, lw1, lb1, m[4], m[3])
            lim, ea, r2, eb = _route(ut, wq_t, kh, kl)
            out = _peer_experts(alpha, ut, up, dnt, lim, ea, r2, eb, x1.reshape(-1, d), m[5], lw2, lb2, tokens)
            return out.reshape(xin.shape)

        x_new = channel(x, ml_x, na_x, lat, t)
        if need_ctx:
            na_c = _na_ctx(nqc, nkc, nvc)
            ctx = channel(ctx, ml_c, na_c, cxm, t_c)
        x = x_new
    return x
```

```python
import functools

import jax
import jax.numpy as jnp
import numpy as np
from jax import lax
from jax.experimental import pallas as pl
from jax.experimental.pallas import tpu as pltpu

F32 = jnp.float32
BF16 = jnp.bfloat16

GRID_W = 64
ML_HEADS = 4
ML_DV = 128
ML_DK = 64
ML_CHUNK = 256
CONV_W = 5
NA_HEADS = 8
NA_DH = 64
NA_WIN_H = 8
NA_WIN_W = 16
ROPE_BASE = 10000.0
PEER_HEADS = 8
PEER_NKEYS = 128
PEER_TOPK = 16
N_MOD = 6
ML_QK_W = 2 * ML_HEADS * ML_DK
ML_V_W = ML_HEADS * ML_DV
ML_G_W = 4 * ML_HEADS
NA_W = NA_HEADS * NA_DH
LN_EPS = 1e-5
NEG_BIG = -1e30

LANES = 128
SUBLANES = 8
VMEM_LIMIT = 56 * 1024 * 1024
TOKEN_TILE = 512

NN = (((1,), (0,)), ((), ()))
NT = (((1,), (1,)), ((), ()))


def _dg(a, b, dims=NN):
    return lax.dot_general(a, b, dims, preferred_element_type=F32)


def _split_bf16(a):
    hi = a.astype(BF16)
    lo = (a - hi.astype(F32)).astype(BF16)
    return hi, lo


def _dot3(a, b, dims=NN):
    ah, al = _split_bf16(a)
    bh, bl = _split_bf16(b)
    return _dg(ah, bh, dims) + (_dg(ah, bl, dims) + _dg(al, bh, dims))


def _sigmoid(x):
    return 1.0 / (1.0 + jnp.exp(-x))


def _log_sigmoid(x):
    return jnp.minimum(x, 0.0) - jnp.log(1.0 + jnp.exp(-jnp.abs(x)))


def _layer_norm(z, w, b):
    mu = jnp.mean(z, axis=-1, keepdims=True)
    zc = z - mu
    var = jnp.mean(zc * zc, axis=-1, keepdims=True)
    return zc * lax.rsqrt(var + LN_EPS) * w + b


def _params(sem):
    return pltpu.CompilerParams(dimension_semantics=sem, vmem_limit_bytes=VMEM_LIMIT)


def _mod_kernel(c_ref, w_ref, b_ref, o_ref):
    c = c_ref[...]
    s = c * _sigmoid(c)
    o_ref[0] = _dot3(s, w_ref[0]) + b_ref[0]


def _modulation(cond, ada_w, ada_b):
    depth, d, n = ada_w.shape
    rows = cond.shape[0]
    tn = 1024
    return pl.pallas_call(
        _mod_kernel,
        grid=(depth, n // tn),
        in_specs=[
            pl.BlockSpec((rows, d), lambda l, j: (0, 0)),
            pl.BlockSpec((1, d, tn), lambda l, j: (l, 0, j)),
            pl.BlockSpec((1, 1, tn), lambda l, j: (l, 0, j)),
        ],
        out_specs=pl.BlockSpec((1, rows, tn), lambda l, j: (l, 0, j)),
        out_shape=jax.ShapeDtypeStruct((depth, rows, n), F32),
        compiler_params=_params(("parallel", "parallel")),
        name="modulation",
    )(cond, ada_w, ada_b.reshape(depth, 1, n))


def _inproj_kernel(x_ref, sc_ref, sh_ref, wm_ref, wg_ref, wgt_ref,
                   qk_ref, v_ref, o_ref, nq_ref, nk_ref, nv_ref, g_ref, gt_ref):
    u = x_ref[0] * (1.0 + sc_ref[0]) + sh_ref[0]
    p = _dg(u.astype(BF16), wm_ref[...])
    offs = np.cumsum([0, ML_QK_W, ML_V_W, ML_V_W, NA_W, NA_W, NA_W])
    qk, v, o, nq, nk_, nv = [p[:, a:b] for a, b in zip(offs[:-1], offs[1:])]
    qk_ref[0] = qk
    v_ref[0] = v.astype(BF16)
    o_ref[0] = o.astype(BF16)
    nq_ref[0] = (nq * (NA_DH ** -0.5)).astype(BF16)
    nk_ref[0] = nk_.astype(BF16)
    nv_ref[0] = nv.astype(BF16)
    g_ref[0] = _dot3(u, wg_ref[...])
    gt_ref[0] = _dot3(wgt_ref[...], u, NT)


def _inproj(x, sc, sh, wm, wg, wgt):
    b, t, d = x.shape
    tm = min(t, TOKEN_TILE)
    tok = lambda w, dt: jax.ShapeDtypeStruct((b, t, w), dt)
    blk = lambda w: pl.BlockSpec((1, tm, w), lambda i, j: (i, j, 0))
    return pl.pallas_call(
        _inproj_kernel,
        grid=(b, t // tm),
        in_specs=[
            blk(d),
            pl.BlockSpec((1, 1, d), lambda i, j: (i, 0, 0)),
            pl.BlockSpec((1, 1, d), lambda i, j: (i, 0, 0)),
            pl.BlockSpec(wm.shape, lambda i, j: (0, 0)),
            pl.BlockSpec(wg.shape, lambda i, j: (0, 0)),
            pl.BlockSpec(wgt.shape, lambda i, j: (0, 0)),
        ],
        out_specs=[blk(ML_QK_W), blk(ML_V_W), blk(ML_V_W), blk(NA_W), blk(NA_W), blk(NA_W), blk(LANES),
                   pl.BlockSpec((1, ML_G_W, tm), lambda i, j: (i, 0, j))],
        out_shape=[tok(ML_QK_W, F32), tok(ML_V_W, BF16), tok(ML_V_W, BF16), tok(NA_W, BF16), tok(NA_W, BF16),
                   tok(NA_W, BF16), tok(LANES, F32), jax.ShapeDtypeStruct((b, ML_G_W, t), F32)],
        compiler_params=_params(("parallel", "parallel")),
        name="inproj",
    )(x, sc, sh, wm, wg, wgt)


CONV_TILE = 128


def _fill_padded(qk_ref, pad_ref, t):
    width = qk_ref.shape[-1]
    zero = jnp.zeros((8, width), F32)
    pad_ref[pl.ds(0, 8), :] = zero
    pad_ref[pl.ds(8 + t, 8), :] = zero

    def copy(i, carry):
        r0 = pl.multiple_of(i * CONV_TILE, CONV_TILE)
        pad_ref[pl.ds(r0 + 8, CONV_TILE), :] = qk_ref[0, pl.ds(r0, CONV_TILE), :]
        return carry

    lax.fori_loop(0, t // CONV_TILE, copy, 0)


def _conv_tile(pad_ref, cw_ref, r0):
    n = CONV_TILE + 16
    win = pad_ref[pl.ds(r0, n), :]
    acc = None
    for j in range(CONV_W):
        k = 6 + j
        sh = pltpu.roll(win, n - k, axis=0)[:CONV_TILE]
        term = sh * cw_ref[pl.ds(j, 1), :]
        acc = term if acc is None else acc + term
    return acc * _sigmoid(acc)


def _rope_tile(y, cos, sin):
    lane = lax.broadcasted_iota(jnp.int32, (1, y.shape[1]), 1)
    first = (lane % ML_DK) < (ML_DK // 2)
    n = y.shape[1]
    partner = jnp.where(first, pltpu.roll(y, n - ML_DK // 2, axis=1), pltpu.roll(y, ML_DK // 2, axis=1))
    cos4 = jnp.concatenate([cos] * (n // LANES), axis=1)
    sin4 = jnp.concatenate([sin] * (n // LANES), axis=1)
    return y * cos4 + partner * sin4


def _mlstm_kernel(need_ctx, t_x, t_c,
                  qkx_ref, vx_ref, ox_ref, gx_ref, gtx_ref,
                  qkc_ref, vc_ref, oc_ref, gc_ref, gtc_ref,
                  cos_ref, sin_ref, cw_ref, ks_ref, bg_ref, bgt_ref, nw_ref,
                  *rest):
    if need_ctx:
        mlx_ref, mlc_ref = rest[:2]
        scratch = rest[2:]
    else:
        mlx_ref, mlc_ref = rest[0], None
        scratch = rest[1:]
    padx_ref, padc_ref, qsx_ref, qsc_ref, hfx_ref, hbx_ref, hfc_ref, hbc_ref, st_ref, m_ref = scratch

    _fill_padded(qkx_ref, padx_ref, t_x)
    _fill_padded(qkc_ref, padc_ref, t_c)
    kscale = ks_ref[...]

    def prep_x(i, carry):
        r0 = pl.multiple_of(i * CONV_TILE, CONV_TILE)
        y = _conv_tile(padx_ref, cw_ref, r0)
        y = _rope_tile(y, cos_ref[pl.ds(r0, CONV_TILE), :], sin_ref[pl.ds(r0, CONV_TILE), :])
        qsx_ref[pl.ds(r0, CONV_TILE), :] = y * kscale
        return carry

    def prep_c(i, carry):
        r0 = pl.multiple_of(i * CONV_TILE, CONV_TILE)
        qsc_ref[pl.ds(r0, CONV_TILE), :] = _conv_tile(padc_ref, cw_ref, r0) * kscale
        return carry

    lax.fori_loop(0, t_x // CONV_TILE, prep_x, 0)
    lax.fori_loop(0, t_c // CONV_TILE, prep_c, 0)

    st_ref[...] = jnp.zeros(st_ref.shape, F32)
    m_ref[...] = jnp.zeros(m_ref.shape, F32)

    L = ML_CHUNK
    row = lax.broadcasted_iota(jnp.int32, (L, L), 0)
    col = lax.broadcasted_iota(jnp.int32, (L, L), 1)
    lower = col <= row
    upper = col >= row
    ones_pad = (lax.broadcasted_iota(jnp.int32, (L, ML_DV), 1) == 0).astype(BF16)
    pair_lane = lax.broadcasted_iota(jnp.int32, (1, LANES), 1)
    pair_half = (pair_lane < ML_DK, pair_lane >= ML_DK)
    bg = bg_ref[...]
    bgt = bgt_ref[...]

    def chunk_pair(qs_ref, v_ref, g_ref, gt_ref, hf_ref, hb_ref, n_chunks, write_h):
        def body(i, carry):
            for d in range(2):
                c = i if d == 0 else n_chunks - 1 - i
                r0 = pl.multiple_of(c * L, L)
                mask = lower if d == 0 else upper
                mask_t = upper if d == 0 else lower
                qk = qs_ref[pl.ds(r0, L), :]
                vv = v_ref[0, pl.ds(r0, L), :]
                g = g_ref[0, pl.ds(r0, L), :][:, 0:ML_G_W] + bg
                gt = gt_ref[0, c] + bgt
                lsg = _log_sigmoid(g)
                lsgt = _log_sigmoid(gt)
                h_ref = hf_ref if d == 0 else hb_ref
                for h in range(ML_HEADS):
                    ci = 2 * d * ML_HEADS + h
                    fi = (2 * d + 1) * ML_HEADS + h
                    li_col, lf_col = g[:, ci:ci + 1], lsg[:, fi:fi + 1]
                    li_row, lf_row = gt[ci:ci + 1, :], lsgt[fi:fi + 1, :]
                    b_col = jnp.sum(jnp.where(mask, lf_row, 0.0), axis=1, keepdims=True)
                    b_row = jnp.sum(jnp.where(mask_t, lf_col, 0.0), axis=0, keepdims=True)
                    b_last = jnp.sum(lf_row, axis=1, keepdims=True)
                    sidx = d * ML_HEADS + h
                    m_in = m_ref[sidx][:, 0:1]
                    ct = st_ref[sidx]
                    lo = (h // 2) * LANES
                    q_h = jnp.where(pair_half[h % 2], qk[:, lo:lo + LANES], 0.0)
                    k_h = qk[:, ML_QK_W // 2 + lo: ML_QK_W // 2 + lo + LANES]
                    v_ext = jnp.concatenate([vv[:, h * ML_DV:(h + 1) * ML_DV], ones_pad], axis=1)
                    if write_h:
                        dm = jnp.where(mask, b_col - b_row + li_row, NEG_BIG)
                        m_inter = b_col + m_in
                        m_j = jnp.maximum(m_inter, jnp.max(dm, axis=1, keepdims=True))
                        s = _dg(q_h.astype(BF16), k_h.astype(BF16), NT) * jnp.exp(dm - m_j)
                        inter = jnp.exp(m_inter - m_j)
                        ne = _dg(s.astype(BF16), v_ext) + inter * _dg(q_h.astype(BF16), ct.astype(BF16))
                        den = jnp.maximum(jnp.abs(ne[:, ML_DV:ML_DV + 1]), jnp.exp(-m_j))
                        h_ref[pl.ds(r0, L), h * ML_DV:(h + 1) * ML_DV] = ne[:, 0:ML_DV] / den
                    a_col = b_last - b_col + li_col
                    m_loc = jnp.max(a_col, axis=0, keepdims=True)
                    kw = (k_h * jnp.exp(a_col - m_loc)).T.astype(BF16)
                    c_loc = _dg(kw, v_ext)
                    m_new = jnp.maximum(b_last + m_in, m_loc)
                    st_ref[sidx] = jnp.exp(b_last + m_in - m_new) * ct + jnp.exp(m_loc - m_new) * c_loc
                    m_ref[sidx] = jnp.broadcast_to(m_new, (1, LANES))
            return carry

        lax.fori_loop(0, n_chunks, body, 0)

    chunk_pair(qsc_ref, vc_ref, gc_ref, gtc_ref, hfc_ref, hbc_ref, t_c // L, need_ctx)
    chunk_pair(qsx_ref, vx_ref, gx_ref, gtx_ref, hfx_ref, hbx_ref, t_x // L, True)

    nw = nw_ref[...]

    def finish(hf_ref, hb_ref, o_ref, out_ref, t):
        def body(i, carry):
            r0 = pl.multiple_of(i * CONV_TILE, CONV_TILE)
            hsum = hf_ref[pl.ds(r0, CONV_TILE), :] + hb_ref[pl.ds(r0, CONV_TILE), :]
            parts = []
            for h in range(ML_HEADS):
                hh = hsum[:, h * ML_DV:(h + 1) * ML_DV]
                mu = jnp.mean(hh, axis=1, keepdims=True)
                hc = hh - mu
                var = jnp.mean(hc * hc, axis=1, keepdims=True)
                parts.append(hc * lax.rsqrt(var + LN_EPS))
            y = jnp.concatenate(parts, axis=1) * nw
            y = y * _sigmoid(o_ref[0, pl.ds(r0, CONV_TILE), :].astype(F32))
            out_ref[0, pl.ds(r0, CONV_TILE), :] = y.astype(BF16)
            return carry

        lax.fori_loop(0, t // CONV_TILE, body, 0)

    finish(hfx_ref, hbx_ref, ox_ref, mlx_ref, t_x)
    if need_ctx:
        finish(hfc_ref, hbc_ref, oc_ref, mlc_ref, t_c)


def _mlstm(need_ctx, px, pc, cos, sin, conv_w, kscale, b_gate, ml_norm_w):
    qkx, vx, ox, gx, gtx = px
    qkc, vc, oc, gc, gtc = pc
    b, t_x, _ = qkx.shape
    t_c = qkc.shape[1]
    L = ML_CHUNK
    gtx = gtx.reshape(b, ML_G_W, t_x // L, L).transpose(0, 2, 1, 3)
    gtc = gtc.reshape(b, ML_G_W, t_c // L, L).transpose(0, 2, 1, 3)
    cw = jnp.zeros((8, ML_QK_W), F32).at[:CONV_W].set(conv_w)
    tokx = lambda w: pl.BlockSpec((1, t_x, w), lambda i: (i, 0, 0))
    tokc = lambda w: pl.BlockSpec((1, t_c, w), lambda i: (i, 0, 0))
    const = lambda a: pl.BlockSpec(a.shape, lambda i: (0,) * a.ndim)
    bg = b_gate.reshape(1, ML_G_W)
    bgt = b_gate.reshape(ML_G_W, 1)
    nw = ml_norm_w.reshape(1, ML_V_W)
    out_specs = [tokx(ML_V_W)]
    out_shape = [jax.ShapeDtypeStruct((b, t_x, ML_V_W), BF16)]
    if need_ctx:
        out_specs.append(tokc(ML_V_W))
        out_shape.append(jax.ShapeDtypeStruct((b, t_c, ML_V_W), BF16))
    outs = pl.pallas_call(
        functools.partial(_mlstm_kernel, need_ctx, t_x, t_c),
        grid=(b,),
        in_specs=[
            tokx(ML_QK_W), tokx(ML_V_W), tokx(ML_V_W), tokx(LANES),
            pl.BlockSpec((1, t_x // L, ML_G_W, L), lambda i: (i, 0, 0, 0)),
            tokc(ML_QK_W), tokc(ML_V_W), tokc(ML_V_W), tokc(LANES),
            pl.BlockSpec((1, t_c // L, ML_G_W, L), lambda i: (i, 0, 0, 0)),
            const(cos), const(sin), const(cw), const(kscale), const(bg), const(bgt), const(nw),
        ],
        out_specs=out_specs,
        out_shape=out_shape,
        scratch_shapes=[
            pltpu.VMEM((t_x + 16, ML_QK_W), F32), pltpu.VMEM((t_c + 16, ML_QK_W), F32),
            pltpu.VMEM((t_x, ML_QK_W), F32), pltpu.VMEM((t_c, ML_QK_W), F32),
            pltpu.VMEM((t_x, ML_V_W), F32), pltpu.VMEM((t_x, ML_V_W), F32),
            pltpu.VMEM((t_c, ML_V_W), F32), pltpu.VMEM((t_c, ML_V_W), F32),
            pltpu.VMEM((2 * ML_HEADS, 2 * ML_DK, 2 * ML_DV), F32),
            pltpu.VMEM((2 * ML_HEADS, 1, LANES), F32),
        ],
        compiler_params=_params(("parallel",)),
        name="mlstm",
    )(qkx, vx, ox, gx, gtx, qkc, vc, oc, gc, gtc, cos, sin, cw, kscale, bg, bgt, nw)
    return (outs[0], outs[1]) if need_ctx else (outs[0], None)


def _rope_tables(t):
    pos = np.arange(t)
    n_freq = ML_DK // 4
    inv = ROPE_BASE ** (-np.arange(n_freq, dtype=np.float32) / n_freq)
    ang = np.concatenate([(pos // GRID_W)[:, None] * inv, (pos % GRID_W)[:, None] * inv], -1).astype(np.float32)
    cos, sin = np.cos(ang), np.sin(ang)
    cos_h = np.concatenate([cos, cos], -1)
    sin_h = np.concatenate([-sin, sin], -1)
    return (jnp.asarray(np.concatenate([cos_h, cos_h], -1), F32),
            jnp.asarray(np.concatenate([sin_h, sin_h], -1), F32))


NA_GROUP = 4
NA_BAND = NA_WIN_H + NA_GROUP - 1
NA_FINISH_TILE = 512


def _head_masks():
    lane = lax.broadcasted_iota(jnp.int32, (1, 2 * NA_DH), 1)
    return lane < NA_DH, lane >= NA_DH


def _na_kernel(rows, q_ref, k_ref, v_ref, kc_ref, vc_ref, bm_ref, o_ref, sctx_ref, pctx_ref, oloc_ref, den_ref):
    masks = _head_masks()
    t = rows * GRID_W
    n_groups = rows // NA_GROUP
    gq = NA_GROUP * GRID_W
    band = NA_BAND * GRID_W
    kc = kc_ref[0]
    vc = vc_ref[0]

    for hh in range(2):
        q_all = q_ref[0]
        sctx_ref[hh] = _dg(jnp.where(masks[hh], q_all, jnp.zeros_like(q_all)), kc, NT)

    def group(g, carry):
        b0 = jnp.clip(NA_GROUP * g - NA_WIN_H // 2, 0, rows - NA_BAND)
        kind = jnp.where(g == 0, 0, jnp.where(g == n_groups - 1, 2, 1))
        qrows = pl.ds(pl.multiple_of(g * gq, gq), gq)
        krows = pl.ds(pl.multiple_of(b0 * GRID_W, GRID_W), band)
        q = q_ref[0, qrows, :]
        kb = k_ref[0, krows, :]
        vb = v_ref[0, krows, :]
        for hh in range(2):
            qm = jnp.where(masks[hh], q, jnp.zeros_like(q))
            s_loc = _dg(qm, kb, NT) + bm_ref[hh, kind]
            s_ctx = sctx_ref[hh, qrows, :]
            m = jnp.maximum(jnp.max(s_loc, axis=1, keepdims=True), jnp.max(s_ctx, axis=1, keepdims=True))
            p_loc = jnp.exp(s_loc - m)
            p_ctx = jnp.exp(s_ctx - m)
            den = jnp.sum(p_loc, axis=1, keepdims=True) + jnp.sum(p_ctx, axis=1, keepdims=True)
            pctx_ref[hh, qrows, :] = p_ctx.astype(BF16)
            den_ref[hh, qrows, :] = jnp.broadcast_to(den, (gq, 2 * NA_DH))
            oloc_ref[hh, qrows, :] = _dg(p_loc.astype(BF16), vb)
        return carry

    lax.fori_loop(0, n_groups, group, 0)

    def finish(i, carry):
        trows = pl.ds(pl.multiple_of(i * NA_FINISH_TILE, NA_FINISH_TILE), NA_FINISH_TILE)
        outs = []
        for hh in range(2):
            o = oloc_ref[hh, trows, :] + _dg(pctx_ref[hh, trows, :], vc)
            outs.append(o / den_ref[hh, trows, :])
        o_ref[0, trows, :] = jnp.where(masks[0], outs[0], outs[1]).astype(BF16)
        return carry

    lax.fori_loop(0, t // NA_FINISH_TILE, finish, 0)


def _na_latent(nq, nk, nv, nkc, nvc, bias):
    b, t, _ = nq.shape
    t_c = nkc.shape[1]
    rows = t // GRID_W
    assert rows % NA_GROUP == 0 and rows >= NA_BAND and t % NA_FINISH_TILE == 0
    tok = lambda tt: pl.BlockSpec((1, tt, 2 * NA_DH), lambda i, j: (i, 0, j))
    return pl.pallas_call(
        functools.partial(_na_kernel, rows),
        grid=(b, NA_HEADS // 2),
        in_specs=[tok(t), tok(t), tok(t), tok(t_c), tok(t_c),
                  pl.BlockSpec((2,) + bias.shape[1:], lambda i, j: (j, 0, 0, 0))],
        out_specs=tok(t),
        out_shape=jax.ShapeDtypeStruct((b, t, NA_W), BF16),
        scratch_shapes=[pltpu.VMEM((2, t, t_c), F32), pltpu.VMEM((2, t, t_c), BF16),
                        pltpu.VMEM((2, t, 2 * NA_DH), F32), pltpu.VMEM((2, t, 2 * NA_DH), F32)],
        compiler_params=_params(("parallel", "parallel")),
        name="na_latent",
    )(nq, nk, nv, nkc, nvc, bias)


def _nactx_kernel(q_ref, k_ref, v_ref, o_ref):
    masks = _head_masks()
    q, k, v = q_ref[0], k_ref[0], v_ref[0]
    outs = []
    for hh in range(2):
        qm = jnp.where(masks[hh], q, jnp.zeros_like(q))
        s = _dg(qm, k, NT)
        p = jnp.exp(s - jnp.max(s, axis=1, keepdims=True))
        outs.append(_dg(p.astype(BF16), v) / jnp.sum(p, axis=1, keepdims=True))
    o_ref[0] = jnp.where(masks[0], outs[0], outs[1]).astype(BF16)


def _na_ctx(nqc, nkc, nvc):
    b, t_c, _ = nqc.shape
    tok = pl.BlockSpec((1, t_c, 2 * NA_DH), lambda i, j: (i, 0, j))
    return pl.pallas_call(
        _nactx_kernel,
        grid=(b, NA_HEADS // 2),
        in_specs=[tok, tok, tok],
        out_specs=tok,
        out_shape=jax.ShapeDtypeStruct((b, t_c, NA_W), BF16),
        compiler_params=_params(("parallel", "parallel")),
        name="na_ctx",
    )(nqc, nkc, nvc)


def _na_bias(rpb):
    c = np.arange(GRID_W)
    win_lo = np.clip(c - NA_WIN_W // 2, 0, GRID_W - NA_WIN_W)
    ok = (c[None, :] >= win_lo[:, None]) & (c[None, :] < win_lo[:, None] + NA_WIN_W)
    span = np.clip(np.arange(2 * GRID_W - 1) - (GRID_W - 1), 1 - NA_WIN_W, NA_WIN_W - 1) + NA_WIN_W - 1
    ext = rpb[:, :, span]
    cols = jnp.stack([ext[:, :, GRID_W - 1 - q: 2 * GRID_W - 1 - q] for q in range(GRID_W)], axis=2)
    cols = jnp.where(ok[None, None], cols, NEG_BIG).transpose(0, 2, 1, 3)
    half = NA_WIN_H // 2
    kinds = [lambda rq: (rq, 0),
             lambda rq: (half + rq, rq),
             lambda rq: (NA_BAND - NA_GROUP + rq, NA_BAND - NA_WIN_H)]
    pieces = []
    for kind in kinds:
        for rq in range(NA_GROUP):
            q_rel, lo = kind(rq)
            first = lo - q_rel + NA_WIN_H - 1
            window = cols[:, :, first:first + NA_WIN_H]
            pad = ((0, 0), (0, 0), (lo, NA_BAND - NA_WIN_H - lo), (0, 0))
            pieces.append(jnp.pad(window, pad, constant_values=NEG_BIG))
    bias = jnp.stack(pieces, axis=1)
    return bias.reshape(NA_HEADS, len(kinds), NA_GROUP * GRID_W, NA_BAND * GRID_W)


def _outproj_kernel(alpha, ml_ref, na_ref, x_ref, g1_ref, w_ref, lw_ref, lb_ref, sc_ref, sh_ref,
                    x1_ref, ut_ref):
    a = jnp.concatenate([ml_ref[0], na_ref[0]], axis=1)
    y = _dg(a, w_ref[...])
    x1 = _layer_norm(alpha * x_ref[0] + g1_ref[0] * y, lw_ref[...], lb_ref[...])
    x1_ref[0] = x1
    ut_ref[...] = (x1 * (1.0 + sc_ref[0]) + sh_ref[0]).T.astype(BF16)


def _outproj(alpha, ml, na, x, g1, w_out, lw, lb, sc2, sh2):
    b, t, d = x.shape
    tm = min(t, TOKEN_TILE)
    blk = lambda w: pl.BlockSpec((1, tm, w), lambda i, j: (i, j, 0))
    per_b = pl.BlockSpec((1, 1, d), lambda i, j: (i, 0, 0))
    const = lambda a: pl.BlockSpec(a.shape, lambda i, j: (0,) * a.ndim)
    tposed = pl.BlockSpec((d, tm), lambda i, j: (0, i * (t // tm) + j))
    return pl.pallas_call(
        functools.partial(_outproj_kernel, alpha),
        grid=(b, t // tm),
        in_specs=[blk(ML_V_W), blk(NA_W), blk(d), per_b, const(w_out), const(lw), const(lb), per_b, per_b],
        out_specs=[blk(d), tposed],
        out_shape=[jax.ShapeDtypeStruct((b, t, d), F32), jax.ShapeDtypeStruct((d, b * t), BF16)],
        compiler_params=_params(("parallel", "parallel")),
        name="outproj",
    )(ml, na, x, g1, w_out, lw, lb, sc2, sh2)


ROUTE_TB = 256
ROUTE_GROUP = 8
CAND_ROWS = 16 + 7 * 8 + 8


def _cand_tables():
    a_idx = np.zeros(CAND_ROWS, np.int64)
    b_idx = np.zeros(CAND_ROWS, np.int64)
    a_idx[0:16], b_idx[0:16] = 0, np.arange(16)
    for a in range(1, 8):
        a_idx[16 + 8 * (a - 1): 24 + 8 * (a - 1)] = a
        b_idx[16 + 8 * (a - 1): 24 + 8 * (a - 1)] = np.arange(8)
    a_idx[72:80], b_idx[72:80] = np.arange(8, 16), 0
    valid = (a_idx + 1) * (b_idx + 1) <= PEER_TOPK
    flat = (a_idx * PEER_TOPK + b_idx).astype(np.float32)
    neg = np.where(valid, 0.0, -np.inf).astype(np.float32)
    tile = lambda v: jnp.asarray(np.tile(v[:, None], (1, ROUTE_TB)), F32)
    return tile(flat), tile(neg)


def _tree(op, xs):
    while len(xs) > 1:
        xs = [op(xs[i], xs[i + 1]) for i in range(0, len(xs) - 1, 2)] + ([xs[-1]] if len(xs) % 2 else [])
    return xs[0]


def _top_rounds(s, orders, order_end):
    n = s.shape[0] // SUBLANES
    vals = [s[SUBLANES * i:SUBLANES * (i + 1)] for i in range(n)]
    ranks = [jnp.full(vals[0].shape, float(PEER_TOPK), F32)] * n
    tops = []
    for k in range(PEER_TOPK):
        m = jnp.max(_tree(jnp.maximum, vals), axis=0, keepdims=True)
        first = _tree(jnp.minimum, [jnp.where(v == m, o, order_end) for v, o in zip(vals, orders)])
        first = jnp.min(first, axis=0, keepdims=True)
        hits = [o == first for o in orders]
        ranks = [jnp.where(h, float(k), r) for h, r in zip(hits, ranks)]
        vals = [jnp.where(h, -jnp.inf, v) for h, v in zip(hits, vals)]
        tops.append(m)
    return jnp.concatenate(ranks, axis=0), tops


def _one_per_round(rank):
    taken = jnp.sum(jnp.where(rank < float(PEER_TOPK), 1.0, 0.0), axis=0, keepdims=True)
    return jnp.max(jnp.abs(taken - float(PEER_TOPK))) == 0.0


def _sort16_network():
    pairs = []

    def merge(lo, n, r):
        step = r * 2
        if step < n:
            merge(lo, n, step)
            merge(lo + r, n, step)
            pairs.extend((i, i + r) for i in range(lo + r, lo + n - r, step))
        else:
            pairs.append((lo, lo + r))

    def sort(lo, n):
        if n > 1:
            sort(lo, n // 2)
            sort(lo + n // 2, n // 2)
            merge(lo, n, 1)

    sort(0, PEER_TOPK)
    return pairs


def _larger_smaller(a, b):
    if b is None:
        return a, None
    if a is None:
        return b, None
    return jnp.maximum(a, b), jnp.minimum(a, b)


def _top_sorted(slabs):
    v = list(slabs) + [None] * (PEER_TOPK - len(slabs))
    for i, j in _sort16_network():
        v[i], v[j] = _larger_smaller(v[i], v[j])
    for shift in (4, 2, 1):
        moved = [None if a is None else pltpu.roll(a, shift, axis=0) for a in v]
        v = [_larger_smaller(v[k], moved[PEER_TOPK - 1 - k])[0] for k in range(PEER_TOPK)]
        for d in (8, 4, 2, 1):
            for i in range(PEER_TOPK):
                if i & d == 0:
                    v[i], v[i + d] = _larger_smaller(v[i], v[i + d])
    return v


def _rank_by_count(s, tops):
    n = s.shape[0] // SUBLANES
    ranks = []
    for i in range(n):
        v = s[SUBLANES * i:SUBLANES * (i + 1)]
        r = jnp.zeros(v.shape, F32)
        for a, t in enumerate(tops):
            r = jnp.where(t > v, float(a + 1), r)
        ranks.append(r)
    return jnp.concatenate(ranks, axis=0)


def _strictly_descending(tops):
    steps = [jnp.where(a > b, 1.0, 0.0) for a, b in zip(tops[:-1], tops[1:])]
    return jnp.min(_tree(jnp.minimum, steps)) == 1.0


def _route_kernel(ut_ref, wq_ref, kh_ref, kl_ref, flat_ref, neg_ref,
                  lim_ref, ea_ref, r2_ref, eb_ref, qt_ref, rank_ref, t_ref, e_ref, pick_ref):
    tb = ut_ref.shape[1]
    nk = PEER_NKEYS
    qt_ref[...] = _dg(wq_ref[...], ut_ref[...])
    sub_iota = lax.broadcasted_iota(jnp.int32, (SUBLANES, tb), 0).astype(F32)
    key_order = [sub_iota + float(SUBLANES * i) for i in range(nk // SUBLANES)]

    def sub_scores(i, carry):
        todo = []
        for g in range(ROUTE_GROUP):
            hh = ROUTE_GROUP * i + g
            q = qt_ref[pl.ds(pl.multiple_of(hh * nk, nk), nk), :]
            qh, ql = _split_bf16(q)
            kh = kh_ref[hh]
            s0 = _dg(kh, qh) + (_dg(kh, ql) + _dg(kl_ref[hh], qh))
            tops = _top_sorted([s0[SUBLANES * r:SUBLANES * (r + 1)] for r in range(nk // SUBLANES)])
            rank = _rank_by_count(s0, tops)
            rank_ref[hh] = rank
            t_ref[hh] = jnp.concatenate([t[0:1] for t in tops], axis=0)
            e_ref[hh] = jnp.exp(s0 - tops[0][0:1])
            todo.append((hh, s0, jnp.logical_and(_strictly_descending(tops), _one_per_round(rank))))

        @pl.when(jnp.logical_not(functools.reduce(jnp.logical_and, [ok for _, _, ok in todo])))
        def _():
            for hh, s0, _ in todo:
                rank_t, tops_t = _top_rounds(s0, key_order, float(nk))
                rank_ref[hh] = rank_t
                t_ref[hh] = jnp.concatenate(tops_t, axis=0)

        return carry

    lax.fori_loop(0, 2 * PEER_HEADS // ROUTE_GROUP, sub_scores, 0)

    def joint(i, carry):
        todo = []
        for g in range(ROUTE_GROUP):
            h = ROUTE_GROUP * i + g
            t1 = t_ref[2 * h]
            t2 = t_ref[2 * h + 1]
            blocks = [t1[0:1] + t2]
            for a in range(1, 8):
                blocks.append(t1[a:a + 1] + t2[0:8])
            blocks.append(t1[8:16] + t2[0:1])
            cand = jnp.concatenate(blocks, axis=0) + neg_ref[...]
            cut = _top_sorted([cand[SUBLANES * r:SUBLANES * (r + 1)] for r in range(CAND_ROWS // SUBLANES)])[-1]
            pick = jnp.where(cand >= cut[0:1], 0.0, float(PEER_TOPK))
            pick_ref[g] = pick
            todo.append((h, cand, t1[0:1] + t2[0:1], _one_per_round(pick)))

        @pl.when(jnp.logical_not(functools.reduce(jnp.logical_and, [ok for _, _, _, ok in todo])))
        def _():
            flat = [flat_ref[SUBLANES * r:SUBLANES * (r + 1), :] for r in range(CAND_ROWS // SUBLANES)]
            for g, (_, cand, _, _) in enumerate(todo):
                pick_ref[g] = _top_rounds(cand, flat, 1e9)[0]

        for g, (h, cand, top, _) in enumerate(todo):
            chosen = jnp.where(pick_ref[g] < float(PEER_TOPK), 1.0, 0.0)
            z = jnp.sum(chosen * jnp.exp(cand - top), axis=0, keepdims=True)
            counts = [jnp.sum(chosen[0:16], axis=0, keepdims=True)]
            for a in range(1, 8):
                counts.append(jnp.sum(chosen[16 + 8 * (a - 1): 24 + 8 * (a - 1)], axis=0, keepdims=True))
            tail = chosen[72:80]
            for a in range(8):
                counts.append(tail[a:a + 1])
            rank1 = rank_ref[2 * h].astype(BF16)
            lim = jnp.zeros((nk, tb), BF16)
            for a in range(PEER_TOPK):
                lim = jnp.where(rank1 == float(a), jnp.broadcast_to(counts[a], (nk, tb)).astype(BF16), lim)
            lim_ref[h] = lim.astype(F32)
            ea_ref[h] = e_ref[2 * h] * (1.0 / z)
            r2_ref[h] = rank_ref[2 * h + 1].astype(BF16)
            eb_ref[h] = e_ref[2 * h + 1].astype(BF16)
        return carry

    lax.fori_loop(0, PEER_HEADS // ROUTE_GROUP, joint, 0)


def _route(ut, wq, kh, kl):
    d, n = ut.shape
    tb = ROUTE_TB
    flat, neg = _cand_tables()
    nk = PEER_NKEYS
    const = lambda a: pl.BlockSpec(a.shape, lambda i: (0,) * a.ndim)
    tok = pl.BlockSpec((d, tb), lambda i: (0, i))
    out_blk = pl.BlockSpec((PEER_HEADS, nk, tb), lambda i: (0, 0, i))
    meta = lambda dt: jax.ShapeDtypeStruct((PEER_HEADS, nk, n), dt)
    return pl.pallas_call(
        _route_kernel,
        grid=(n // tb,),
        in_specs=[tok, const(wq), const(kh), const(kl), const(flat), const(neg)],
        out_specs=[out_blk, out_blk, out_blk, out_blk],
        out_shape=[meta(F32), meta(F32), meta(BF16), meta(BF16)],
        scratch_shapes=[
            pltpu.VMEM((2 * PEER_HEADS * nk, tb), F32),
            pltpu.VMEM((2 * PEER_HEADS, nk, tb), F32),
            pltpu.VMEM((2 * PEER_HEADS, PEER_TOPK, tb), F32),
            pltpu.VMEM((2 * PEER_HEADS, nk, tb), F32),
            pltpu.VMEM((ROUTE_GROUP, CAND_ROWS, tb), F32),
        ],
        compiler_params=_params(("parallel",)),
        name="peer_route",
    )(ut, wq, kh, kl, flat, neg)


PEER_TB = 512
PEER_EB = 2048


def _gelu_tanh(x):
    return 0.5 * x * (1.0 + jnp.tanh(0.7978845608028654 * (x + 0.044715 * (x * x * x))))


def _peer_kernel(alpha, n_eb, ut_ref, up_ref, dnt_ref, lim_ref, ea_ref, r2_ref, eb_ref,
                 x1_ref, g2_ref, lw_ref, lb_ref, out_ref, acc_ref, z_ref):
    e = pl.program_id(1)
    nk = PEER_NKEYS
    tb = ut_ref.shape[1]

    @pl.when(e == 0)
    def _():
        acc_ref[...] = jnp.zeros(acc_ref.shape, F32)

    ht = _dg(up_ref[...], ut_ref[...])
    zero = jnp.zeros((nk, tb), BF16)
    for j in range(PEER_EB // nk):
        act = _gelu_tanh(ht[j * nk:(j + 1) * nk].astype(BF16))
        gate = None
        for h in range(PEER_HEADS):
            lim = jnp.broadcast_to(lim_ref[h, j:j + 1, :], (nk, tb)).astype(BF16)
            ea = jnp.broadcast_to(ea_ref[h, j:j + 1, :], (nk, tb)).astype(BF16)
            term = jnp.where(r2_ref[h] < lim, eb_ref[h], zero) * ea
            gate = term if gate is None else gate + term
        z_ref[j * nk:(j + 1) * nk, :] = act * gate
    acc_ref[...] += _dg(dnt_ref[...], z_ref[...])

    @pl.when(e == n_eb - 1)
    def _():
        y = acc_ref[...].T
        out_ref[...] = _layer_norm(alpha * x1_ref[...] + g2_ref[0] * y, lw_ref[...], lb_ref[...])


def _peer_experts(alpha, ut, up, dnt, lim, ea, r2, eb, x1, g2, lw, lb, tokens_per_batch):
    n, d = x1.shape
    tb = min(PEER_TB, tokens_per_batch)
    n_exp = up.shape[0]
    n_eb = n_exp // PEER_EB
    nk = PEER_NKEYS
    keys_per_step = PEER_EB // nk
    tok = pl.BlockSpec((tb, d), lambda i, e: (i, 0))
    row_meta = pl.BlockSpec((PEER_HEADS, keys_per_step, tb), lambda i, e: (0, e, i))
    col_meta = pl.BlockSpec((PEER_HEADS, nk, tb), lambda i, e: (0, 0, i))
    const = lambda a: pl.BlockSpec(a.shape, lambda i, e: (0,) * a.ndim)
    return pl.pallas_call(
        functools.partial(_peer_kernel, alpha, n_eb),
        grid=(n // tb, n_eb),
        in_specs=[pl.BlockSpec((d, tb), lambda i, e: (0, i)),
                  pl.BlockSpec((PEER_EB, d), lambda i, e: (e, 0)),
                  pl.BlockSpec((d, PEER_EB), lambda i, e: (0, e)),
                  row_meta, row_meta, col_meta, col_meta,
                  tok,
                  pl.BlockSpec((1, 1, d), lambda i, e: (i * tb // tokens_per_batch, 0, 0)),
                  const(lw), const(lb)],
        out_specs=tok,
        out_shape=jax.ShapeDtypeStruct((n, d), F32),
        scratch_shapes=[pltpu.VMEM((d, tb), F32), pltpu.VMEM((PEER_EB, tb), BF16)],
        compiler_params=_params(("parallel", "arbitrary")),
        name="peer_experts",
    )(ut, up, dnt, lim, ea, r2, eb, x1, g2, lw, lb)


def kernel(x, c, ctx, c_ctx, ada_w, ada_b, w_in, b_gate, conv_w, ml_norm_w, na_rpb, w_out, ln1_w, ln1_b,
           peer_wq, peer_keys, peer_up, peer_down, ln2_w, ln2_b):
    depth = ada_w.shape[0]
    b, t, d = x.shape
    t_c = ctx.shape[1]
    alpha = (2.0 * depth) ** 0.25
    rows = t // GRID_W

    n_cond = b + 1
    cond = jnp.zeros((-(-n_cond // 8) * 8, d), F32).at[:b].set(c).at[b].set(c_ctx)
    mods = _modulation(cond, ada_w, ada_b)
    cos, sin = _rope_tables(t)
    kscale = jnp.concatenate([jnp.ones((1, ML_QK_W // 2), F32), jnp.full((1, ML_QK_W // 2), ML_DK ** -0.5, F32)], 1)

    g_lo = ML_QK_W + 2 * ML_V_W
    g_hi = g_lo + ML_G_W
    for l in range(depth):
        need_ctx = l < depth - 1
        mod = mods[l].reshape(-1, N_MOD, d)
        lat = [mod[:b, i].reshape(b, 1, d) for i in range(N_MOD)]
        cxm = [jnp.broadcast_to(mod[b, i].reshape(1, 1, d), (b, 1, d)) for i in range(N_MOD)]

        w = w_in[l]
        wm = jnp.concatenate([w[:, :g_lo], w[:, g_hi:]], axis=1).astype(BF16)
        wg = jnp.zeros((d, LANES), F32).at[:, :ML_G_W].set(w[:, g_lo:g_hi])
        wgt = w[:, g_lo:g_hi].T

        qkx, vx, ox, nqx, nkx, nvx, gx, gtx = _inproj(x, lat[1], lat[0], wm, wg, wgt)
        qkc, vc, oc, nqc, nkc, nvc, gc, gtc = _inproj(ctx, cxm[1], cxm[0], wm, wg, wgt)

        ml_x, ml_c = _mlstm(need_ctx, (qkx, vx, ox, gx, gtx), (qkc, vc, oc, gc, gtc),
                            cos, sin, conv_w[l], kscale, b_gate[l], ml_norm_w[l])
        na_x = _na_latent(nqx, nkx, nvx, nkc, nvc, _na_bias(na_rpb[l]))

        wo = w_out[l].astype(BF16)
        lw1, lb1 = ln1_w[l].reshape(1, d), ln1_b[l].reshape(1, d)
        lw2, lb2 = ln2_w[l].reshape(1, d), ln2_b[l].reshape(1, d)
        wq_t = peer_wq[l].T.astype(BF16)
        keys = peer_keys[l].reshape(2 * PEER_HEADS, PEER_NKEYS, -1)
        kh = keys.astype(BF16)
        kl = (keys - kh.astype(F32)).astype(BF16)
        up = peer_up[l].astype(BF16)
        dnt = peer_down[l].T.astype(BF16)

        def channel(xin, ml, na, m, tokens):
            x1, ut = _outproj(alpha, ml, na, xin, m[2], wo, lw1, lb1, m[4], m[3])
            lim, ea, r2, eb = _route(ut, wq_t, kh, kl)
            out = _peer_experts(alpha, ut, up, dnt, lim, ea, r2, eb, x1.reshape(-1, d), m[5], lw2, lb2, tokens)
            return out.reshape(xin.shape)

        x_new = channel(x, ml_x, na_x, lat, t)
        if need_ctx:
            na_c = _na_ctx(nqc, nkc, nvc)
            ctx = channel(ctx, ml_c, na_c, cxm, t_c)
        x = x_new
    return x
```
